```python
import math
import jax
import jax.numpy as jnp
from jax import lax
import numpy as np

D_MODEL = 1024
BATCH = 1
SEQ = 16384
DEPTH = 2
DEC_BATCH = 32
DEC_SEQ = 8
PAST_LEN = 16384
PAGE_SIZE = 128

SSD_HEADS = 16
SSD_HEADDIM = 64
SSD_INNER = SSD_HEADS * SSD_HEADDIM
SSD_GROUPS = 2
SSD_STATE = 128
SSD_CONV = 4
SSD_CHUNK = 128
SSD_CONV_DIM = SSD_INNER + 2 * SSD_GROUPS * SSD_STATE
SSD_COLS = SSD_INNER + SSD_CONV_DIM + SSD_HEADS
RWKV_HEADS = 16
RWKV_HEADDIM = 64
RWKV_DIM = RWKV_HEADS * RWKV_HEADDIM
RWKV_DECAY_LORA = 64
RWKV_AAA_LORA = 64
RWKV_GATE_LORA = 128
RWKV_COLS = 3 * RWKV_DIM + RWKV_DECAY_LORA + RWKV_AAA_LORA + RWKV_GATE_LORA
RWKV_LNX_EPS = 64e-5
NSA_HEADS = 16
NSA_KV_HEADS = 2
NSA_HPG = NSA_HEADS // NSA_KV_HEADS
NSA_HEADDIM = 64
NSA_DIM = NSA_HEADS * NSA_HEADDIM
NSA_COLS = NSA_DIM + 3 * 2 * NSA_KV_HEADS * NSA_HEADDIM + 3 * NSA_HEADS
CMP_BLOCK = 32
CMP_STRIDE = 16
CMP_HIDDEN = 128
SEL_BLOCK = 64
N_SEL = 16
WINDOW = 512
Q_BLOCK = 128
FORCE_BONUS = 1e4
IN_COLS = SSD_COLS + RWKV_COLS + NSA_COLS + 3 * D_MODEL
N_GROUPS = 4
EXPERTS_PER_GROUP = 4
N_EXPERTS = N_GROUPS * EXPERTS_PER_GROUP
TOP_K_IN_GROUP = 2
EXPERT_HIDDEN = 256
DEEPNORM_ALPHA = (2 * DEPTH) ** 0.25
DEEPNORM_BETA = (8 * DEPTH) ** -0.25
LN_EPS = 1e-5
RMS_EPS = 1e-5
NEG_INF = -1e30

kernel_name = 'hybrid_ssd_rwkv7_nsa_hmoe_step'


def layer_norm(x, g, b):
    xf = x.astype(jnp.float32)
    mu = jnp.mean(xf, axis=-1, keepdims=True)
    var = jnp.mean(jnp.square(xf - mu), axis=-1, keepdims=True)
    return ((xf - mu) * lax.rsqrt(var + LN_EPS)).astype(x.dtype) * g + b


def masked_softmax(s, mask):
    s = jnp.where(mask, s.astype(jnp.float32), NEG_INF)
    return jnp.where(mask, jax.nn.softmax(s, axis=-1), 0.0)


def causal_conv(x, state, w, b):
    L = x.shape[1]
    xp = jnp.concatenate([state.astype(x.dtype), x], axis=1)
    y = b + sum(xp[:, k:k + L] * w[k] for k in range(SSD_CONV))
    return y, xp[:, L:]


def ssd_scan(x, dt, a, bm, cm, h0):
    B, L = x.shape[:2]
    q = min(SSD_CHUNK, L)
    nc = -(-L // q)
    pad = nc * q - L

    def padl(z):
        return jnp.pad(z, ((0, 0), (0, pad)) + ((0, 0),) * (z.ndim - 2))

    x, dt, bm, cm = padl(x), padl(dt), padl(bm), padl(cm)
    hg = SSD_HEADS // SSD_GROUPS
    xdt = (x * dt[..., None]).reshape(B, nc, q, SSD_GROUPS, hg, SSD_HEADDIM)
    acum = jnp.cumsum((dt * a).reshape(B, nc, q, SSD_GROUPS, hg), axis=2)
    bc = bm.reshape(B, nc, q, SSD_GROUPS, SSD_STATE)
    cc = cm.reshape(B, nc, q, SSD_GROUPS, SSD_STATE)
    causal = jnp.tril(jnp.ones((q, q), bool))[None, None, :, :, None, None]
    lmat = jnp.exp(jnp.where(causal, acum[:, :, :, None] - acum[:, :, None, :], NEG_INF))
    cb = jnp.einsum('bcign,bcjgn->bcijg', cc, bc)
    y_diag = jnp.einsum('bcijg,bcijgh,bcjghp->bcighp', cb, lmat, xdt)
    states = jnp.einsum('bcjgn,bcjgh,bcjghp->bcghpn', bc, jnp.exp(acum[:, :, -1:] - acum), xdt)

    def step(h, inp):
        st, dec = inp
        return h * dec[..., None, None] + st, h

    h_fin, h_in = lax.scan(step, h0.astype(jnp.float32).reshape(B, SSD_GROUPS, hg, SSD_HEADDIM, SSD_STATE),
                           (jnp.swapaxes(states, 0, 1), jnp.swapaxes(jnp.exp(acum[:, :, -1]), 0, 1)))
    y_off = jnp.einsum('bcign,bcghpn,bcigh->bcighp', cc, jnp.swapaxes(h_in, 0, 1), jnp.exp(acum))
    y = (y_diag + y_off).reshape(B, nc * q, SSD_HEADS, SSD_HEADDIM)[:, :L]
    return y, h_fin.reshape(B, SSD_HEADS, SSD_HEADDIM, SSD_STATE)


def ssd_mixer(cols, conv_st, h0, prm, l):
    B, L = cols.shape[:2]
    z, xbc, dt = jnp.split(cols, [SSD_INNER, SSD_INNER + SSD_CONV_DIM], axis=-1)
    xbc, conv_new = causal_conv(xbc, conv_st, prm['ssd_conv_w'][l], prm['ssd_conv_b'][l])
    xbc = jax.nn.silu(xbc)
    xs, bm, cm = jnp.split(xbc, [SSD_INNER, SSD_INNER + SSD_GROUPS * SSD_STATE], axis=-1)
    xs = xs.reshape(B, L, SSD_HEADS, SSD_HEADDIM)
    bm = bm.reshape(B, L, SSD_GROUPS, SSD_STATE)
    cm = cm.reshape(B, L, SSD_GROUPS, SSD_STATE)
    dt = jax.nn.softplus((dt + prm['ssd_dt_bias'][l]).astype(jnp.float32))
    a = -jnp.exp(prm['ssd_a_log'][l].astype(jnp.float32))
    y, h_new = ssd_scan(xs, dt, a, bm, cm, h0)
    y = y + prm['ssd_d'][l][:, None] * xs
    y = y.reshape(B, L, SSD_INNER) * jax.nn.silu(z)
    yg = y.reshape(B, L, SSD_GROUPS, SSD_INNER // SSD_GROUPS).astype(jnp.float32)
    yg = yg * lax.rsqrt(jnp.mean(yg * yg, axis=-1, keepdims=True) + RMS_EPS)
    return yg.reshape(B, L, SSD_INNER).astype(cols.dtype) * prm['ssd_norm_w'][l], conv_new, h_new


def rwkv_mixer(cols, shift_st, s0, prm, l):
    B, L = cols.shape[:2]
    prev = jnp.concatenate([shift_st[:, None].astype(cols.dtype), cols[:, :-1]], axis=1)
    xm = cols + (prev - cols) * prm['rwkv_mu'][l]
    o1 = 3 * RWKV_DIM + RWKV_DECAY_LORA
    r, k, v, w_lo, a_lo, g_lo = jnp.split(
        xm, [RWKV_DIM, 2 * RWKV_DIM, 3 * RWKV_DIM, o1, o1 + RWKV_AAA_LORA], axis=-1)
    w = prm['rwkv_w0'][l] + jnp.tanh(w_lo) @ prm['rwkv_w2'][l]
    decay = jnp.exp(-jnp.exp(-jax.nn.softplus(-w.astype(jnp.float32)) - 0.5))
    a = jax.nn.sigmoid(prm['rwkv_a0'][l] + a_lo @ prm['rwkv_a2'][l])
    g = jax.nn.sigmoid(g_lo) @ prm['rwkv_g2'][l]

    def heads(t):
        return t.reshape(B, L, RWKV_HEADS, RWKV_HEADDIM).astype(jnp.float32)

    kk = heads(k * prm['rwkv_k_k'][l])
    kk = kk * lax.rsqrt(jnp.maximum(jnp.sum(kk * kk, axis=-1, keepdims=True), 1e-24))
    k = k * (1 + (a - 1) * prm['rwkv_k_a'][l])
    rh, kh, vh, ah, dh = heads(r), heads(k), heads(v), heads(a), heads(decay)

    def step(S, inp):
        r_t, d_t, k_t, v_t, kk_t, b_t = inp
        skk = jnp.einsum('bhvk,bhk->bhv', S, kk_t)
        S = S * d_t[:, :, None, :] - skk[..., None] * b_t[:, :, None, :] + v_t[..., None] * k_t[:, :, None, :]
        return S, jnp.einsum('bhvk,bhk->bhv', S, r_t)

    seq = tuple(jnp.swapaxes(t, 0, 1) for t in (rh, dh, kh, vh, kk, kk * ah))
    s_new, ys = lax.scan(step, s0.astype(jnp.float32), seq)
    y = jnp.swapaxes(ys, 0, 1)
    mu = jnp.mean(y, axis=-1, keepdims=True)
    var = jnp.mean(jnp.square(y - mu), axis=-1, keepdims=True)
    y = ((y - mu) * lax.rsqrt(var + RWKV_LNX_EPS)).reshape(B, L, RWKV_DIM) * prm['rwkv_lnx_w'][l] + prm['rwkv_lnx_b'][l]
    y = y + (jnp.sum(rh * kh * prm['rwkv_r_k'][l], axis=-1, keepdims=True) * vh).reshape(B, L, RWKV_DIM)
    return (y * g).astype(cols.dtype), cols[:, -1], s_new


def compress_blocks(rows, prm, l):
    B, T = rows.shape[:2]
    nseg = -(-T // CMP_STRIDE)
    rows = jnp.pad(rows, ((0, 0), (0, nseg * CMP_STRIDE - T), (0, 0), (0, 0), (0, 0)))
    seg = rows.reshape(B, nseg, CMP_STRIDE, 2, NSA_KV_HEADS, NSA_HEADDIM)
    r = CMP_BLOCK // CMP_STRIDE
    nc = nseg - r + 1
    pe, w1, w2 = prm['cmp_pe'][l], prm['cmp_w1'][l], prm['cmp_w2'][l]
    hid = 0.0
    for i in range(r):
        sl = slice(i * CMP_STRIDE, (i + 1) * CMP_STRIDE)
        pe_i = jnp.transpose(pe[:, sl], (1, 0, 2))[None, None, :, :, None, :]
        hid = hid + jnp.einsum('bcsegd,esdf->bcegf', seg[:, i:i + nc] + pe_i, w1[:, sl])
    comp = jnp.einsum('bcegf,efh->bcegh', jax.nn.silu(hid), w2)
    end = jnp.arange(nc) * CMP_STRIDE + CMP_BLOCK - 1
    return comp[:, :, 0], comp[:, :, 1], end


def nsa_block(q, t, gates, ck, cv, c_end, sel_fetch, ns, w_rows, w_pos):
    B, Q = q.shape[:2]
    scale = NSA_HEADDIM ** -0.5
    s = jnp.einsum('bqghd,bcgd->bqghc', q, ck) * scale
    pc = masked_softmax(s, (c_end[None, :] <= t[:, None])[None, :, None, None, :])
    o_cmp = jnp.einsum('bqghc,bcgd->bqghd', pc, cv)
    imp = jnp.sum(pc, axis=3)
    nc = imp.shape[-1]
    ratio = SEL_BLOCK // CMP_STRIDE
    lead = CMP_BLOCK // CMP_STRIDE - 1
    pp = jnp.pad(imp, ((0, 0), (0, 0), (0, 0), (lead, ratio * ns - nc)))
    p_slc = sum(pp[..., r:r + ratio * (ns - 1) + 1:ratio] for r in range(ratio + lead))
    j = jnp.arange(ns)[None, :]
    cur = (t // SEL_BLOCK)[:, None]
    forced = (j == 0) | (j == cur) | (j == cur - 1)
    valid = j * SEL_BLOCK <= t[:, None]
    score = jnp.where(valid[None, :, None, :], p_slc + FORCE_BONUS * forced[None, :, None, :], NEG_INF)
    _, idx = lax.top_k(score, min(N_SEL, ns))
    pos = idx[..., None] * SEL_BLOCK + jnp.arange(SEL_BLOCK)
    kv = sel_fetch(pos)
    n_keys = idx.shape[-1] * SEL_BLOCK
    ks = kv[..., 0, :].reshape(B, Q, NSA_KV_HEADS, n_keys, NSA_HEADDIM)
    vs = kv[..., 1, :].reshape(B, Q, NSA_KV_HEADS, n_keys, NSA_HEADDIM)
    s = jnp.einsum('bqghd,bqgnd->bqghn', q, ks) * scale
    ps = masked_softmax(s, (pos.reshape(B, Q, NSA_KV_HEADS, n_keys) <= t[None, :, None, None])[:, :, :, None, :])
    o_sel = jnp.einsum('bqghn,bqgnd->bqghd', ps, vs)
    s = jnp.einsum('bqghd,bkgd->bqghk', q, w_rows[:, :, 0]) * scale
    dlt = t[:, None] - w_pos[None, :]
    pw = masked_softmax(s, ((dlt >= 0) & (dlt < WINDOW) & (w_pos[None, :] >= 0))[None, :, None, None, :])
    o_win = jnp.einsum('bqghk,bkgd->bqghd', pw, w_rows[:, :, 1])
    return gates[..., 0:1] * o_cmp + gates[..., 1:2] * o_sel + gates[..., 2:3] * o_win


def nsa_mixer(cols, prm, l, pos0, win_buf, cache_cmp, cache_sel, page_table):
    B, L = cols.shape[:2]
    kvw = 6 * NSA_KV_HEADS * NSA_HEADDIM
    q, kv, gl = jnp.split(cols, [NSA_DIM, NSA_DIM + kvw], axis=-1)
    q = q.reshape(B, L, NSA_KV_HEADS, NSA_HPG, NSA_HEADDIM)
    kv = kv.reshape(B, L, 3, 2, NSA_KV_HEADS, NSA_HEADDIM)
    rows_cmp, rows_sel, rows_win = kv[:, :, 0], kv[:, :, 1], kv[:, :, 2]
    gates = jax.nn.sigmoid(gl).reshape(B, L, NSA_KV_HEADS, NSA_HPG, 3)
    if cache_cmp is None:
        full_cmp = rows_cmp
    else:
        past = cache_cmp[l, page_table].reshape(B, pos0, 2, NSA_KV_HEADS, NSA_HEADDIM)
        full_cmp = jnp.concatenate([past.astype(rows_cmp.dtype), rows_cmp], axis=1)
    ck, cv, c_end = compress_blocks(full_cmp, prm, l)
    ns = -(-(pos0 + L) // SEL_BLOCK)
    b_ix = jnp.arange(B)[:, None, None, None, None]
    g_ix = jnp.arange(NSA_KV_HEADS)[None, None, :, None, None]

    def sel_fetch(pos):
        new_kv = rows_sel[b_ix, jnp.clip(pos - pos0, 0, L - 1), :, g_ix]
        if cache_sel is None:
            return new_kv
        page = page_table[b_ix, jnp.clip(pos // PAGE_SIZE, 0, page_table.shape[1] - 1)]
        past_kv = cache_sel[l, page, pos % PAGE_SIZE, :, g_ix]
        return jnp.where((pos >= pos0)[..., None, None], new_kv, past_kv.astype(new_kv.dtype))

    if win_buf is None:
        win_all = rows_win
    else:
        win_all = jnp.concatenate([win_buf.astype(rows_win.dtype), rows_win], axis=1)
    n_all = win_all.shape[1]
    new_win = win_all[:, n_all - min(WINDOW, n_all):]
    if win_buf is None:
        nb = L // Q_BLOCK
        win_pad = jnp.pad(rows_win, ((0, 0), (WINDOW, 0), (0, 0), (0, 0), (0, 0)))

        def body(inp):
            qb, gb, s0 = inp
            w_rows = lax.dynamic_slice_in_dim(win_pad, s0, WINDOW + Q_BLOCK, axis=1)
            w_pos = pos0 + s0 - WINDOW + jnp.arange(WINDOW + Q_BLOCK)
            return nsa_block(qb, pos0 + s0 + jnp.arange(Q_BLOCK), gb, ck, cv, c_end, sel_fetch, ns, w_rows, w_pos)

        qs = jnp.swapaxes(q.reshape(B, nb, Q_BLOCK, NSA_KV_HEADS, NSA_HPG, NSA_HEADDIM), 0, 1)
        gs = jnp.swapaxes(gates.reshape(B, nb, Q_BLOCK, NSA_KV_HEADS, NSA_HPG, 3), 0, 1)
        o = jnp.swapaxes(lax.map(body, (qs, gs, jnp.arange(nb) * Q_BLOCK)), 0, 1)
    else:
        w_pos = pos0 - win_buf.shape[1] + jnp.arange(n_all)
        o = nsa_block(q, pos0 + jnp.arange(L), gates, ck, cv, c_end, sel_fetch, ns, win_all, w_pos)
    return o.reshape(B, L, NSA_DIM), rows_cmp, rows_sel, new_win


def hier_moe(u, prm, l):
    B, L = u.shape[:2]
    lg = (u @ prm['w_group'][l]).astype(jnp.float32)
    pg = jax.nn.softmax(lg, axis=-1)
    g_star = jnp.argmax(lg, axis=-1)
    pg_star = jnp.take_along_axis(pg, g_star[..., None], axis=-1)
    le = (u @ prm['w_router'][l]).astype(jnp.float32).reshape(B, L, N_GROUPS, EXPERTS_PER_GROUP)
    le = jnp.take_along_axis(le, g_star[..., None, None], axis=2)[:, :, 0]
    tv, ti = lax.top_k(le, TOP_K_IN_GROUP)
    wts = jax.nn.softmax(tv, axis=-1) * pg_star
    eid = g_star[..., None] * EXPERTS_PER_GROUP + ti
    gate = jnp.sum(jax.nn.one_hot(eid, N_EXPERTS, dtype=u.dtype) * wts[..., None].astype(u.dtype), axis=-2)
    h = jax.nn.silu(jnp.einsum('bld,edh->bleh', u, prm['moe_w1'][l])) * jnp.einsum('bld,edh->bleh', u, prm['moe_w3'][l])
    return jnp.einsum('bleh,ble,ehd->bld', h, gate, prm['moe_w2'][l])


def trunk_layer(x, c, prm, l, pos0, conv_st, ssm_st, shift_st, rwkv_st, win_buf, cache_cmp, cache_sel, page_table):
    mod = jax.nn.silu(c) @ prm['w_ada'][l] + prm['b_ada'][l]
    sh1, sc1, g1, sh2, sc2, g2 = jnp.split(mod[:, None, :], 6, axis=-1)
    u = x * (1 + sc1) + sh1
    cols = u @ prm['w_in'][l]
    c_ssd, c_rwkv, c_nsa, c_gate = jnp.split(
        cols, [SSD_COLS, SSD_COLS + RWKV_COLS, SSD_COLS + RWKV_COLS + NSA_COLS], axis=-1)
    y_a, conv_new, ssm_new = ssd_mixer(c_ssd, conv_st, ssm_st, prm, l)
    y_b, shift_new, rwkv_new = rwkv_mixer(c_rwkv, shift_st, rwkv_st, prm, l)
    y_c, cmp_rows, sel_rows, win_new = nsa_mixer(c_nsa, prm, l, pos0, win_buf, cache_cmp, cache_sel, page_table)
    ga, gb, gc = jnp.split(jax.nn.sigmoid(c_gate), 3, axis=-1)
    merged = ga * (y_a @ prm['w_o_ssd'][l]) + gb * (y_b @ prm['w_o_rwkv'][l]) + gc * (y_c @ prm['w_o_nsa'][l])
    x = layer_norm(DEEPNORM_ALPHA * x + g1 * (merged @ prm['w_out'][l]), prm['ln1_g'][l], prm['ln1_b'][l])
    u = x * (1 + sc2) + sh2
    x = layer_norm(DEEPNORM_ALPHA * x + g2 * hier_moe(u, prm, l), prm['ln2_g'][l], prm['ln2_b'][l])
    return x, (cmp_rows, sel_rows, win_new, ssm_new, conv_new, rwkv_new, shift_new)


def setup_inputs(seed: int = 0) -> dict:
    key = jax.random.key(seed)
    keys = iter(jax.random.split(key, 80))
    f32 = jnp.float32

    def nrm(shape, scale=1.0):
        return scale * jax.random.normal(next(keys), shape, f32)

    def unif(shape, lo, hi):
        return jax.random.uniform(next(keys), shape, f32, lo, hi)

    n_pages = PAST_LEN // PAGE_SIZE
    n_used = DEC_BATCH * n_pages
    n_phys = (5 * n_used + 3) // 4
    win_buf = min(WINDOW, PAST_LEN)
    D = D_MODEL
    kvrow = (2, NSA_KV_HEADS, NSA_HEADDIM)
    dt = jnp.exp(unif((DEPTH, SSD_HEADS), math.log(1e-3), math.log(1e-1)))
    page_table = jax.random.permutation(next(keys), n_phys)[:n_used].reshape(DEC_BATCH, n_pages).astype(jnp.int32)
    return {
        'x_prompt': nrm((BATCH, SEQ, D)),
        'x_sample': nrm((DEC_BATCH, DEC_SEQ, D)),
        'c_prompt': nrm((BATCH, D)),
        'c_sample': nrm((DEC_BATCH, D)),
        'cache_cmp': nrm((DEPTH, n_phys, PAGE_SIZE) + kvrow),
        'cache_sel': nrm((DEPTH, n_phys, PAGE_SIZE) + kvrow),
        'cache_win': nrm((DEPTH, DEC_BATCH, win_buf) + kvrow),
        'state_ssm': nrm((DEPTH, DEC_BATCH, SSD_HEADS, SSD_HEADDIM, SSD_STATE), 0.1),
        'state_ssm_conv': nrm((DEPTH, DEC_BATCH, SSD_CONV - 1, SSD_CONV_DIM)),
        'state_rwkv': nrm((DEPTH, DEC_BATCH, RWKV_HEADS, RWKV_HEADDIM, RWKV_HEADDIM), 0.1),
        'state_rwkv_shift': nrm((DEPTH, DEC_BATCH, RWKV_COLS)),
        'page_table': page_table,
        'w_ada': nrm((DEPTH, D, 6 * D), 0.5 * D ** -0.5),
        'b_ada': nrm((DEPTH, 6 * D), 0.01),
        'w_in': nrm((DEPTH, D, IN_COLS), D ** -0.5),
        'ssd_conv_w': nrm((DEPTH, SSD_CONV, SSD_CONV_DIM), SSD_CONV ** -0.5),
        'ssd_conv_b': nrm((DEPTH, SSD_CONV_DIM), 0.01),
        'ssd_dt_bias': dt + jnp.log(-jnp.expm1(-dt)),
        'ssd_a_log': jnp.log(unif((DEPTH, SSD_HEADS), 1.0, 16.0)),
        'ssd_d': 1.0 + nrm((DEPTH, SSD_HEADS), 0.01),
        'ssd_norm_w': 1.0 + nrm((DEPTH, SSD_INNER), 0.01),
        'rwkv_mu': unif((DEPTH, RWKV_COLS), 0.0, 1.0),
        'rwkv_w0': unif((DEPTH, RWKV_DIM), -6.0, -1.0),
        'rwkv_w2': nrm((DEPTH, RWKV_DECAY_LORA, RWKV_DIM), 0.1 * RWKV_DECAY_LORA ** -0.5),
        'rwkv_a0': nrm((DEPTH, RWKV_DIM), 0.1),
        'rwkv_a2': nrm((DEPTH, RWKV_AAA_LORA, RWKV_DIM), RWKV_AAA_LORA ** -0.5),
        'rwkv_g2': nrm((DEPTH, RWKV_GATE_LORA, RWKV_DIM), RWKV_GATE_LORA ** -0.5),
        'rwkv_k_k': 0.85 + nrm((DEPTH, RWKV_DIM), 0.01),
        'rwkv_k_a': 1.0 + nrm((DEPTH, RWKV_DIM), 0.01),
        'rwkv_r_k': nrm((DEPTH, RWKV_HEADS, RWKV_HEADDIM), 0.1),
        'rwkv_lnx_w': 1.0 + nrm((DEPTH, RWKV_DIM), 0.01),
        'rwkv_lnx_b': nrm((DEPTH, RWKV_DIM), 0.01),
        'cmp_pe': nrm((DEPTH, 2, CMP_BLOCK, NSA_HEADDIM), 0.1),
        'cmp_w1': nrm((DEPTH, 2, CMP_BLOCK, NSA_HEADDIM, CMP_HIDDEN), (CMP_BLOCK * NSA_HEADDIM) ** -0.5),
        'cmp_w2': nrm((DEPTH, 2, CMP_HIDDEN, NSA_HEADDIM), CMP_HIDDEN ** -0.5),
        'w_o_ssd': nrm((DEPTH, SSD_INNER, D), SSD_INNER ** -0.5),
        'w_o_rwkv': nrm((DEPTH, RWKV_DIM, D), RWKV_DIM ** -0.5),
        'w_o_nsa': nrm((DEPTH, NSA_DIM, D), NSA_DIM ** -0.5),
        'w_out': nrm((DEPTH, D, D), DEEPNORM_BETA * D ** -0.5),
        'ln1_g': 1.0 + nrm((DEPTH, D), 0.01),
        'ln1_b': nrm((DEPTH, D), 0.01),
        'ln2_g': 1.0 + nrm((DEPTH, D), 0.01),
        'ln2_b': nrm((DEPTH, D), 0.01),
        'w_group': nrm((DEPTH, D, N_GROUPS), D ** -0.5),
        'w_router': nrm((DEPTH, D, N_EXPERTS), D ** -0.5),
        'moe_w1': nrm((DEPTH, N_EXPERTS, D, EXPERT_HIDDEN), D ** -0.5),
        'moe_w3': nrm((DEPTH, N_EXPERTS, D, EXPERT_HIDDEN), D ** -0.5),
        'moe_w2': nrm((DEPTH, N_EXPERTS, EXPERT_HIDDEN, D), DEEPNORM_BETA * EXPERT_HIDDEN ** -0.5),
    }


def reference(x_prompt, x_sample, c_prompt, c_sample, cache_cmp, cache_sel, cache_win, state_ssm, state_ssm_conv,
              state_rwkv, state_rwkv_shift, page_table, w_ada, b_ada, w_in, ssd_conv_w, ssd_conv_b, ssd_dt_bias,
              ssd_a_log, ssd_d, ssd_norm_w, rwkv_mu, rwkv_w0, rwkv_w2, rwkv_a0, rwkv_a2, rwkv_g2, rwkv_k_k, rwkv_k_a,
              rwkv_r_k, rwkv_lnx_w, rwkv_lnx_b, cmp_pe, cmp_w1, cmp_w2, w_o_ssd, w_o_rwkv, w_o_nsa, w_out, ln1_g,
              ln1_b, ln2_g, ln2_b, w_group, w_router, moe_w1, moe_w3, moe_w2):
    prm = dict(w_ada=w_ada, b_ada=b_ada, w_in=w_in, ssd_conv_w=ssd_conv_w, ssd_conv_b=ssd_conv_b,
               ssd_dt_bias=ssd_dt_bias, ssd_a_log=ssd_a_log, ssd_d=ssd_d, ssd_norm_w=ssd_norm_w, rwkv_mu=rwkv_mu,
               rwkv_w0=rwkv_w0, rwkv_w2=rwkv_w2, rwkv_a0=rwkv_a0, rwkv_a2=rwkv_a2, rwkv_g2=rwkv_g2,
               rwkv_k_k=rwkv_k_k, rwkv_k_a=rwkv_k_a, rwkv_r_k=rwkv_r_k, rwkv_lnx_w=rwkv_lnx_w,
               rwkv_lnx_b=rwkv_lnx_b, cmp_pe=cmp_pe, cmp_w1=cmp_w1, cmp_w2=cmp_w2, w_o_ssd=w_o_ssd,
               w_o_rwkv=w_o_rwkv, w_o_nsa=w_o_nsa, w_out=w_out, ln1_g=ln1_g, ln1_b=ln1_b, ln2_g=ln2_g,
               ln2_b=ln2_b, w_group=w_group, w_router=w_router, moe_w1=moe_w1, moe_w3=moe_w3, moe_w2=moe_w2)
    bp = x_prompt.shape[0]
    past_len = page_table.shape[1] * PAGE_SIZE
    conv0 = jnp.zeros((bp, SSD_CONV - 1, SSD_CONV_DIM), x_prompt.dtype)
    ssm0 = jnp.zeros((bp, SSD_HEADS, SSD_HEADDIM, SSD_STATE), jnp.float32)
    rwkv0 = jnp.zeros((bp, RWKV_HEADS, RWKV_HEADDIM, RWKV_HEADDIM), jnp.float32)
    shift0 = jnp.zeros((bp, RWKV_COLS), x_prompt.dtype)
    xp, xs = x_prompt, x_sample
    st_p, st_s = [], []
    for l in range(DEPTH):
        xp, sp_l = trunk_layer(xp, c_prompt, prm, l, 0, conv0, ssm0, shift0, rwkv0, None, None, None, None)
        xs, ss_l = trunk_layer(xs, c_sample, prm, l, past_len, state_ssm_conv[l], state_ssm[l], state_rwkv_shift[l],
                               state_rwkv[l], cache_win[l], cache_cmp, cache_sel, page_table)
        st_p.append(sp_l)
        st_s.append(ss_l)
    sp = [jnp.stack(z) for z in zip(*st_p)]
    ss = [jnp.stack(z) for z in zip(*st_s)]
    return (xp, xs, sp[0], sp[1], sp[2], sp[3], sp[4], sp[5], sp[6], ss[0], ss[1], ss[2], ss[3], ss[4], ss[5], ss[6])
```

```python
import functools
import math

import jax
import jax.numpy as jnp
from jax import lax
from jax.experimental import pallas as pl
from jax.experimental.pallas import tpu as pltpu

F32 = jnp.float32
BF16 = jnp.bfloat16
HIGHEST = lax.Precision.HIGHEST

D_MODEL = 1024
DEPTH = 2
PAGE = 128
SSD_HEADS, SSD_HD, SSD_GROUPS, SSD_STATE, SSD_CONV = 16, 64, 2, 128, 4
SSD_INNER = SSD_HEADS * SSD_HD
SSD_CONV_DIM = SSD_INNER + 2 * SSD_GROUPS * SSD_STATE
SSD_COLS = SSD_INNER + SSD_CONV_DIM + SSD_HEADS
RWKV_HEADS, RWKV_HD = 16, 64
RWKV_DIM = RWKV_HEADS * RWKV_HD
RWKV_COLS = 3 * RWKV_DIM + 64 + 64 + 128
RWKV_LNX_EPS = 64e-5
NSA_HEADS, NSA_KVH, NSA_HPG, NSA_HD = 16, 2, 8, 64
NSA_DIM = NSA_HEADS * NSA_HD
NSA_COLS = NSA_DIM + 3 * 2 * NSA_KVH * NSA_HD + 3 * NSA_HEADS
CMP_BLOCK, CMP_STRIDE, CMP_HIDDEN = 32, 16, 128
SEL_BLOCK, N_SEL, WINDOW = 64, 16, 512
FORCE_BONUS = 1e4
N_GROUPS, EPG, N_EXPERTS, EXPERT_HIDDEN = 4, 4, 16, 256
ALPHA = (2 * DEPTH) ** 0.25
LN_EPS = 1e-5
RMS_EPS = 1e-5
NEG = -1e30

LANE = 128
SUBLANE = 8
VMEM_LIMIT = 56 * 1024 * 1024


def _cparams(sem):
    return pltpu.CompilerParams(dimension_semantics=sem, vmem_limit_bytes=VMEM_LIMIT)


def _dot(a, b):
    return jnp.dot(a.astype(BF16), b.astype(BF16), preferred_element_type=F32)


def _dot_hi(a, b):
    return jnp.dot(a, b, precision=HIGHEST, preferred_element_type=F32)


def _dot_nt(a, b):
    return lax.dot_general(a.astype(BF16), b.astype(BF16), (((1,), (1,)), ((), ())),
                           preferred_element_type=F32)


def _dot_nt_hi(a, b):
    return lax.dot_general(a, b, (((1,), (1,)), ((), ())), precision=HIGHEST,
                           preferred_element_type=F32)


def _sigmoid(x):
    return 1.0 / (1.0 + jnp.exp(-x))


def _silu(x):
    return x * _sigmoid(x)


def _softplus(x):
    return jnp.maximum(x, 0.0) + jnp.log(1.0 + jnp.exp(-jnp.abs(x)))


def _iota(shape, dim):
    return lax.broadcasted_iota(jnp.int32, shape, dim)


def _head_onehot(n_heads, hd, pad_rows):
    r = jnp.arange(pad_rows)[:, None]
    c = jnp.arange(n_heads * hd)[None, :] // hd
    return (r == c).astype(F32)


def _ada_kernel(c_ref, w_ref, b_ref, o_ref):
    o_ref[...] = _dot_hi(_silu(c_ref[...]), w_ref[...]) + b_ref[...]


def ada_mod(c_all, w_ada, b_ada):
    nb = c_all.shape[0]
    return pl.pallas_call(
        _ada_kernel,
        grid=(DEPTH, 6),
        in_specs=[pl.BlockSpec((nb, D_MODEL), lambda l, j: (0, 0)),
                  pl.BlockSpec((None, D_MODEL, D_MODEL), lambda l, j: (l, 0, j)),
                  pl.BlockSpec((None, 1, D_MODEL), lambda l, j: (l, 0, j))],
        out_specs=pl.BlockSpec((None, nb, D_MODEL), lambda l, j: (l, 0, j)),
        out_shape=jax.ShapeDtypeStruct((DEPTH, nb, 6 * D_MODEL), F32),
        compiler_params=_cparams(("arbitrary", "arbitrary")),
        name="ada_mod",
    )(c_all, w_ada, b_ada.reshape(DEPTH, 1, 6 * D_MODEL))


def _inproj_kernel(x_ref, sc_ref, sh_ref, w_ref, o_ref):
    u = x_ref[...] * (1.0 + sc_ref[...]) + sh_ref[...]
    o_ref[...] = _dot(u, w_ref[...])


def mod_proj(x, sc, sh, w, tm=256):
    m, k = x.shape
    n = w.shape[1]
    tm = min(tm, m)
    per_row = sc.shape[0] != 1
    mspec = (pl.BlockSpec((tm, k), lambda i: (i, 0)) if per_row
             else pl.BlockSpec((1, k), lambda i: (0, 0)))
    return pl.pallas_call(
        _inproj_kernel,
        grid=(m // tm,),
        in_specs=[pl.BlockSpec((tm, k), lambda i: (i, 0)), mspec, mspec,
                  pl.BlockSpec((k, n), lambda i: (0, 0))],
        out_specs=pl.BlockSpec((tm, n), lambda i: (i, 0)),
        out_shape=jax.ShapeDtypeStruct((m, n), F32),
        compiler_params=_cparams(("arbitrary",)),
        name="mod_proj",
    )(x, sc, sh, w)


SSD_Q = 128
SSD_W = SSD_INNER + SSD_CONV_DIM + LANE


def _ssd_kernel(n_valid, zxd_ref, cst_ref, h0_ref, cw_ref, cb_ref, dtb_ref, alog_ref, dexp_ref,
                nw_ref, e16_ref, tri_ref, y_ref, hout_ref, ext, hT):
    j = pl.program_id(1)
    q = SSD_Q

    @pl.when(j == 0)
    def _():
        ext[0:8, :] = cst_ref[...]
        hT[...] = h0_ref[...]

    ext[8:8 + q, :] = zxd_ref[:, SSD_INNER:SSD_INNER + SSD_CONV_DIM]
    conv = (cb_ref[...] + ext[5:5 + q, :] * cw_ref[0:1, :] + ext[6:6 + q, :] * cw_ref[1:2, :]
            + ext[7:7 + q, :] * cw_ref[2:3, :] + ext[8:8 + q, :] * cw_ref[3:4, :])
    ext[0:8, :] = ext[q:q + 8, :]
    xbc = _silu(conv)
    xs = xbc[:, :SSD_INNER]
    bm = xbc[:, SSD_INNER:SSD_INNER + 2 * SSD_STATE]
    cm = xbc[:, SSD_INNER + 2 * SSD_STATE:]

    row = _iota((q, LANE), 0)
    lane = _iota((q, LANE), 1)
    dt = _softplus(zxd_ref[:, SSD_INNER + SSD_CONV_DIM:] + dtb_ref[...])
    dt = jnp.where((lane < SSD_HEADS) & (row + j * q < n_valid), dt, 0.0)
    a = -jnp.exp(alog_ref[...])
    acum = _dot_hi(tri_ref[...], dt * a)
    a_last = acum[q - 1:q, :]
    e16 = e16_ref[...]
    dt_e = _dot_hi(dt, e16)
    ea_e = _dot_hi(jnp.exp(acum), e16)
    dte_e = _dot_hi(jnp.exp(a_last - acum), e16)
    elast_e = ea_e[q - 1:q, :]
    xdt = xs * dt_e
    acum_t = acum.T
    tril = row >= lane
    lo_half = lane < SSD_HD

    y_tiles = []
    for g in range(SSD_GROUPS):
        cc = cm[:, g * SSD_STATE:(g + 1) * SSD_STATE]
        bc = bm[:, g * SSD_STATE:(g + 1) * SSD_STATE]
        cb = _dot_nt(cc, bc)
        gs = slice(g * 512, (g + 1) * 512)
        h_in = hT[:, gs]
        y_off = _dot(cc, h_in) * ea_e[:, gs]
        hT[:, gs] = h_in * elast_e[:, gs] + _dot(bc.T, xdt[:, gs] * dte_e[:, gs])
        for tl in range(4):
            t = 4 * g + tl
            xt = xdt[:, t * LANE:(t + 1) * LANE]
            yd = jnp.zeros((q, LANE), F32)
            for sub in range(2):
                h = 2 * t + sub
                diff = acum[:, h:h + 1] - acum_t[h:h + 1, :]
                m = cb * jnp.exp(jnp.where(tril, diff, NEG))
                yd = yd + _dot(m, jnp.where(lo_half if sub == 0 else ~lo_half, xt, 0.0))
            y_tiles.append(yd + y_off[:, tl * LANE:(tl + 1) * LANE])
    y = jnp.concatenate(y_tiles, axis=1) + dexp_ref[...] * xs
    y = y * _silu(zxd_ref[:, :SSD_INNER])
    outs = []
    for g in range(SSD_GROUPS):
        yg = y[:, g * 512:(g + 1) * 512]
        ms = jnp.sum(yg * yg, axis=-1, keepdims=True) * (1.0 / 512.0)
        outs.append(yg * lax.rsqrt(ms + RMS_EPS))
    y_ref[...] = jnp.concatenate(outs, axis=1) * nw_ref[...]

    @pl.when(j == pl.num_programs(1) - 1)
    def _():
        hout_ref[...] = hT[...]


def ssd_mixer(zxd, conv_st8, h0t, n_valid, cw8, cb, dtb, alog, dexp, nw):
    b, lp, _ = zxd.shape
    nj = lp // SSD_Q
    e16 = _head_onehot(SSD_HEADS, SSD_HD, LANE)
    tri = (jnp.arange(SSD_Q)[:, None] >= jnp.arange(SSD_Q)[None, :]).astype(F32)
    full = lambda shp: pl.BlockSpec(shp, lambda bi, j: (0,) * len(shp))
    return pl.pallas_call(
        functools.partial(_ssd_kernel, n_valid),
        grid=(b, nj),
        in_specs=[pl.BlockSpec((None, SSD_Q, SSD_W), lambda bi, j: (bi, j, 0)),
                  pl.BlockSpec((None, 8, SSD_CONV_DIM), lambda bi, j: (bi, 0, 0)),
                  pl.BlockSpec((None, SSD_STATE, SSD_INNER), lambda bi, j: (bi, 0, 0)),
                  full((8, SSD_CONV_DIM)), full((1, SSD_CONV_DIM)), full((1, LANE)), full((1, LANE)),
                  full((1, SSD_INNER)), full((1, SSD_INNER)), full((LANE, SSD_INNER)),
                  full((SSD_Q, SSD_Q))],
        out_specs=[pl.BlockSpec((None, SSD_Q, SSD_INNER), lambda bi, j: (bi, j, 0)),
                   pl.BlockSpec((None, SSD_STATE, SSD_INNER), lambda bi, j: (bi, 0, 0))],
        out_shape=[jax.ShapeDtypeStruct((b, lp, SSD_INNER), F32),
                   jax.ShapeDtypeStruct((b, SSD_STATE, SSD_INNER), F32)],
        scratch_shapes=[pltpu.VMEM((SSD_Q + 8, SSD_CONV_DIM), F32),
                        pltpu.VMEM((SSD_STATE, SSD_INNER), F32)],
        compiler_params=_cparams(("arbitrary", "arbitrary")),
        name="ssd_mixer",
    )(zxd, conv_st8, h0t, cw8, cb, dtb, alog, dexp, nw, e16, tri)


def _rwkv_prep_kernel(n_valid, tr, x_ref, sh8_ref, mu_ref, w0_ref, a0_ref, kk_ref, ka_ref, rk_ref,
                      w2_ref, a2_ref, g2_ref, seg_ref, e16_ref,
                      r_o, ld_o, k_o, v_o, kk_o, bb_o, g_o, bv_o, ext):
    j = pl.program_id(1)

    @pl.when(j == 0)
    def _():
        ext[0:8, :] = sh8_ref[...]

    x = x_ref[...]
    ext[8:8 + tr, :] = x
    prev = ext[7:7 + tr, :]
    ext[0:8, :] = ext[tr:tr + 8, :]
    xm = x + (prev - x) * mu_ref[...]
    d = RWKV_DIM
    r, k, v = xm[:, :d], xm[:, d:2 * d], xm[:, 2 * d:3 * d]
    lo = xm[:, 3 * d:3 * d + LANE]
    glo = xm[:, 3 * d + LANE:]
    w = w0_ref[...] + _dot_hi(jnp.tanh(lo), w2_ref[...])
    ld = -jnp.exp(-_softplus(-w) - 0.5)
    a = _sigmoid(a0_ref[...] + _dot_hi(lo, a2_ref[...]))
    g = _dot_hi(_sigmoid(glo), g2_ref[...])
    seg, e16 = seg_ref[...], e16_ref[...]
    kk = k * kk_ref[...]
    ss = _dot_hi(kk * kk, seg)
    kk = kk * _dot_hi(lax.rsqrt(jnp.maximum(ss, 1e-24)), e16)
    k2 = k * (1.0 + (a - 1.0) * ka_ref[...])
    bonus = _dot_hi(_dot_hi(r * k2 * rk_ref[...], seg), e16)
    valid = (_iota((tr, 1), 0) + j * tr) < n_valid
    zero = lambda t: jnp.where(valid, t, 0.0)
    r_o[...] = r
    ld_o[...] = zero(ld)
    k_o[...] = zero(k2)
    v_o[...] = zero(v)
    kk_o[...] = zero(kk)
    bb_o[...] = zero(kk * a)
    g_o[...] = g
    bv_o[...] = bonus * v


def rwkv_prep(cols, shift8, n_valid, prm):
    b, lp, _ = cols.shape
    tr = min(128, lp)
    d = RWKV_DIM
    full = lambda shp: pl.BlockSpec(shp, lambda bi, j: (0,) * len(shp))
    row = lambda: pl.BlockSpec((None, tr, d), lambda bi, j: (bi, j, 0))
    return pl.pallas_call(
        functools.partial(_rwkv_prep_kernel, n_valid, tr),
        grid=(b, lp // tr),
        in_specs=[pl.BlockSpec((None, tr, RWKV_COLS), lambda bi, j: (bi, j, 0)),
                  pl.BlockSpec((None, 8, RWKV_COLS), lambda bi, j: (bi, 0, 0)),
                  full((1, RWKV_COLS))] + [full((1, d))] * 5 + [full((LANE, d))] * 3
                 + [full((d, LANE)), full((LANE, d))],
        out_specs=[row() for _ in range(8)],
        out_shape=[jax.ShapeDtypeStruct((b, lp, d), F32)] * 8,
        scratch_shapes=[pltpu.VMEM((tr + 8, RWKV_COLS), F32)],
        compiler_params=_cparams(("arbitrary", "arbitrary")),
        name="rwkv_prep",
    )(cols, shift8, *prm)


RWKV_C = 64


def _rwkv_scan_kernel(nch, r_ref, ld_ref, k_ref, v_ref, kk_ref, bb_ref, g_ref, bv_ref, s0_ref,
                      lnw_ref, lnb_ref, y_ref, sout_ref, st):
    i = pl.program_id(2)
    c = RWKV_C

    @pl.when(i == 0)
    def _():
        st[...] = s0_ref[...]

    n2 = 2 * c
    rr = _iota((n2, n2), 0)
    cc = _iota((n2, n2), 1)
    eye = (rr == cc).astype(F32)
    upper = rr < cc
    upper_eq = rr <= cc
    tri = (_iota((c, c), 0) >= _iota((c, c), 1)).astype(F32)
    lane = _iota((c, LANE), 1)
    m0 = lane < RWKV_HD
    hmean = jnp.where((rr < RWKV_HD) == (cc < RWKV_HD), 1.0 / RWKV_HD, 0.0)

    def stack(t):
        return jnp.concatenate([jnp.where(m0, t, 0.0), jnp.where(m0, 0.0, t)], axis=0)

    for ci in range(nch):
        sl = pl.ds(ci * c, c)
        r, ld, k, v, kk, bb = r_ref[sl, :], ld_ref[sl, :], k_ref[sl, :], v_ref[sl, :], kk_ref[sl, :], bb_ref[sl, :]
        cum = _dot_hi(tri, ld)
        p_in = jnp.exp(cum)
        p_inv = jnp.exp(-cum)
        p_c = p_in[c - 1:c, :]
        kks = stack(kk * jnp.exp(cum - ld))
        rs = stack(r * p_in)
        bs = stack(bb * p_inv)
        ks = stack(k * p_inv)
        vs = stack(v)
        nt = jnp.where(upper, _dot_nt_hi(bs, kks), 0.0)
        avk = jnp.where(upper, _dot_nt_hi(ks, kks), 0.0)
        arb = jnp.where(upper_eq, _dot_nt_hi(bs, rs), 0.0)
        ark = jnp.where(upper_eq, _dot_nt_hi(ks, rs), 0.0)
        tinv = eye - nt
        pw = nt
        for _ in range(5):
            pw = _dot_hi(pw, pw)
            tinv = tinv + _dot_hi(tinv, pw)
        kks_t, rs_t, vs_t = kks.T, rs.T, vs.T
        s = st[...]
        ut = _dot_hi(_dot_hi(s, kks_t) + _dot_hi(vs_t, avk), tinv)
        yt = _dot_hi(s, rs_t) - _dot_hi(ut, arb) + _dot_hi(vs_t, ark)
        st[...] = s * p_c + _dot_hi(jnp.concatenate([-ut, vs_t], axis=1),
                                    jnp.concatenate([bs * p_c, ks * p_c], axis=0))
        ys = yt.T
        y = ys[:c, :] + ys[c:, :]
        mu = _dot_hi(y, hmean)
        yc = y - mu
        var = _dot_hi(yc * yc, hmean)
        y = yc * lax.rsqrt(var + RWKV_LNX_EPS) * lnw_ref[...] + lnb_ref[...] + bv_ref[sl, :]
        y_ref[sl, :] = y * g_ref[sl, :]

    @pl.when(i == pl.num_programs(2) - 1)
    def _():
        sout_ref[...] = st[...]


def rwkv_scan(r, ld, k, v, kk, bb, g, bv, s0p, lnw, lnb):
    b, lp, d = r.shape
    nch = min(4, lp // RWKV_C)
    rb = nch * RWKV_C
    npair = d // LANE
    row = lambda: pl.BlockSpec((None, rb, LANE), lambda bi, p, i: (bi, i, p))
    st = lambda: pl.BlockSpec((None, None, LANE, LANE), lambda bi, p, i: (bi, p, 0, 0))
    vec = lambda: pl.BlockSpec((1, LANE), lambda bi, p, i: (0, p))
    return pl.pallas_call(
        functools.partial(_rwkv_scan_kernel, nch),
        grid=(b, npair, lp // rb),
        in_specs=[row() for _ in range(8)] + [st(), vec(), vec()],
        out_specs=[row(), st()],
        out_shape=[jax.ShapeDtypeStruct((b, lp, d), F32),
                   jax.ShapeDtypeStruct((b, npair, LANE, LANE), F32)],
        scratch_shapes=[pltpu.VMEM((LANE, LANE), F32)],
        compiler_params=_cparams(("arbitrary", "arbitrary", "arbitrary")),
        name="rwkv_scan",
    )(r, ld, k, v, kk, bb, g, bv, s0p, lnw, lnb)


def _pair_blockdiag(s):
    b = s.shape[0]
    s = s.reshape(b, 8, 2, RWKV_HD, RWKV_HD)
    z = jnp.zeros_like(s[:, :, 0])
    top = jnp.concatenate([s[:, :, 0], z], axis=-1)
    bot = jnp.concatenate([z, s[:, :, 1]], axis=-1)
    return jnp.concatenate([top, bot], axis=-2)


def _pair_unblock(sp):
    b = sp.shape[0]
    return jnp.stack([sp[:, :, :RWKV_HD, :RWKV_HD], sp[:, :, RWKV_HD:, RWKV_HD:]], axis=2).reshape(
        b, RWKV_HEADS, RWKV_HD, RWKV_HD)


CMP_HW = 4 * NSA_KVH * CMP_HIDDEN
NSEG_PAGE = PAGE // CMP_STRIDE


def _cmp_h_rows_kernel(nsr, xk_ref, xv_ref, wk_ref, wv_ref, o_ref):
    half = CMP_HW // 2
    acc_k = jnp.zeros((nsr, half), F32)
    acc_v = jnp.zeros((nsr, half), F32)
    for s in range(CMP_STRIDE):
        rows = pl.ds(s, nsr, stride=CMP_STRIDE)
        acc_k = acc_k + _dot(xk_ref[rows, :], wk_ref[s])
        acc_v = acc_v + _dot(xv_ref[rows, :], wv_ref[s])
    o_ref[...] = jnp.concatenate([acc_k, acc_v], axis=1)


def cmp_h_rows(rows, wk, wv, lt=0):
    b, t, _ = rows.shape
    tb = 2048 if t % 2048 == 0 else t
    nsr = tb // CMP_STRIDE
    wspec = pl.BlockSpec((CMP_STRIDE, LANE, CMP_HW // 2), lambda bi, i: (0, 0, 0))
    return pl.pallas_call(
        functools.partial(_cmp_h_rows_kernel, nsr),
        grid=(b, t // tb),
        in_specs=[pl.BlockSpec((None, tb, LANE), lambda bi, i: (bi, i, lt)),
                  pl.BlockSpec((None, tb, LANE), lambda bi, i: (bi, i, lt + 1)), wspec, wspec],
        out_specs=pl.BlockSpec((None, nsr, CMP_HW), lambda bi, i: (bi, i, 0)),
        out_shape=jax.ShapeDtypeStruct((b, t // CMP_STRIDE, CMP_HW), F32),
        compiler_params=_cparams(("arbitrary", "arbitrary")),
        name="cmp_h_rows",
    )(rows, rows, wk, wv)


PAGES_PER_STEP = 16


def _cmp_h_pages_kernel(pt_ref, *refs):
    pp = PAGES_PER_STEP
    pk, pv = refs[:pp], refs[pp:2 * pp]
    wk_ref, wv_ref, o_ref = refs[2 * pp:]
    half = CMP_HW // 2
    acc_k = jnp.zeros((pp * NSEG_PAGE, half), F32)
    acc_v = jnp.zeros((pp * NSEG_PAGE, half), F32)
    for s in range(CMP_STRIDE):
        rows = pl.ds(s, NSEG_PAGE, stride=CMP_STRIDE)
        xk = jnp.concatenate([r[rows, :] for r in pk], axis=0)
        xv = jnp.concatenate([r[rows, :] for r in pv], axis=0)
        acc_k = acc_k + _dot(xk, wk_ref[s])
        acc_v = acc_v + _dot(xv, wv_ref[s])
    o_ref[...] = jnp.concatenate([acc_k, acc_v], axis=1)


def _page_specs(layer, lane_tile, n_pages):
    pp = PAGES_PER_STEP

    def spec(p):
        def imap(bi, i, pt):
            return (layer, pt[bi, jnp.minimum(i * pp + p, n_pages - 1)], 0, lane_tile)
        return pl.BlockSpec((None, None, PAGE, LANE), imap)
    return [spec(p) for p in range(pp)]


def cmp_h_pages(cache, page_table, layer, wk, wv):
    b, n_pages = page_table.shape
    pp = PAGES_PER_STEP
    wspec = pl.BlockSpec((CMP_STRIDE, LANE, CMP_HW // 2), lambda bi, i, pt: (0, 0, 0))
    gs = pltpu.PrefetchScalarGridSpec(
        num_scalar_prefetch=1, grid=(b, n_pages // pp),
        in_specs=_page_specs(layer, 0, n_pages) + _page_specs(layer, 1, n_pages) + [wspec, wspec],
        out_specs=pl.BlockSpec((None, pp * NSEG_PAGE, CMP_HW), lambda bi, i, pt: (bi, i, 0)))
    return pl.pallas_call(
        _cmp_h_pages_kernel, grid_spec=gs,
        out_shape=jax.ShapeDtypeStruct((b, n_pages * NSEG_PAGE, CMP_HW), F32),
        compiler_params=_cparams(("arbitrary", "arbitrary")),
        name="cmp_h_pages",
    )(page_table, *([cache] * (2 * pp)), wk, wv)


NCP = 1024


def _cmp_finish_kernel(hn_row, hp_ref, hn_ref, pek_ref, pev_ref, wk_ref, wv_ref, w2k_ref, w2v_ref,
                       ck_ref, cvt_ref):
    qw = CMP_HW // 4
    rk = jnp.zeros((8, 2 * qw), F32)
    rv = jnp.zeros((8, 2 * qw), F32)
    for s in range(CMP_STRIDE):
        rk = rk + _dot(pek_ref[s], wk_ref[s])
        rv = rv + _dot(pev_ref[s], wv_ref[s])
    last = _iota((NCP, qw), 0) == hn_row

    def hidden(off, rb):
        h0 = hp_ref[:, off:off + qw]
        h1 = pltpu.roll(hp_ref[:, off + qw:off + 2 * qw], NCP - 1, 0)
        h1 = jnp.where(last, hn_ref[0:1, off + qw:off + 2 * qw], h1)
        return _silu(h0 + h1 + rb[0:1, :qw] + rb[1:2, qw:])

    ck_ref[...] = _dot(hidden(0, rk), w2k_ref[...])
    cvt_ref[...] = _dot(hidden(2 * qw, rv), w2v_ref[...]).T.astype(BF16)


def cmp_finish(hp, hn, hn_row, pek, pev, wk, wv, w2k, w2v):
    b = hp.shape[0]
    full = lambda shp: pl.BlockSpec(shp, lambda bi: (0,) * len(shp))
    return pl.pallas_call(
        functools.partial(_cmp_finish_kernel, hn_row),
        grid=(b,),
        in_specs=[pl.BlockSpec((None, NCP, CMP_HW), lambda bi: (bi, 0, 0)),
                  pl.BlockSpec((None, 8, CMP_HW), lambda bi: (bi, 0, 0)),
                  full((CMP_STRIDE, 8, LANE)), full((CMP_STRIDE, 8, LANE)),
                  full((CMP_STRIDE, LANE, CMP_HW // 2)), full((CMP_STRIDE, LANE, CMP_HW // 2)),
                  full((CMP_HW // 4, LANE)), full((CMP_HW // 4, LANE))],
        out_specs=[pl.BlockSpec((None, NCP, LANE), lambda bi: (bi, 0, 0)),
                   pl.BlockSpec((None, LANE, NCP), lambda bi: (bi, 0, 0))],
        out_shape=[jax.ShapeDtypeStruct((b, NCP, LANE), F32),
                   jax.ShapeDtypeStruct((b, LANE, NCP), BF16)],
        compiler_params=_cparams(("arbitrary",)),
        name="cmp_finish",
    )(hp, hn, pek, pev, wk, wv, w2k, w2v)


def _cmp_weights(pe, w1, w2):
    eye_g = jnp.eye(NSA_KVH, dtype=F32)

    def first(e):
        w = w1[e].reshape(2, CMP_STRIDE, NSA_HD, CMP_HIDDEN)
        w = jnp.einsum('isdf,gh->sgdihf', w, eye_g)
        return w.reshape(CMP_STRIDE, LANE, CMP_HW // 2).astype(BF16)

    def second(e):
        return jnp.einsum('fd,gh->gfhd', w2[e], eye_g).reshape(CMP_HW // 4, LANE).astype(BF16)

    def pos(e):
        p = pe[e].reshape(2, CMP_STRIDE, NSA_HD)
        p = jnp.tile(jnp.transpose(p, (1, 0, 2)), (1, 1, NSA_KVH))
        return jnp.pad(p, ((0, 0), (0, 6), (0, 0)))

    return pos(0), pos(1), first(0), first(1), second(0), second(1)


def _kvprep_rows_kernel(nsub, tk, xk_ref, xv_ref, k_ref, vt_ref):
    k_ref[...] = xk_ref[...].astype(BF16)
    for u in range(nsub):
        vt_ref[u] = xv_ref[u * tk:(u + 1) * tk, :].T.astype(BF16)


def kvprep_rows(rows, tk, lt=0):
    b, t, _ = rows.shape
    tt = 512 if t % 512 == 0 else tk
    nsub = tt // tk
    return pl.pallas_call(
        functools.partial(_kvprep_rows_kernel, nsub, tk),
        grid=(b, t // tt),
        in_specs=[pl.BlockSpec((None, tt, LANE), lambda bi, i: (bi, i, lt)),
                  pl.BlockSpec((None, tt, LANE), lambda bi, i: (bi, i, lt + 1))],
        out_specs=[pl.BlockSpec((None, tt, LANE), lambda bi, i: (bi, i, 0)),
                   pl.BlockSpec((None, nsub, LANE, tk), lambda bi, i: (bi, i, 0, 0))],
        out_shape=[jax.ShapeDtypeStruct((b, t, LANE), BF16),
                   jax.ShapeDtypeStruct((b, t // tk, LANE, tk), BF16)],
        compiler_params=_cparams(("arbitrary", "arbitrary")),
        name="kvprep_rows",
    )(rows, rows)


SEL_TK = 512
WIN_TK = 128


def _kvprep_pages_kernel(n_steps, pt_ref, *refs):
    pp = PAGES_PER_STEP
    pk, pv = refs[:pp], refs[pp:2 * pp]
    nk_ref, nv_ref, k_ref, vt_ref = refs[2 * pp:]
    i = pl.program_id(1)
    per = SEL_TK // PAGE

    @pl.when(i < n_steps - 1)
    def _():
        for p in range(pp):
            k_ref[p * PAGE:(p + 1) * PAGE, :] = pk[p][...].astype(BF16)
            vt_ref[p // per, :, (p % per) * PAGE:(p % per + 1) * PAGE] = pv[p][...].T.astype(BF16)

    @pl.when(i == n_steps - 1)
    def _():
        k_ref[...] = jnp.zeros(k_ref.shape, BF16)
        vt_ref[...] = jnp.zeros(vt_ref.shape, BF16)
        k_ref[0:PAGE, :] = nk_ref[...].astype(BF16)
        vt_ref[0, :, 0:PAGE] = nv_ref[...].T.astype(BF16)


def kvprep_pages(cache, page_table, layer, new_rows, lt0=0):
    b, n_pages = page_table.shape
    pp = PAGES_PER_STEP
    n_steps = n_pages // pp + 1
    t = n_steps * pp * PAGE
    new = lambda lt: pl.BlockSpec((None, PAGE, LANE), lambda bi, i, pt: (bi, 0, lt0 + lt))
    gs = pltpu.PrefetchScalarGridSpec(
        num_scalar_prefetch=1, grid=(b, n_steps),
        in_specs=_page_specs(layer, 0, n_pages) + _page_specs(layer, 1, n_pages) + [new(0), new(1)],
        out_specs=[pl.BlockSpec((None, pp * PAGE, LANE), lambda bi, i, pt: (bi, i, 0)),
                   pl.BlockSpec((None, pp * PAGE // SEL_TK, LANE, SEL_TK), lambda bi, i, pt: (bi, i, 0, 0))])
    return pl.pallas_call(
        functools.partial(_kvprep_pages_kernel, n_steps), grid_spec=gs,
        out_shape=[jax.ShapeDtypeStruct((b, t, LANE), BF16),
                   jax.ShapeDtypeStruct((b, t // SEL_TK, LANE, SEL_TK), BF16)],
        compiler_params=_cparams(("arbitrary", "arbitrary")),
        name="kvprep_pages",
    )(page_table, *([cache] * (2 * pp)), new_rows, new_rows)


TQ = 128
NQL = NSA_HEADS * TQ
NSA_CMP_TILE, NSA_SEL_TILE, NSA_WIN_TILE, NSA_GL_TILE = 8, 10, 12, 14
NSA_W = (NSA_GL_TILE + 1) * LANE
WIN_TILES = WINDOW // WIN_TK + 1


def _rowmax(x):
    return jnp.max(x, axis=0, keepdims=True)


def _rowsum(x):
    return jnp.sum(x, axis=0, keepdims=True)


def _over_heads(v128):
    return jnp.concatenate([v128[:, :TQ]] * NSA_HPG + [v128[:, TQ:]] * NSA_HPG, axis=1)


def _nsa_attn_kernel(pos0, nc, ns, n_sel_tiles, wt0, wpos0, n_win,
                     q_ref, gl_ref, ck_ref, cvt_ref, ks_ref, vst_ref, *rest):
    kw = rest[:WIN_TILES]
    vw = rest[WIN_TILES:2 * WIN_TILES]
    msel_ref, o_ref, st_scr, sel_scr, work_scr, acc_scr, ml_scr = rest[2 * WIN_TILES:]
    i = pl.program_id(1)
    t0 = pos0 + i * TQ
    nsp = sel_scr.shape[0]

    qt = (q_ref[...] * (NSA_HD ** -0.5)).T
    zero = jnp.zeros((NSA_HD, TQ), F32)
    pieces = []
    for jh in range(NSA_HEADS):
        blk = qt[jh * NSA_HD:(jh + 1) * NSA_HD, :]
        pieces.append(jnp.concatenate([blk, zero] if jh < NSA_HPG else [zero, blk], axis=0))
    qbd = jnp.concatenate(pieces, axis=1)
    qbd16 = qbd.astype(BF16)
    t_lane = t0 + (_iota((1, NQL), 1) % TQ)

    nct = NCP // LANE
    m = jnp.full((1, NQL), NEG, F32)
    for ct in range(nct):
        s = _dot_hi(ck_ref[ct * LANE:(ct + 1) * LANE, :], qbd)
        c = ct * LANE + _iota((LANE, NQL), 0)
        s = jnp.where((CMP_STRIDE * c + (CMP_BLOCK - 1) <= t_lane) & (c < nc), s, NEG)
        st_scr[ct * LANE:(ct + 1) * LANE, :] = s
        m = jnp.maximum(m, _rowmax(s))
    l = jnp.zeros((1, NQL), F32)
    acc = jnp.zeros((LANE, NQL), F32)
    for ct in range(nct):
        s = st_scr[ct * LANE:(ct + 1) * LANE, :]
        p = jnp.where(s > 0.5 * NEG, jnp.exp(s - m), 0.0)
        st_scr[ct * LANE:(ct + 1) * LANE, :] = p
        l = l + _rowsum(p)
        acc = acc + _dot(cvt_ref[:, ct * LANE:(ct + 1) * LANE], p)
    inv = 1.0 / jnp.where(l > 0.0, l, 1.0)
    acc_scr[0] = acc * inv
    pslc = jnp.zeros((nsp, 2 * TQ), F32)
    for ct in range(nct):
        pc = st_scr[ct * LANE:(ct + 1) * LANE, :] * inv
        imp = []
        for g in range(NSA_KVH):
            tot = pc[:, g * NSA_HPG * TQ:g * NSA_HPG * TQ + TQ]
            for h in range(1, NSA_HPG):
                tot = tot + pc[:, (g * NSA_HPG + h) * TQ:(g * NSA_HPG + h + 1) * TQ]
            imp.append(tot)
        pslc = pslc + _dot_hi(msel_ref[:, ct * LANE:(ct + 1) * LANE], jnp.concatenate(imp, axis=1))

    jrow = _iota((nsp, 2 * TQ), 0)
    t_gq = t0 + (_iota((1, 2 * TQ), 1) % TQ)
    cur = lax.shift_right_logical(t_gq, 6)
    forced = (jrow == 0) | (jrow == cur) | (jrow == cur - 1)
    score = jnp.where(jrow * SEL_BLOCK <= t_gq, pslc + jnp.where(forced, FORCE_BONUS, 0.0), NEG)
    work_scr[...] = jnp.where(jrow < ns, score, -3e38)
    sel_scr[...] = jnp.zeros(sel_scr.shape, F32)
    jrow_f = jrow.astype(F32)

    def pick(_, carry):
        w = work_scr[...]
        best = _rowmax(w)
        first = jnp.min(jnp.where(w == best, jrow_f, 1e9), axis=0, keepdims=True)
        hit = jrow_f == first
        sel_scr[...] = jnp.where(hit, 1.0, sel_scr[...])
        work_scr[...] = jnp.where(hit, -jnp.inf, w)
        return carry

    lax.fori_loop(0, N_SEL, pick, 0)

    acc_scr[1] = jnp.zeros((LANE, NQL), F32)
    ml_scr[0:1, :] = jnp.full((1, NQL), NEG, F32)
    ml_scr[1:2, :] = jnp.zeros((1, NQL), F32)
    per = SEL_TK // SEL_BLOCK
    krow = _iota((SEL_BLOCK, NQL), 0)

    def sel_tile(kt, carry):
        s = _dot(ks_ref[pl.ds(pl.multiple_of(kt * SEL_TK, SEL_TK), SEL_TK), :], qbd16)
        selrows = sel_scr[pl.ds(pl.multiple_of(kt * per, per), per), :]
        blocks = []
        for jj in range(per):
            on = _over_heads(selrows[jj:jj + 1, :]) > 0.5
            pos = kt * SEL_TK + jj * SEL_BLOCK + krow
            blocks.append(jnp.where(on & (pos <= t_lane), s[jj * SEL_BLOCK:(jj + 1) * SEL_BLOCK, :], NEG))
        s = jnp.concatenate(blocks, axis=0)
        m_old = ml_scr[0:1, :]
        m_new = jnp.maximum(m_old, _rowmax(s))
        alpha = jnp.exp(m_old - m_new)
        p = jnp.where(s > 0.5 * NEG, jnp.exp(s - m_new), 0.0)
        ml_scr[0:1, :] = m_new
        ml_scr[1:2, :] = alpha * ml_scr[1:2, :] + _rowsum(p)
        acc_scr[1] = acc_scr[1] * alpha + _dot(vst_ref[kt], p)
        return carry

    n_tiles = jnp.minimum((t0 + TQ + SEL_TK - 1) // SEL_TK, n_sel_tiles)
    lax.fori_loop(0, n_tiles, sel_tile, 0)
    l_sel = ml_scr[1:2, :]
    acc_scr[1] = acc_scr[1] * (1.0 / jnp.where(l_sel > 0.0, l_sel, 1.0))

    m = jnp.full((1, NQL), NEG, F32)
    l = jnp.zeros((1, NQL), F32)
    acc = jnp.zeros((LANE, NQL), F32)
    wrow = _iota((WIN_TK, NQL), 0)
    for w in range(WIN_TILES):
        tile = i + (wt0 - (WIN_TILES - 1) + w)
        s = _dot(kw[w][...], qbd16)
        idx = tile * WIN_TK + wrow
        wpos = wpos0 + idx
        dlt = t_lane - wpos
        ok = (dlt >= 0) & (dlt < WINDOW) & (wpos >= 0) & (idx < n_win) & (tile >= 0)
        s = jnp.where(ok, s, NEG)
        m_new = jnp.maximum(m, _rowmax(s))
        alpha = jnp.exp(m - m_new)
        p = jnp.where(s > 0.5 * NEG, jnp.exp(s - m_new), 0.0)
        l = alpha * l + _rowsum(p)
        acc = acc * alpha + _dot(vw[w][...], p)
        m = m_new
    acc_scr[2] = acc * (1.0 / jnp.where(l > 0.0, l, 1.0))

    gt = _sigmoid(gl_ref[...]).T
    outs = []
    for jh in range(NSA_HEADS):
        rs = slice((jh // NSA_HPG) * NSA_HD, (jh // NSA_HPG + 1) * NSA_HD)
        ls = slice(jh * TQ, (jh + 1) * TQ)
        o = gt[3 * jh:3 * jh + 1, :] * acc_scr[0, rs, ls]
        o = o + gt[3 * jh + 1:3 * jh + 2, :] * acc_scr[1, rs, ls]
        o = o + gt[3 * jh + 2:3 * jh + 3, :] * acc_scr[2, rs, ls]
        outs.append(o)
    o_ref[...] = jnp.concatenate(outs, axis=0).T


def nsa_attention(cols, ck, cvt, ks, vst, kwin, vwt, pos0, n_new, n_buf, n_win):
    b, lq, _ = cols.shape
    q = gl = cols
    tk_total = ks.shape[1]
    n_sel_tiles = tk_total // SEL_TK
    nsp = tk_total // SEL_BLOCK
    t_total = pos0 + n_new
    nseg = -(-t_total // CMP_STRIDE)
    nc = nseg - CMP_BLOCK // CMP_STRIDE + 1
    ns = -(-t_total // SEL_BLOCK)
    wt0 = n_buf // WIN_TK
    n_wtiles = kwin.shape[1] // WIN_TK
    j = jnp.arange(nsp)[:, None]
    c = jnp.arange(NCP)[None, :]
    msel = ((c >= 4 * j - 1) & (c <= 4 * j + 3)).astype(F32)

    def wspec(w, vt):
        def imap(bi, i):
            tile = jnp.clip(i + (wt0 - (WIN_TILES - 1) + w), 0, n_wtiles - 1)
            return (bi, tile, 0, 0) if vt else (bi, tile, 0)
        return pl.BlockSpec((None, None, LANE, WIN_TK) if vt else (None, WIN_TK, LANE), imap)

    kern = functools.partial(_nsa_attn_kernel, pos0, nc, ns, n_sel_tiles, wt0, pos0 - n_buf, n_win)
    return pl.pallas_call(
        kern,
        grid=(b, lq // TQ),
        in_specs=[pl.BlockSpec((None, TQ, NSA_DIM), lambda bi, i: (bi, i, 0)),
                  pl.BlockSpec((None, TQ, LANE), lambda bi, i: (bi, i, NSA_GL_TILE)),
                  pl.BlockSpec((None, NCP, LANE), lambda bi, i: (bi, 0, 0)),
                  pl.BlockSpec((None, LANE, NCP), lambda bi, i: (bi, 0, 0)),
                  pl.BlockSpec((None, tk_total, LANE), lambda bi, i: (bi, 0, 0)),
                  pl.BlockSpec((None, n_sel_tiles, LANE, SEL_TK), lambda bi, i: (bi, 0, 0, 0))]
                 + [wspec(w, False) for w in range(WIN_TILES)]
                 + [wspec(w, True) for w in range(WIN_TILES)]
                 + [pl.BlockSpec((nsp, NCP), lambda bi, i: (0, 0))],
        out_specs=pl.BlockSpec((None, TQ, NSA_DIM), lambda bi, i: (bi, i, 0)),
        out_shape=jax.ShapeDtypeStruct((b, lq, NSA_DIM), F32),
        scratch_shapes=[pltpu.VMEM((NCP, NQL), F32), pltpu.VMEM((nsp, 2 * TQ), F32),
                        pltpu.VMEM((nsp, 2 * TQ), F32), pltpu.VMEM((3, LANE, NQL), F32),
                        pltpu.VMEM((8, NQL), F32)],
        compiler_params=_cparams(("arbitrary", "arbitrary")),
        name="nsa_attention",
    )(q, gl, ck, cvt, ks, vst, *([kwin] * WIN_TILES), *([vwt] * WIN_TILES), msel)


def _layer_norm(v, g, b):
    mu = jnp.mean(v, axis=-1, keepdims=True)
    vc = v - mu
    var = jnp.mean(vc * vc, axis=-1, keepdims=True)
    return vc * lax.rsqrt(var + LN_EPS) * g + b


def _merge_kernel(x_ref, ya_ref, yb_ref, yc_ref, gate_ref, g1_ref, wa_ref, wb_ref, wc_ref, wo_ref,
                  lg_ref, lb_ref, o_ref):
    d = D_MODEL
    merged = (_sigmoid(gate_ref[:, :d]) * _dot(ya_ref[...], wa_ref[...])
              + _sigmoid(gate_ref[:, d:2 * d]) * _dot(yb_ref[...], wb_ref[...])
              + _sigmoid(gate_ref[:, 2 * d:]) * _dot(yc_ref[...], wc_ref[...]))
    o = _dot(merged, wo_ref[...])
    o_ref[...] = _layer_norm(ALPHA * x_ref[...] + g1_ref[...] * o, lg_ref[...], lb_ref[...])


def merge_out(x, ya, yb, yc, gate, g1, wa, wb, wc, wo, lg, lb, tm=256):
    m, d = x.shape
    tm = min(tm, m)
    per_row = g1.shape[0] != 1
    row = lambda w: pl.BlockSpec((tm, w), lambda i: (i, 0))
    mspec = row(d) if per_row else pl.BlockSpec((1, d), lambda i: (0, 0))
    wspec = pl.BlockSpec((d, d), lambda i: (0, 0))
    vspec = pl.BlockSpec((1, d), lambda i: (0, 0))
    return pl.pallas_call(
        _merge_kernel,
        grid=(m // tm,),
        in_specs=[row(d), row(d), row(d), row(d), row(3 * d), mspec, wspec, wspec, wspec, wspec, vspec, vspec],
        out_specs=row(d),
        out_shape=jax.ShapeDtypeStruct((m, d), F32),
        compiler_params=_cparams(("arbitrary",)),
        name="merge_out",
    )(x, ya, yb, yc, gate, g1, wa, wb, wc, wo, lg, lb)


def _lane_first(mask, lane_f):
    return jnp.min(jnp.where(mask, lane_f, 1e9), axis=-1, keepdims=True)


def _moe_kernel(x_ref, sc_ref, sh_ref, g2_ref, wr_ref, w1_ref, w3_ref, w2_ref, lg_ref, lb_ref,
                o_ref, u_scr, gate_scr, acc_scr):
    e = pl.program_id(1)
    tm = x_ref.shape[0]
    lane = _iota((tm, LANE), 1)
    lane_f = lane.astype(F32)

    @pl.when(e == 0)
    def _():
        u = x_ref[...] * (1.0 + sc_ref[...]) + sh_ref[...]
        u_scr[...] = u.astype(BF16)
        logits = _dot_hi(u, wr_ref[...])
        lg = jnp.where(lane < N_GROUPS, logits, -jnp.inf)
        gmax = jnp.max(lg, axis=-1, keepdims=True)
        gstar = _lane_first(lg == gmax, lane_f)
        pg = 1.0 / jnp.sum(jnp.exp(lg - gmax), axis=-1, keepdims=True)
        in_grp = (lane >= N_GROUPS) & (lane < N_GROUPS + N_EXPERTS) & (
            lax.shift_right_logical(lane - N_GROUPS, 2).astype(F32) == gstar)
        le = jnp.where(in_grp, logits, -jnp.inf)
        v1 = jnp.max(le, axis=-1, keepdims=True)
        i1 = _lane_first(le == v1, lane_f)
        le2 = jnp.where(lane_f == i1, -jnp.inf, le)
        v2 = jnp.max(le2, axis=-1, keepdims=True)
        i2 = _lane_first(le2 == v2, lane_f)
        e2 = jnp.exp(v2 - v1)
        den = 1.0 / (1.0 + e2)
        gate_scr[...] = jnp.where(lane_f == i1, den * pg, jnp.where(lane_f == i2, e2 * den * pg, 0.0))
        acc_scr[...] = jnp.zeros(acc_scr.shape, F32)

    u = u_scr[...]
    ge = jnp.sum(jnp.where(lane == e + N_GROUPS, gate_scr[...], 0.0), axis=-1, keepdims=True)
    h = _silu(_dot(u, w1_ref[...])) * _dot(u, w3_ref[...])
    acc_scr[...] += _dot(h * ge, w2_ref[...])

    @pl.when(e == N_EXPERTS - 1)
    def _():
        o_ref[...] = _layer_norm(ALPHA * x_ref[...] + g2_ref[...] * acc_scr[...], lg_ref[...], lb_ref[...])


def moe_out(x, sc, sh, g2, wr, w1, w3, w2, lg, lb, tm=512):
    m, d = x.shape
    tm = min(tm, m)
    per_row = sc.shape[0] != 1
    row = pl.BlockSpec((tm, d), lambda i, e: (i, 0))
    mspec = row if per_row else pl.BlockSpec((1, d), lambda i, e: (0, 0))
    vspec = pl.BlockSpec((1, d), lambda i, e: (0, 0))
    return pl.pallas_call(
        _moe_kernel,
        grid=(m // tm, N_EXPERTS),
        in_specs=[row, mspec, mspec, mspec, pl.BlockSpec((d, LANE), lambda i, e: (0, 0)),
                  pl.BlockSpec((None, d, EXPERT_HIDDEN), lambda i, e: (e, 0, 0)),
                  pl.BlockSpec((None, d, EXPERT_HIDDEN), lambda i, e: (e, 0, 0)),
                  pl.BlockSpec((None, EXPERT_HIDDEN, d), lambda i, e: (e, 0, 0)), vspec, vspec],
        out_specs=row,
        out_shape=jax.ShapeDtypeStruct((m, d), F32),
        scratch_shapes=[pltpu.VMEM((tm, d), BF16), pltpu.VMEM((tm, LANE), F32), pltpu.VMEM((tm, d), F32)],
        compiler_params=_cparams(("arbitrary", "arbitrary")),
        name="moe_out",
    )(x, sc, sh, g2, wr, w1, w3, w2, lg, lb)


def _pad_rows(a, n):
    return a if a.shape[1] == n else jnp.pad(a, ((0, 0), (0, n - a.shape[1]), (0, 0)))


def _layer_weights(l, p):
    w_in = p['w_in'][l]
    o1, o2, o3 = SSD_COLS, SSD_COLS + RWKV_COLS, SSD_COLS + RWKV_COLS + NSA_COLS
    padc = lambda w, n: jnp.pad(w, ((0, 0), (0, n - w.shape[1])))
    w_ssd = padc(w_in[:, :o1], SSD_W)
    w_nsa = padc(w_in[:, o2:o3], NSA_W)
    z64 = jnp.zeros((64, RWKV_DIM), F32)
    seg = _head_onehot(RWKV_HEADS, RWKV_HD, LANE).T
    rwkv_prm = (p['rwkv_mu'][l][None], p['rwkv_w0'][l][None], p['rwkv_a0'][l][None], p['rwkv_k_k'][l][None],
                p['rwkv_k_a'][l][None], p['rwkv_r_k'][l].reshape(1, RWKV_DIM),
                jnp.concatenate([p['rwkv_w2'][l], z64], 0), jnp.concatenate([z64, p['rwkv_a2'][l]], 0),
                p['rwkv_g2'][l], seg, seg.T)
    wr = jnp.pad(jnp.concatenate([p['w_group'][l], p['w_router'][l]], axis=1),
                 ((0, 0), (0, LANE - N_GROUPS - N_EXPERTS)))
    row = lambda v: v[None]
    return dict(
        w_ssd=w_ssd.astype(BF16), w_rwkv=w_in[:, o1:o2].astype(BF16), w_nsa=w_nsa.astype(BF16),
        w_gate=w_in[:, o3:].astype(BF16),
        ssd=(jnp.pad(p['ssd_conv_w'][l], ((0, 4), (0, 0))), row(p['ssd_conv_b'][l]),
             row(jnp.pad(p['ssd_dt_bias'][l], (0, LANE - SSD_HEADS))), row(jnp.pad(p['ssd_a_log'][l], (0, LANE - SSD_HEADS))),
             row(jnp.repeat(p['ssd_d'][l], SSD_HD)), row(p['ssd_norm_w'][l])),
        rwkv=rwkv_prm, lnx=(row(p['rwkv_lnx_w'][l]), row(p['rwkv_lnx_b'][l])),
        cmp=_cmp_weights(p['cmp_pe'][l], p['cmp_w1'][l], p['cmp_w2'][l]),
        wo=tuple(p[k][l].astype(BF16) for k in ('w_o_ssd', 'w_o_rwkv', 'w_o_nsa', 'w_out')),
        ln1=(row(p['ln1_g'][l]), row(p['ln1_b'][l])), ln2=(row(p['ln2_g'][l]), row(p['ln2_b'][l])),
        wr=wr, w1=p['moe_w1'][l].astype(BF16), w3=p['moe_w3'][l].astype(BF16), w2=p['moe_w2'][l].astype(BF16))


def _trunk_layer(x, mod, l, w, pos0, conv_st, ssm_st, shift_st, rwkv_st, cache_win, cache_cmp, cache_sel,
                 page_table):
    b, L, d = x.shape
    m = b * L
    x2 = x.reshape(m, d)
    sh1, sc1, g1, sh2, sc2, g2 = [mod[:, k * d:(k + 1) * d] for k in range(6)]
    if b > 1:
        sh1, sc1, g1, sh2, sc2, g2 = [jnp.repeat(t, L, axis=0) for t in (sh1, sc1, g1, sh2, sc2, g2)]
    proj = lambda wt: mod_proj(x2, sc1, sh1, wt).reshape(b, L, -1)
    c_ssd, c_rwkv, c_nsa, c_gate = proj(w['w_ssd']), proj(w['w_rwkv']), proj(w['w_nsa']), proj(w['w_gate'])

    lp = -(-L // SSD_Q) * SSD_Q
    cst8 = jnp.pad(conv_st, ((0, 0), (8 - (SSD_CONV - 1), 0), (0, 0)))
    h0t = jnp.transpose(ssm_st.reshape(b, SSD_INNER, SSD_STATE), (0, 2, 1))
    y_a, ht = ssd_mixer(_pad_rows(c_ssd, lp), cst8, h0t, L, *w['ssd'])
    y_a = y_a[:, :L]
    ssm_new = jnp.transpose(ht, (0, 2, 1)).reshape(b, SSD_HEADS, SSD_HD, SSD_STATE)
    conv_new = c_ssd[:, L - (SSD_CONV - 1):, SSD_INNER:SSD_INNER + SSD_CONV_DIM]

    lp = -(-L // RWKV_C) * RWKV_C
    sh8 = jnp.pad(shift_st[:, None, :], ((0, 0), (7, 0), (0, 0)))
    pre = rwkv_prep(_pad_rows(c_rwkv, lp), sh8, L, w['rwkv'])
    y_b, sp = rwkv_scan(*pre, _pair_blockdiag(rwkv_st), *w['lnx'])
    y_b = y_b[:, :L]
    rwkv_new = _pair_unblock(sp)
    shift_new = c_rwkv[:, -1]

    pek, pev, wk, wv, w2k, w2v = w['cmp']
    kvrow = (2, NSA_KVH, NSA_HD)
    rows = lambda tile: c_nsa[:, :, tile * LANE:(tile + 2) * LANE]
    if cache_cmp is None:
        hp = cmp_h_rows(c_nsa, wk, wv, NSA_CMP_TILE)
        hp = _pad_rows(hp, NCP)
        hn = jnp.zeros((b, 8, CMP_HW), F32)
        ks, vst = kvprep_rows(c_nsa, SEL_TK, NSA_SEL_TILE)
        kwin, vwt = kvprep_rows(c_nsa, WIN_TK, NSA_WIN_TILE)
        cq, n_buf, n_win = c_nsa, 0, L
        win_new = rows(NSA_WIN_TILE)[:, max(L - WINDOW, 0):]
    else:
        cq = _pad_rows(c_nsa, TQ)
        hp = _pad_rows(cmp_h_pages(cache_cmp, page_table, l, wk, wv), NCP)
        hn = cmp_h_rows(cq, wk, wv, NSA_CMP_TILE)
        ks, vst = kvprep_pages(cache_sel, page_table, l, cq, NSA_SEL_TILE)
        n_buf = cache_win.shape[1]
        win_all = jnp.concatenate([cache_win.reshape(b, n_buf, 2 * LANE), rows(NSA_WIN_TILE)], axis=1)
        n_win = n_buf + L
        kwin, vwt = kvprep_rows(_pad_rows(win_all, -(-(n_buf + TQ) // WIN_TK) * WIN_TK), WIN_TK)
        win_new = win_all[:, n_win - min(WINDOW, n_win):]
    hn_row = NCP - 1 if cache_cmp is None else pos0 // CMP_STRIDE - 1
    ck, cvt = cmp_finish(hp, hn, hn_row, pek, pev, wk, wv, w2k, w2v)
    y_c = nsa_attention(cq, ck, cvt, ks, vst, kwin, vwt, pos0, L, n_buf, n_win)[:, :L]
    cmp_rows = rows(NSA_CMP_TILE).reshape((b, L) + kvrow)
    sel_rows = rows(NSA_SEL_TILE).reshape((b, L) + kvrow)
    win_new = win_new.reshape(win_new.shape[:2] + kvrow)

    flat = lambda t: t.reshape(m, -1)
    x1 = merge_out(x2, flat(y_a), flat(y_b), flat(y_c), flat(c_gate), g1, *w['wo'], *w['ln1'])
    x_out = moe_out(x1, sc2, sh2, g2, w['wr'], w['w1'], w['w3'], w['w2'], *w['ln2'])
    return x_out.reshape(b, L, d), (cmp_rows, sel_rows, win_new, ssm_new, conv_new, rwkv_new, shift_new)


def kernel(x_prompt, x_sample, c_prompt, c_sample, cache_cmp, cache_sel, cache_win, state_ssm, state_ssm_conv,
           state_rwkv, state_rwkv_shift, page_table, w_ada, b_ada, w_in, ssd_conv_w, ssd_conv_b, ssd_dt_bias,
           ssd_a_log, ssd_d, ssd_norm_w, rwkv_mu, rwkv_w0, rwkv_w2, rwkv_a0, rwkv_a2, rwkv_g2, rwkv_k_k, rwkv_k_a,
           rwkv_r_k, rwkv_lnx_w, rwkv_lnx_b, cmp_pe, cmp_w1, cmp_w2, w_o_ssd, w_o_rwkv, w_o_nsa, w_out, ln1_g,
           ln1_b, ln2_g, ln2_b, w_group, w_router, moe_w1, moe_w3, moe_w2):
    p = dict(w_in=w_in, ssd_conv_w=ssd_conv_w, ssd_conv_b=ssd_conv_b, ssd_dt_bias=ssd_dt_bias, ssd_a_log=ssd_a_log,
             ssd_d=ssd_d, ssd_norm_w=ssd_norm_w, rwkv_mu=rwkv_mu, rwkv_w0=rwkv_w0, rwkv_w2=rwkv_w2, rwkv_a0=rwkv_a0,
             rwkv_a2=rwkv_a2, rwkv_g2=rwkv_g2, rwkv_k_k=rwkv_k_k, rwkv_k_a=rwkv_k_a, rwkv_r_k=rwkv_r_k,
             rwkv_lnx_w=rwkv_lnx_w, rwkv_lnx_b=rwkv_lnx_b, cmp_pe=cmp_pe, cmp_w1=cmp_w1, cmp_w2=cmp_w2,
             w_o_ssd=w_o_ssd, w_o_rwkv=w_o_rwkv, w_o_nsa=w_o_nsa, w_out=w_out, ln1_g=ln1_g, ln1_b=ln1_b,
             ln2_g=ln2_g, ln2_b=ln2_b, w_group=w_group, w_router=w_router, moe_w1=moe_w1, moe_w3=moe_w3,
             moe_w2=moe_w2)
    bp, bs = x_prompt.shape[0], x_sample.shape[0]
    past_len = page_table.shape[1] * PAGE
    nb = -(-(bp + bs) // SUBLANE) * SUBLANE
    c_all = jnp.pad(jnp.concatenate([c_prompt, c_sample], axis=0), ((0, nb - bp - bs), (0, 0)))
    mod = ada_mod(c_all, w_ada, b_ada)
    n_phys = cache_cmp.shape[1]
    cmp_pages = cache_cmp.reshape(DEPTH, n_phys, PAGE, 2 * LANE)
    sel_pages = cache_sel.reshape(DEPTH, n_phys, PAGE, 2 * LANE)
    zeros = lambda *s: jnp.zeros(s, F32)
    xp, xs = x_prompt, x_sample
    st_p, st_s = [], []
    for l in range(DEPTH):
        w = _layer_weights(l, p)
        xp, sp_l = _trunk_layer(xp, mod[l, :bp], l, w, 0, zeros(bp, SSD_CONV - 1, SSD_CONV_DIM),
                                zeros(bp, SSD_HEADS, SSD_HD, SSD_STATE), zeros(bp, RWKV_COLS),
                                zeros(bp, RWKV_HEADS, RWKV_HD, RWKV_HD), None, None, None, None)
        xs, ss_l = _trunk_layer(xs, mod[l, bp:bp + bs], l, w, past_len, state_ssm_conv[l], state_ssm[l],
                                state_rwkv_shift[l], state_rwkv[l], cache_win[l], cmp_pages, sel_pages, page_table)
        st_p.append(sp_l)
        st_s.append(ss_l)
    sp = [jnp.stack(z) for z in zip(*st_p)]
    ss = [jnp.stack(z) for z in zip(*st_s)]
    return (xp, xs, sp[0], sp[1], sp[2], sp[3], sp[4], sp[5], sp[6], ss[0], ss[1], ss[2], ss[3], ss[4], ss[5], ss[6])
```

```python
import functools
import math

import jax
import jax.numpy as jnp
from jax import lax
from jax.experimental import pallas as pl
from jax.experimental.pallas import tpu as pltpu

F32 = jnp.float32
BF16 = jnp.bfloat16
HIGHEST = lax.Precision.HIGHEST

D_MODEL = 1024
DEPTH = 2
PAGE = 128
SSD_HEADS, SSD_HD, SSD_GROUPS, SSD_STATE, SSD_CONV = 16, 64, 2, 128, 4
SSD_INNER = SSD_HEADS * SSD_HD
SSD_CONV_DIM = SSD_INNER + 2 * SSD_GROUPS * SSD_STATE
SSD_COLS = SSD_INNER + SSD_CONV_DIM + SSD_HEADS
RWKV_HEADS, RWKV_HD = 16, 64
RWKV_DIM = RWKV_HEADS * RWKV_HD
RWKV_COLS = 3 * RWKV_DIM + 64 + 64 + 128
RWKV_LNX_EPS = 64e-5
NSA_HEADS, NSA_KVH, NSA_HPG, NSA_HD = 16, 2, 8, 64
NSA_DIM = NSA_HEADS * NSA_HD
NSA_COLS = NSA_DIM + 3 * 2 * NSA_KVH * NSA_HD + 3 * NSA_HEADS
CMP_BLOCK, CMP_STRIDE, CMP_HIDDEN = 32, 16, 128
SEL_BLOCK, N_SEL, WINDOW = 64, 16, 512
FORCE_BONUS = 1e4
N_GROUPS, EPG, N_EXPERTS, EXPERT_HIDDEN = 4, 4, 16, 256
ALPHA = (2 * DEPTH) ** 0.25
LN_EPS = 1e-5
RMS_EPS = 1e-5
NEG = -1e30

LANE = 128
SUBLANE = 8
VMEM_LIMIT = 56 * 1024 * 1024


def _cparams(sem):
    return pltpu.CompilerParams(dimension_semantics=sem, vmem_limit_bytes=VMEM_LIMIT)


def _dot(a, b):
    return jnp.dot(a.astype(BF16), b.astype(BF16), preferred_element_type=F32)


def _dot_hi(a, b):
    return jnp.dot(a, b, precision=HIGHEST, preferred_element_type=F32)


def _dot_nt(a, b):
    return lax.dot_general(a.astype(BF16), b.astype(BF16), (((1,), (1,)), ((), ())),
                           preferred_element_type=F32)


def _dot_nt_hi(a, b):
    return lax.dot_general(a, b, (((1,), (1,)), ((), ())), precision=HIGHEST,
                           preferred_element_type=F32)


def _split16(a):
    hi = a.astype(BF16)
    return hi, (a - hi.astype(F32)).astype(BF16)


def _dot3(a, b):
    (ah, al), (bh, bl) = a, b
    return _dot(ah, bh) + _dot(ah, bl) + _dot(al, bh)


def _dot3_nt(a, b):
    (ah, al), (bh, bl) = a, b
    return _dot_nt(ah, bh) + _dot_nt(ah, bl) + _dot_nt(al, bh)


def _sigmoid(x):
    return 1.0 / (1.0 + jnp.exp(-x))


def _silu(x):
    return x * _sigmoid(x)


def _softplus(x):
    return jnp.maximum(x, 0.0) + jnp.log(1.0 + jnp.exp(-jnp.abs(x)))


def _iota(shape, dim):
    return lax.broadcasted_iota(jnp.int32, shape, dim)


def _head_onehot(n_heads, hd, pad_rows):
    r = jnp.arange(pad_rows)[:, None]
    c = jnp.arange(n_heads * hd)[None, :] // hd
    return (r == c).astype(F32)


def _ada_kernel(c_ref, w_ref, b_ref, o_ref):
    o_ref[...] = _dot_hi(_silu(c_ref[...]), w_ref[...]) + b_ref[...]


def ada_mod(c_all, w_ada, b_ada):
    nb = c_all.shape[0]
    return pl.pallas_call(
        _ada_kernel,
        grid=(DEPTH, 6),
        in_specs=[pl.BlockSpec((nb, D_MODEL), lambda l, j: (0, 0)),
                  pl.BlockSpec((None, D_MODEL, D_MODEL), lambda l, j: (l, 0, j)),
                  pl.BlockSpec((None, 1, D_MODEL), lambda l, j: (l, 0, j))],
        out_specs=pl.BlockSpec((None, nb, D_MODEL), lambda l, j: (l, 0, j)),
        out_shape=jax.ShapeDtypeStruct((DEPTH, nb, 6 * D_MODEL), F32),
        compiler_params=_cparams(("arbitrary", "arbitrary")),
        name="ada_mod",
    )(c_all, w_ada, b_ada.reshape(DEPTH, 1, 6 * D_MODEL))


def _inproj_kernel(x_ref, sc_ref, sh_ref, w_ref, o_ref):
    u = x_ref[...] * (1.0 + sc_ref[...]) + sh_ref[...]
    o_ref[...] = _dot(u, w_ref[...])


def mod_proj(x, sc, sh, w, tm=256):
    m, k = x.shape
    n = w.shape[1]
    tm = min(tm, m)
    per_row = sc.shape[0] != 1
    mspec = (pl.BlockSpec((tm, k), lambda i: (i, 0)) if per_row
             else pl.BlockSpec((1, k), lambda i: (0, 0)))
    return pl.pallas_call(
        _inproj_kernel,
        grid=(m // tm,),
        in_specs=[pl.BlockSpec((tm, k), lambda i: (i, 0)), mspec, mspec,
                  pl.BlockSpec((k, n), lambda i: (0, 0))],
        out_specs=pl.BlockSpec((tm, n), lambda i: (i, 0)),
        out_shape=jax.ShapeDtypeStruct((m, n), F32),
        compiler_params=_cparams(("arbitrary",)),
        name="mod_proj",
    )(x, sc, sh, w)


SSD_Q = 128
SSD_W = SSD_INNER + SSD_CONV_DIM + LANE


def _ssd_kernel(n_valid, zxd_ref, cst_ref, h0_ref, cw_ref, cb_ref, dtb_ref, alog_ref, dexp_ref,
                nw_ref, e16_ref, tri_ref, y_ref, hout_ref, ext, hT):
    j = pl.program_id(1)
    q = SSD_Q

    @pl.when(j == 0)
    def _():
        ext[0:8, :] = cst_ref[...]
        hT[...] = h0_ref[...]

    ext[8:8 + q, :] = zxd_ref[:, SSD_INNER:SSD_INNER + SSD_CONV_DIM]
    conv = (cb_ref[...] + ext[5:5 + q, :] * cw_ref[0:1, :] + ext[6:6 + q, :] * cw_ref[1:2, :]
            + ext[7:7 + q, :] * cw_ref[2:3, :] + ext[8:8 + q, :] * cw_ref[3:4, :])
    ext[0:8, :] = ext[q:q + 8, :]
    xbc = _silu(conv)
    xs = xbc[:, :SSD_INNER]
    bm = xbc[:, SSD_INNER:SSD_INNER + 2 * SSD_STATE]
    cm = xbc[:, SSD_INNER + 2 * SSD_STATE:]

    row = _iota((q, LANE), 0)
    lane = _iota((q, LANE), 1)
    dt = _softplus(zxd_ref[:, SSD_INNER + SSD_CONV_DIM:] + dtb_ref[...])
    dt = jnp.where((lane < SSD_HEADS) & (row + j * q < n_valid), dt, 0.0)
    a = -jnp.exp(alog_ref[...])
    acum = _dot_hi(tri_ref[...], dt * a)
    a_last = acum[q - 1:q, :]
    e16 = e16_ref[...]
    dt_e = _dot_hi(dt, e16)
    ea_e = _dot_hi(jnp.exp(acum), e16)
    dte_e = _dot_hi(jnp.exp(a_last - acum), e16)
    elast_e = ea_e[q - 1:q, :]
    xdt = xs * dt_e
    acum_t = acum.T
    tril = row >= lane
    lo_half = lane < SSD_HD

    y_tiles = []
    for g in range(SSD_GROUPS):
        cc = cm[:, g * SSD_STATE:(g + 1) * SSD_STATE]
        bc = bm[:, g * SSD_STATE:(g + 1) * SSD_STATE]
        cb = _dot_nt(cc, bc)
        gs = slice(g * 512, (g + 1) * 512)
        h_in = hT[:, gs]
        y_off = _dot(cc, h_in) * ea_e[:, gs]
        hT[:, gs] = h_in * elast_e[:, gs] + _dot(bc.T, xdt[:, gs] * dte_e[:, gs])
        for tl in range(4):
            t = 4 * g + tl
            xt = xdt[:, t * LANE:(t + 1) * LANE]
            yd = jnp.zeros((q, LANE), F32)
            for sub in range(2):
                h = 2 * t + sub
                diff = acum[:, h:h + 1] - acum_t[h:h + 1, :]
                m = cb * jnp.exp(jnp.where(tril, diff, NEG))
                yd = yd + _dot(m, jnp.where(lo_half if sub == 0 else ~lo_half, xt, 0.0))
            y_tiles.append(yd + y_off[:, tl * LANE:(tl + 1) * LANE])
    y = jnp.concatenate(y_tiles, axis=1) + dexp_ref[...] * xs
    y = y * _silu(zxd_ref[:, :SSD_INNER])
    outs = []
    for g in range(SSD_GROUPS):
        yg = y[:, g * 512:(g + 1) * 512]
        ms = jnp.sum(yg * yg, axis=-1, keepdims=True) * (1.0 / 512.0)
        outs.append(yg * lax.rsqrt(ms + RMS_EPS))
    y_ref[...] = jnp.concatenate(outs, axis=1) * nw_ref[...]

    @pl.when(j == pl.num_programs(1) - 1)
    def _():
        hout_ref[...] = hT[...]


def ssd_mixer(zxd, conv_st8, h0t, n_valid, cw8, cb, dtb, alog, dexp, nw):
    b, lp, _ = zxd.shape
    nj = lp // SSD_Q
    e16 = _head_onehot(SSD_HEADS, SSD_HD, LANE)
    tri = (jnp.arange(SSD_Q)[:, None] >= jnp.arange(SSD_Q)[None, :]).astype(F32)
    full = lambda shp: pl.BlockSpec(shp, lambda bi, j: (0,) * len(shp))
    return pl.pallas_call(
        functools.partial(_ssd_kernel, n_valid),
        grid=(b, nj),
        in_specs=[pl.BlockSpec((None, SSD_Q, SSD_W), lambda bi, j: (bi, j, 0)),
                  pl.BlockSpec((None, 8, SSD_CONV_DIM), lambda bi, j: (bi, 0, 0)),
                  pl.BlockSpec((None, SSD_STATE, SSD_INNER), lambda bi, j: (bi, 0, 0)),
                  full((8, SSD_CONV_DIM)), full((1, SSD_CONV_DIM)), full((1, LANE)), full((1, LANE)),
                  full((1, SSD_INNER)), full((1, SSD_INNER)), full((LANE, SSD_INNER)),
                  full((SSD_Q, SSD_Q))],
        out_specs=[pl.BlockSpec((None, SSD_Q, SSD_INNER), lambda bi, j: (bi, j, 0)),
                   pl.BlockSpec((None, SSD_STATE, SSD_INNER), lambda bi, j: (bi, 0, 0))],
        out_shape=[jax.ShapeDtypeStruct((b, lp, SSD_INNER), F32),
                   jax.ShapeDtypeStruct((b, SSD_STATE, SSD_INNER), F32)],
        scratch_shapes=[pltpu.VMEM((SSD_Q + 8, SSD_CONV_DIM), F32),
                        pltpu.VMEM((SSD_STATE, SSD_INNER), F32)],
        compiler_params=_cparams(("arbitrary", "arbitrary")),
        name="ssd_mixer",
    )(zxd, conv_st8, h0t, cw8, cb, dtb, alog, dexp, nw, e16, tri)


def _rwkv_prep_kernel(n_valid, tr, x_ref, sh8_ref, mu_ref, w0_ref, a0_ref, kk_ref, ka_ref, rk_ref,
                      w2_ref, a2_ref, g2_ref, seg_ref, e16_ref,
                      r_o, ld_o, k_o, v_o, kk_o, bb_o, g_o, bv_o, ext):
    j = pl.program_id(1)

    @pl.when(j == 0)
    def _():
        ext[0:8, :] = sh8_ref[...]

    x = x_ref[...]
    ext[8:8 + tr, :] = x
    prev = ext[7:7 + tr, :]
    ext[0:8, :] = ext[tr:tr + 8, :]
    xm = x + (prev - x) * mu_ref[...]
    d = RWKV_DIM
    r, k, v = xm[:, :d], xm[:, d:2 * d], xm[:, 2 * d:3 * d]
    lo = xm[:, 3 * d:3 * d + LANE]
    glo = xm[:, 3 * d + LANE:]
    w = w0_ref[...] + _dot_hi(jnp.tanh(lo), w2_ref[...])
    ld = -jnp.exp(-_softplus(-w) - 0.5)
    a = _sigmoid(a0_ref[...] + _dot_hi(lo, a2_ref[...]))
    g = _dot_hi(_sigmoid(glo), g2_ref[...])
    seg, e16 = seg_ref[...], e16_ref[...]
    kk = k * kk_ref[...]
    ss = _dot_hi(kk * kk, seg)
    kk = kk * _dot_hi(lax.rsqrt(jnp.maximum(ss, 1e-24)), e16)
    k2 = k * (1.0 + (a - 1.0) * ka_ref[...])
    bonus = _dot_hi(_dot_hi(r * k2 * rk_ref[...], seg), e16)
    valid = (_iota((tr, 1), 0) + j * tr) < n_valid
    zero = lambda t: jnp.where(valid, t, 0.0)
    r_o[...] = r
    ld_o[...] = zero(ld)
    k_o[...] = zero(k2)
    v_o[...] = zero(v)
    kk_o[...] = zero(kk)
    bb_o[...] = zero(kk * a)
    g_o[...] = g
    bv_o[...] = bonus * v


def rwkv_prep(cols, shift8, n_valid, prm):
    b, lp, _ = cols.shape
    tr = min(128, lp)
    d = RWKV_DIM
    full = lambda shp: pl.BlockSpec(shp, lambda bi, j: (0,) * len(shp))
    row = lambda: pl.BlockSpec((None, tr, d), lambda bi, j: (bi, j, 0))
    return pl.pallas_call(
        functools.partial(_rwkv_prep_kernel, n_valid, tr),
        grid=(b, lp // tr),
        in_specs=[pl.BlockSpec((None, tr, RWKV_COLS), lambda bi, j: (bi, j, 0)),
                  pl.BlockSpec((None, 8, RWKV_COLS), lambda bi, j: (bi, 0, 0)),
                  full((1, RWKV_COLS))] + [full((1, d))] * 5 + [full((LANE, d))] * 3
                 + [full((d, LANE)), full((LANE, d))],
        out_specs=[row() for _ in range(8)],
        out_shape=[jax.ShapeDtypeStruct((b, lp, d), F32)] * 8,
        scratch_shapes=[pltpu.VMEM((tr + 8, RWKV_COLS), F32)],
        compiler_params=_cparams(("arbitrary", "arbitrary")),
        name="rwkv_prep",
    )(cols, shift8, *prm)


RWKV_C = 64


RWKV_NPP = 2


def _split24(a):
    h1 = a.astype(BF16)
    r1 = a - h1.astype(F32)
    h2 = r1.astype(BF16)
    return h1, h2, (r1 - h2.astype(F32)).astype(BF16)


def _rwkv_scan_kernel(nch, r_ref, ld_ref, k_ref, v_ref, kk_ref, bb_ref, g_ref, bv_ref, s0_ref,
                      lnw_ref, lnb_ref, y_ref, sout_ref, st):
    i = pl.program_id(2)
    c = RWKV_C

    @pl.when(i == 0)
    def _():
        st[...] = s0_ref[...]

    n2 = 2 * c
    rr = _iota((n2, n2), 0)
    cc = _iota((n2, n2), 1)
    eye = (rr == cc).astype(F32)
    upper = rr < cc
    upper_eq = rr <= cc
    tri = (_iota((c, c), 0) >= _iota((c, c), 1)).astype(BF16)
    lane = _iota((c, LANE), 1)
    m0 = lane < RWKV_HD
    hmean = jnp.where((rr < RWKV_HD) == (cc < RWKV_HD), 1.0 / RWKV_HD, 0.0).astype(BF16)

    def stack(t):
        return jnp.concatenate([jnp.where(m0, t, 0.0), jnp.where(m0, 0.0, t)], axis=0)

    def head_mean(t):
        return sum(_dot(piece, hmean) for piece in _split24(t))

    units = [(ci, pi) for ci in range(nch) for pi in range(RWKV_NPP)]
    pre = []
    for ci, pi in units:
        sl = pl.ds(ci * c, c)
        ls = slice(pi * LANE, (pi + 1) * LANE)
        r, ld, k, v, kk, bb = (ref[sl, ls] for ref in (r_ref, ld_ref, k_ref, v_ref, kk_ref, bb_ref))
        cum = sum(_dot(tri, piece) for piece in _split24(ld))
        p_in = jnp.exp(cum)
        p_inv = jnp.exp(-cum)
        p_c = p_in[c - 1:c, :]
        kks = stack(kk * jnp.exp(cum - ld))
        rs = stack(r * p_in)
        bs = stack(bb * p_inv)
        ks = stack(k * p_inv)
        kks_p, rs_p, bs_p, ks_p = _split16(kks), _split16(rs), _split16(bs), _split16(ks)
        pre.append(dict(
            p_c=p_c, vs_t=_split16(stack(v).T), kks_t=_split16(kks.T), rs_t=_split16(rs.T),
            upd_r=_split16(jnp.concatenate([bs * p_c, ks * p_c], axis=0)),
            nt=jnp.where(upper, _dot3_nt(bs_p, kks_p), 0.0),
            avk=_split16(jnp.where(upper, _dot3_nt(ks_p, kks_p), 0.0)),
            arb=_split16(jnp.where(upper_eq, _dot3_nt(bs_p, rs_p), 0.0)),
            ark=_split16(jnp.where(upper_eq, _dot3_nt(ks_p, rs_p), 0.0))))
    tinv = [eye - u['nt'] for u in pre]
    pw = [_split16(u['nt']) for u in pre]
    for _ in range(5):
        pw = [_split16(_dot3(w, w)) for w in pw]
        tinv = [t + _dot3(_split16(t), w) for t, w in zip(tinv, pw)]
    tinv = [_split16(t) for t in tinv]

    for ci in range(nch):
        sl = pl.ds(ci * c, c)
        for pi in range(RWKV_NPP):
            ls = slice(pi * LANE, (pi + 1) * LANE)
            u = pre[ci * RWKV_NPP + pi]
            s = st[pi]
            s_p = _split16(s)
            ut = _dot3(_split16(_dot3(s_p, u['kks_t']) + _dot3(u['vs_t'], u['avk'])), tinv[ci * RWKV_NPP + pi])
            ut_p = _split16(ut)
            yt = _dot3(s_p, u['rs_t']) - _dot3(ut_p, u['arb']) + _dot3(u['vs_t'], u['ark'])
            upd_l = tuple(jnp.concatenate([-a, b], axis=1) for a, b in zip(ut_p, u['vs_t']))
            st[pi] = s * u['p_c'] + _dot3(upd_l, u['upd_r'])
            ys = yt.T
            y = ys[:c, :] + ys[c:, :]
            yc = y - head_mean(y)
            var = head_mean(yc * yc)
            y = yc * lax.rsqrt(var + RWKV_LNX_EPS) * lnw_ref[:, ls] + lnb_ref[:, ls] + bv_ref[sl, ls]
            y_ref[sl, ls] = y * g_ref[sl, ls]

    @pl.when(i == pl.num_programs(2) - 1)
    def _():
        sout_ref[...] = st[...]


def rwkv_scan(r, ld, k, v, kk, bb, g, bv, s0p, lnw, lnb):
    b, lp, d = r.shape
    nch = min(4, lp // RWKV_C)
    rb = nch * RWKV_C
    npp = RWKV_NPP
    npair = d // LANE
    row = lambda: pl.BlockSpec((None, rb, npp * LANE), lambda bi, p, i: (bi, i, p))
    st = lambda: pl.BlockSpec((None, npp, LANE, LANE), lambda bi, p, i: (bi, p, 0, 0))
    vec = lambda: pl.BlockSpec((1, npp * LANE), lambda bi, p, i: (0, p))
    return pl.pallas_call(
        functools.partial(_rwkv_scan_kernel, nch),
        grid=(b, npair // npp, lp // rb),
        in_specs=[row() for _ in range(8)] + [st(), vec(), vec()],
        out_specs=[row(), st()],
        out_shape=[jax.ShapeDtypeStruct((b, lp, d), F32),
                   jax.ShapeDtypeStruct((b, npair, LANE, LANE), F32)],
        scratch_shapes=[pltpu.VMEM((npp, LANE, LANE), F32)],
        compiler_params=_cparams(("arbitrary", "arbitrary", "arbitrary")),
        name="rwkv_scan",
    )(r, ld, k, v, kk, bb, g, bv, s0p, lnw, lnb)


def _pair_blockdiag(s):
    b = s.shape[0]
    s = s.reshape(b, 8, 2, RWKV_HD, RWKV_HD)
    z = jnp.zeros_like(s[:, :, 0])
    top = jnp.concatenate([s[:, :, 0], z], axis=-1)
    bot = jnp.concatenate([z, s[:, :, 1]], axis=-1)
    return jnp.concatenate([top, bot], axis=-2)


def _pair_unblock(sp):
    b = sp.shape[0]
    return jnp.stack([sp[:, :, :RWKV_HD, :RWKV_HD], sp[:, :, RWKV_HD:, RWKV_HD:]], axis=2).reshape(
        b, RWKV_HEADS, RWKV_HD, RWKV_HD)


CMP_HW = 4 * NSA_KVH * CMP_HIDDEN
NSEG_PAGE = PAGE // CMP_STRIDE


def _cmp_h_rows_kernel(nsr, xk_ref, xv_ref, wk_ref, wv_ref, o_ref):
    half = CMP_HW // 2
    acc_k = jnp.zeros((nsr, half), F32)
    acc_v = jnp.zeros((nsr, half), F32)
    for s in range(CMP_STRIDE):
        rows = pl.ds(s, nsr, stride=CMP_STRIDE)
        acc_k = acc_k + _dot(xk_ref[rows, :], wk_ref[s])
        acc_v = acc_v + _dot(xv_ref[rows, :], wv_ref[s])
    o_ref[...] = jnp.concatenate([acc_k, acc_v], axis=1)


def cmp_h_rows(rows, wk, wv, lt=0):
    b, t, _ = rows.shape
    tb = 2048 if t % 2048 == 0 else t
    nsr = tb // CMP_STRIDE
    wspec = pl.BlockSpec((CMP_STRIDE, LANE, CMP_HW // 2), lambda bi, i: (0, 0, 0))
    return pl.pallas_call(
        functools.partial(_cmp_h_rows_kernel, nsr),
        grid=(b, t // tb),
        in_specs=[pl.BlockSpec((None, tb, LANE), lambda bi, i: (bi, i, lt)),
                  pl.BlockSpec((None, tb, LANE), lambda bi, i: (bi, i, lt + 1)), wspec, wspec],
        out_specs=pl.BlockSpec((None, nsr, CMP_HW), lambda bi, i: (bi, i, 0)),
        out_shape=jax.ShapeDtypeStruct((b, t // CMP_STRIDE, CMP_HW), F32),
        compiler_params=_cparams(("arbitrary", "arbitrary")),
        name="cmp_h_rows",
    )(rows, rows, wk, wv)


PAGES_PER_STEP = 16


def _cmp_h_pages_kernel(pt_ref, *refs):
    pp = PAGES_PER_STEP
    pk, pv = refs[:pp], refs[pp:2 * pp]
    wk_ref, wv_ref, o_ref = refs[2 * pp:]
    half = CMP_HW // 2
    acc_k = jnp.zeros((pp * NSEG_PAGE, half), F32)
    acc_v = jnp.zeros((pp * NSEG_PAGE, half), F32)
    for s in range(CMP_STRIDE):
        rows = pl.ds(s, NSEG_PAGE, stride=CMP_STRIDE)
        xk = jnp.concatenate([r[rows, :] for r in pk], axis=0)
        xv = jnp.concatenate([r[rows, :] for r in pv], axis=0)
        acc_k = acc_k + _dot(xk, wk_ref[s])
        acc_v = acc_v + _dot(xv, wv_ref[s])
    o_ref[...] = jnp.concatenate([acc_k, acc_v], axis=1)


def _page_specs(layer, lane_tile, n_pages):
    pp = PAGES_PER_STEP

    def spec(p):
        def imap(bi, i, pt):
            return (layer, pt[bi, jnp.minimum(i * pp + p, n_pages - 1)], 0, lane_tile)
        return pl.BlockSpec((None, None, PAGE, LANE), imap)
    return [spec(p) for p in range(pp)]


def cmp_h_pages(cache, page_table, layer, wk, wv):
    b, n_pages = page_table.shape
    pp = PAGES_PER_STEP
    wspec = pl.BlockSpec((CMP_STRIDE, LANE, CMP_HW // 2), lambda bi, i, pt: (0, 0, 0))
    gs = pltpu.PrefetchScalarGridSpec(
        num_scalar_prefetch=1, grid=(b, n_pages // pp),
        in_specs=_page_specs(layer, 0, n_pages) + _page_specs(layer, 1, n_pages) + [wspec, wspec],
        out_specs=pl.BlockSpec((None, pp * NSEG_PAGE, CMP_HW), lambda bi, i, pt: (bi, i, 0)))
    return pl.pallas_call(
        _cmp_h_pages_kernel, grid_spec=gs,
        out_shape=jax.ShapeDtypeStruct((b, n_pages * NSEG_PAGE, CMP_HW), F32),
        compiler_params=_cparams(("arbitrary", "arbitrary")),
        name="cmp_h_pages",
    )(page_table, *([cache] * (2 * pp)), wk, wv)


NCP = 1024


def _cmp_finish_kernel(hn_row, hp_ref, hn_ref, pek_ref, pev_ref, wk_ref, wv_ref, w2k_ref, w2v_ref,
                       ckh_ref, ckl_ref, cvt_ref):
    qw = CMP_HW // 4
    rk = jnp.zeros((8, 2 * qw), F32)
    rv = jnp.zeros((8, 2 * qw), F32)
    for s in range(CMP_STRIDE):
        rk = rk + _dot(pek_ref[s], wk_ref[s])
        rv = rv + _dot(pev_ref[s], wv_ref[s])
    last = _iota((NCP, qw), 0) == hn_row

    def hidden(off, rb):
        h0 = hp_ref[:, off:off + qw]
        h1 = pltpu.roll(hp_ref[:, off + qw:off + 2 * qw], NCP - 1, 0)
        h1 = jnp.where(last, hn_ref[0:1, off + qw:off + 2 * qw], h1)
        return _silu(h0 + h1 + rb[0:1, :qw] + rb[1:2, qw:])

    ckh_ref[...], ckl_ref[...] = _split16(_dot(hidden(0, rk), w2k_ref[...]))
    cvt_ref[...] = _dot(hidden(2 * qw, rv), w2v_ref[...]).T.astype(BF16)


def cmp_finish(hp, hn, hn_row, pek, pev, wk, wv, w2k, w2v):
    b = hp.shape[0]
    full = lambda shp: pl.BlockSpec(shp, lambda bi: (0,) * len(shp))
    return pl.pallas_call(
        functools.partial(_cmp_finish_kernel, hn_row),
        grid=(b,),
        in_specs=[pl.BlockSpec((None, NCP, CMP_HW), lambda bi: (bi, 0, 0)),
                  pl.BlockSpec((None, 8, CMP_HW), lambda bi: (bi, 0, 0)),
                  full((CMP_STRIDE, 8, LANE)), full((CMP_STRIDE, 8, LANE)),
                  full((CMP_STRIDE, LANE, CMP_HW // 2)), full((CMP_STRIDE, LANE, CMP_HW // 2)),
                  full((CMP_HW // 4, LANE)), full((CMP_HW // 4, LANE))],
        out_specs=[pl.BlockSpec((None, NCP, LANE), lambda bi: (bi, 0, 0)),
                   pl.BlockSpec((None, NCP, LANE), lambda bi: (bi, 0, 0)),
                   pl.BlockSpec((None, LANE, NCP), lambda bi: (bi, 0, 0))],
        out_shape=[jax.ShapeDtypeStruct((b, NCP, LANE), BF16),
                   jax.ShapeDtypeStruct((b, NCP, LANE), BF16),
                   jax.ShapeDtypeStruct((b, LANE, NCP), BF16)],
        compiler_params=_cparams(("arbitrary",)),
        name="cmp_finish",
    )(hp, hn, pek, pev, wk, wv, w2k, w2v)


def _cmp_weights(pe, w1, w2):
    eye_g = jnp.eye(NSA_KVH, dtype=F32)

    def first(e):
        w = w1[e].reshape(2, CMP_STRIDE, NSA_HD, CMP_HIDDEN)
        w = jnp.einsum('isdf,gh->sgdihf', w, eye_g)
        return w.reshape(CMP_STRIDE, LANE, CMP_HW // 2).astype(BF16)

    def second(e):
        return jnp.einsum('fd,gh->gfhd', w2[e], eye_g).reshape(CMP_HW // 4, LANE).astype(BF16)

    def pos(e):
        p = pe[e].reshape(2, CMP_STRIDE, NSA_HD)
        p = jnp.tile(jnp.transpose(p, (1, 0, 2)), (1, 1, NSA_KVH))
        return jnp.pad(p, ((0, 0), (0, 6), (0, 0)))

    return pos(0), pos(1), first(0), first(1), second(0), second(1)


def _kvprep_rows_kernel(nsub, tk, xk_ref, xv_ref, k_ref, vt_ref):
    k_ref[...] = xk_ref[...].astype(BF16)
    for u in range(nsub):
        vt_ref[u] = xv_ref[u * tk:(u + 1) * tk, :].T.astype(BF16)


def kvprep_rows(rows, tk, lt=0):
    b, t, _ = rows.shape
    tt = 512 if t % 512 == 0 else tk
    nsub = tt // tk
    return pl.pallas_call(
        functools.partial(_kvprep_rows_kernel, nsub, tk),
        grid=(b, t // tt),
        in_specs=[pl.BlockSpec((None, tt, LANE), lambda bi, i: (bi, i, lt)),
                  pl.BlockSpec((None, tt, LANE), lambda bi, i: (bi, i, lt + 1))],
        out_specs=[pl.BlockSpec((None, tt, LANE), lambda bi, i: (bi, i, 0)),
                   pl.BlockSpec((None, nsub, LANE, tk), lambda bi, i: (bi, i, 0, 0))],
        out_shape=[jax.ShapeDtypeStruct((b, t, LANE), BF16),
                   jax.ShapeDtypeStruct((b, t // tk, LANE, tk), BF16)],
        compiler_params=_cparams(("arbitrary", "arbitrary")),
        name="kvprep_rows",
    )(rows, rows)


SEL_TK = 512
WIN_TK = 128


def _kvprep_pages_kernel(n_steps, pt_ref, *refs):
    pp = PAGES_PER_STEP
    pk, pv = refs[:pp], refs[pp:2 * pp]
    nk_ref, nv_ref, k_ref, vt_ref = refs[2 * pp:]
    i = pl.program_id(1)
    per = SEL_TK // PAGE

    @pl.when(i < n_steps - 1)
    def _():
        for p in range(pp):
            k_ref[p * PAGE:(p + 1) * PAGE, :] = pk[p][...].astype(BF16)
            vt_ref[p // per, :, (p % per) * PAGE:(p % per + 1) * PAGE] = pv[p][...].T.astype(BF16)

    @pl.when(i == n_steps - 1)
    def _():
        k_ref[...] = jnp.zeros(k_ref.shape, BF16)
        vt_ref[...] = jnp.zeros(vt_ref.shape, BF16)
        k_ref[0:PAGE, :] = nk_ref[...].astype(BF16)
        vt_ref[0, :, 0:PAGE] = nv_ref[...].T.astype(BF16)


def kvprep_pages(cache, page_table, layer, new_rows, lt0=0):
    b, n_pages = page_table.shape
    pp = PAGES_PER_STEP
    n_steps = n_pages // pp + 1
    t = n_steps * pp * PAGE
    new = lambda lt: pl.BlockSpec((None, PAGE, LANE), lambda bi, i, pt: (bi, 0, lt0 + lt))
    gs = pltpu.PrefetchScalarGridSpec(
        num_scalar_prefetch=1, grid=(b, n_steps),
        in_specs=_page_specs(layer, 0, n_pages) + _page_specs(layer, 1, n_pages) + [new(0), new(1)],
        out_specs=[pl.BlockSpec((None, pp * PAGE, LANE), lambda bi, i, pt: (bi, i, 0)),
                   pl.BlockSpec((None, pp * PAGE // SEL_TK, LANE, SEL_TK), lambda bi, i, pt: (bi, i, 0, 0))])
    return pl.pallas_call(
        functools.partial(_kvprep_pages_kernel, n_steps), grid_spec=gs,
        out_shape=[jax.ShapeDtypeStruct((b, t, LANE), BF16),
                   jax.ShapeDtypeStruct((b, t // SEL_TK, LANE, SEL_TK), BF16)],
        compiler_params=_cparams(("arbitrary", "arbitrary")),
        name="kvprep_pages",
    )(page_table, *([cache] * (2 * pp)), new_rows, new_rows)


TQ = 128
NQL = NSA_HEADS * TQ
NSA_CMP_TILE, NSA_SEL_TILE, NSA_WIN_TILE, NSA_GL_TILE = 8, 10, 12, 14
NSA_W = (NSA_GL_TILE + 1) * LANE
WIN_TILES = WINDOW // WIN_TK + 1


def _rowmax(x):
    return jnp.max(x, axis=0, keepdims=True)


def _rowsum(x):
    return jnp.sum(x, axis=0, keepdims=True)


LOG2E = 1.4426950408889634
SEL_PER = SEL_TK // SEL_BLOCK


def _real(m):
    return jnp.where(m > 0.5 * NEG, m, 0.0)


def _nsa_attn_kernel(pos0, nc, ns, wt0, wpos0, n_win,
                     q_ref, gl_ref, ckh_ref, ckl_ref, cvt_ref, ks_ref, vst_ref, *rest):
    kw = rest[:WIN_TILES]
    vw = rest[WIN_TILES:2 * WIN_TILES]
    msel_ref, o_ref, sel_scr, work_scr, sel3_scr, imp_scr, oc_scr, acc_scr, ml_scr, ot_scr = rest[2 * WIN_TILES:]
    i = pl.program_id(1)
    t0 = pos0 + i * TQ
    nsp = sel_scr.shape[0]
    hd = NSA_HD

    qt = (q_ref[...] * (hd ** -0.5 * LOG2E)).T
    zero = jnp.zeros((hd, TQ), F32)
    pieces = []
    for jh in range(NSA_HEADS):
        blk = qt[jh * hd:(jh + 1) * hd, :]
        pieces.append(jnp.concatenate([blk, zero] if jh < NSA_HPG else [zero, blk], axis=0))
    qbd = jnp.concatenate(pieces, axis=1)
    qbd16 = qbd.astype(BF16)
    t_q = t0 + _iota((1, TQ), 1)
    gw = NSA_HPG * TQ

    def heads(x, n=NSA_HPG):
        return jnp.concatenate([x] * n, axis=1)

    c_idx = _iota((NCP, TQ), 0)
    cbias = jnp.where((CMP_STRIDE * c_idx + (CMP_BLOCK - 1) <= t_q) & (c_idx < nc), 0.0, NEG)
    for g in range(NSA_KVH):
        gs = slice(g * gw, (g + 1) * gw)
        qh, ql = _split16(qbd[:, gs])
        ckh = ckh_ref[...]
        s = _dot(ckh, qh) + _dot(ckh, ql) + _dot(ckl_ref[...], qh) + heads(cbias)
        p = jnp.exp2(s - _real(_rowmax(s)))
        l = _rowsum(p)
        inv = 1.0 / jnp.where(l > 0.0, l, 1.0)
        oc_scr[g] = _dot(cvt_ref[g * hd:(g + 1) * hd, :], p) * inv
        pc = p * inv
        imp = pc[:, :TQ]
        for h in range(1, NSA_HPG):
            imp = imp + pc[:, h * TQ:(h + 1) * TQ]
        imp_scr[:, g * TQ:(g + 1) * TQ] = imp
    pslc = _dot_hi(msel_ref[...], imp_scr[...])

    jrow = _iota((nsp, 2 * TQ), 0)
    t_gq = t0 + (_iota((1, 2 * TQ), 1) % TQ)
    cur = lax.shift_right_logical(t_gq, 6)
    forced = (jrow == 0) | (jrow == cur) | (jrow == cur - 1)
    score = jnp.where(jrow * SEL_BLOCK <= t_gq, pslc + jnp.where(forced, FORCE_BONUS, 0.0), NEG)
    work_scr[...] = jnp.where(jrow < ns, score, -3e38)
    sel_scr[...] = jnp.zeros(sel_scr.shape, F32)
    jrow_f = jrow.astype(F32)

    def pick(_, carry):
        w = work_scr[...]
        best = _rowmax(w)
        first = jnp.min(jnp.where(w == best, jrow_f, 1e9), axis=0, keepdims=True)
        hit = jrow_f == first
        sel_scr[...] = jnp.where(hit, 1.0, sel_scr[...])
        work_scr[...] = jnp.where(hit, -jnp.inf, w)
        return carry

    lax.fori_loop(0, N_SEL, pick, 0)
    for u in range(nsp // SEL_PER):
        sel3_scr[u] = jnp.where(sel_scr[u * SEL_PER:(u + 1) * SEL_PER, :] > 0.5, 0.0, NEG)

    last = (t0 + TQ + SEL_TK - 1) // SEL_TK - 1
    ml_scr[0:1, :] = jnp.full((1, NQL), NEG, F32)
    ml_scr[1:2, :] = jnp.zeros((1, NQL), F32)
    acc_scr[...] = jnp.zeros(acc_scr.shape, F32)

    def sel_tile(kt, extra):
        s = _dot(ks_ref[pl.ds(pl.multiple_of(kt * SEL_TK, SEL_TK), SEL_TK), :], qbd16)
        sb = sel3_scr[kt]
        s = jnp.concatenate(
            [s[jj * SEL_BLOCK:(jj + 1) * SEL_BLOCK, :]
             + jnp.concatenate([heads(sb[jj:jj + 1, :TQ]), heads(sb[jj:jj + 1, TQ:])], axis=1)
             for jj in range(SEL_PER)], axis=0)
        if extra is not None:
            s = s + extra
        m_old = ml_scr[0:1, :]
        m_new = jnp.maximum(m_old, _rowmax(s))
        alpha = jnp.exp2(m_old - m_new)
        p = jnp.exp2(s - m_new)
        ml_scr[0:1, :] = m_new
        ml_scr[1:2, :] = alpha * ml_scr[1:2, :] + _rowsum(p)
        p16 = p.astype(BF16)
        for g in range(NSA_KVH):
            gs = slice(g * gw, (g + 1) * gw)
            acc_scr[g] = acc_scr[g] * alpha[:, gs] + _dot(vst_ref[kt, g * hd:(g + 1) * hd, :], p16[:, gs])

    def sel_body(kt, carry):
        sel_tile(kt, None)
        return carry

    lax.fori_loop(0, last, sel_body, 0)
    krow = _iota((SEL_TK, TQ), 0)
    sel_tile(last, heads(jnp.where(last * SEL_TK + krow <= t_q, 0.0, NEG), NSA_HEADS))
    l_sel = ml_scr[1:2, :]
    inv_sel = 1.0 / jnp.where(l_sel > 0.0, l_sel, 1.0)

    wrow = _iota((WIN_TK, TQ), 0)
    m = jnp.full((1, NQL), NEG, F32)
    l = jnp.zeros((1, NQL), F32)
    acc_w = [jnp.zeros((hd, gw), F32) for _ in range(NSA_KVH)]
    for w in range(WIN_TILES):
        tile = i + (wt0 - (WIN_TILES - 1) + w)
        idx = tile * WIN_TK + wrow
        wpos = wpos0 + idx
        dlt = t_q - wpos
        ok = (dlt >= 0) & (dlt < WINDOW) & (wpos >= 0) & (idx < n_win) & (tile >= 0)
        s = _dot(kw[w][...], qbd16) + heads(jnp.where(ok, 0.0, NEG), NSA_HEADS)
        m_new = jnp.maximum(m, _rowmax(s))
        alpha = jnp.where(m > 0.5 * NEG, jnp.exp2(m - _real(m_new)), 0.0)
        p = jnp.exp2(s - _real(m_new))
        l = alpha * l + _rowsum(p)
        p16 = p.astype(BF16)
        for g in range(NSA_KVH):
            gs = slice(g * gw, (g + 1) * gw)
            acc_w[g] = acc_w[g] * alpha[:, gs] + _dot(vw[w][g * hd:(g + 1) * hd, :], p16[:, gs])
        m = m_new
    inv_win = 1.0 / jnp.where(l > 0.0, l, 1.0)

    gt = _sigmoid(gl_ref[...]).T
    for jh in range(NSA_HEADS):
        g, h = divmod(jh, NSA_HPG)
        hl = slice(h * TQ, (h + 1) * TQ)
        ls = slice(jh * TQ, (jh + 1) * TQ)
        ot_scr[jh * hd:(jh + 1) * hd, :] = (
            gt[3 * jh:3 * jh + 1, :] * oc_scr[g, :, hl]
            + gt[3 * jh + 1:3 * jh + 2, :] * (acc_scr[g, :, hl] * inv_sel[:, ls])
            + gt[3 * jh + 2:3 * jh + 3, :] * (acc_w[g][:, hl] * inv_win[:, ls]))
    o_ref[...] = ot_scr[...].T


def nsa_attention(cols, ckh, ckl, cvt, ks, vst, kwin, vwt, pos0, n_new, n_buf, n_win):
    b, lq, _ = cols.shape
    q = gl = cols
    tk_total = ks.shape[1]
    n_sel_tiles = tk_total // SEL_TK
    nsp = tk_total // SEL_BLOCK
    t_total = pos0 + n_new
    nseg = -(-t_total // CMP_STRIDE)
    nc = nseg - CMP_BLOCK // CMP_STRIDE + 1
    ns = -(-t_total // SEL_BLOCK)
    wt0 = n_buf // WIN_TK
    n_wtiles = kwin.shape[1] // WIN_TK
    j = jnp.arange(nsp)[:, None]
    c = jnp.arange(NCP)[None, :]
    msel = ((c >= 4 * j - 1) & (c <= 4 * j + 3)).astype(F32)

    def wspec(w, vt):
        def imap(bi, i):
            tile = jnp.clip(i + (wt0 - (WIN_TILES - 1) + w), 0, n_wtiles - 1)
            return (bi, tile, 0, 0) if vt else (bi, tile, 0)
        return pl.BlockSpec((None, None, LANE, WIN_TK) if vt else (None, WIN_TK, LANE), imap)

    kern = functools.partial(_nsa_attn_kernel, pos0, nc, ns, wt0, pos0 - n_buf, n_win)
    return pl.pallas_call(
        kern,
        grid=(b, lq // TQ),
        in_specs=[pl.BlockSpec((None, TQ, NSA_DIM), lambda bi, i: (bi, i, 0)),
                  pl.BlockSpec((None, TQ, LANE), lambda bi, i: (bi, i, NSA_GL_TILE)),
                  pl.BlockSpec((None, NCP, LANE), lambda bi, i: (bi, 0, 0)),
                  pl.BlockSpec((None, NCP, LANE), lambda bi, i: (bi, 0, 0)),
                  pl.BlockSpec((None, LANE, NCP), lambda bi, i: (bi, 0, 0)),
                  pl.BlockSpec((None, tk_total, LANE), lambda bi, i: (bi, 0, 0)),
                  pl.BlockSpec((None, n_sel_tiles, LANE, SEL_TK), lambda bi, i: (bi, 0, 0, 0))]
                 + [wspec(w, False) for w in range(WIN_TILES)]
                 + [wspec(w, True) for w in range(WIN_TILES)]
                 + [pl.BlockSpec((nsp, NCP), lambda bi, i: (0, 0))],
        out_specs=pl.BlockSpec((None, TQ, NSA_DIM), lambda bi, i: (bi, i, 0)),
        out_shape=jax.ShapeDtypeStruct((b, lq, NSA_DIM), F32),
        scratch_shapes=[pltpu.VMEM((nsp, 2 * TQ), F32), pltpu.VMEM((nsp, 2 * TQ), F32),
                        pltpu.VMEM((nsp // SEL_PER, SEL_PER, 2 * TQ), F32),
                        pltpu.VMEM((NCP, 2 * TQ), F32),
                        pltpu.VMEM((NSA_KVH, NSA_HD, NSA_HPG * TQ), F32),
                        pltpu.VMEM((NSA_KVH, NSA_HD, NSA_HPG * TQ), F32),
                        pltpu.VMEM((8, NQL), F32), pltpu.VMEM((NSA_DIM, TQ), F32)],
        compiler_params=_cparams(("arbitrary", "arbitrary")),
        name="nsa_attention",
    )(q, gl, ckh, ckl, cvt, ks, vst, *([kwin] * WIN_TILES), *([vwt] * WIN_TILES), msel)


def _layer_norm(v, g, b):
    mu = jnp.mean(v, axis=-1, keepdims=True)
    vc = v - mu
    var = jnp.mean(vc * vc, axis=-1, keepdims=True)
    return vc * lax.rsqrt(var + LN_EPS) * g + b


def _merge_kernel(x_ref, ya_ref, yb_ref, yc_ref, gate_ref, g1_ref, wa_ref, wb_ref, wc_ref, wo_ref,
                  lg_ref, lb_ref, o_ref):
    d = D_MODEL
    merged = (_sigmoid(gate_ref[:, :d]) * _dot(ya_ref[...], wa_ref[...])
              + _sigmoid(gate_ref[:, d:2 * d]) * _dot(yb_ref[...], wb_ref[...])
              + _sigmoid(gate_ref[:, 2 * d:]) * _dot(yc_ref[...], wc_ref[...]))
    o = _dot(merged, wo_ref[...])
    o_ref[...] = _layer_norm(ALPHA * x_ref[...] + g1_ref[...] * o, lg_ref[...], lb_ref[...])


def merge_out(x, ya, yb, yc, gate, g1, wa, wb, wc, wo, lg, lb, tm=256):
    m, d = x.shape
    tm = min(tm, m)
    per_row = g1.shape[0] != 1
    row = lambda w: pl.BlockSpec((tm, w), lambda i: (i, 0))
    mspec = row(d) if per_row else pl.BlockSpec((1, d), lambda i: (0, 0))
    wspec = pl.BlockSpec((d, d), lambda i: (0, 0))
    vspec = pl.BlockSpec((1, d), lambda i: (0, 0))
    return pl.pallas_call(
        _merge_kernel,
        grid=(m // tm,),
        in_specs=[row(d), row(d), row(d), row(d), row(3 * d), mspec, wspec, wspec, wspec, wspec, vspec, vspec],
        out_specs=row(d),
        out_shape=jax.ShapeDtypeStruct((m, d), F32),
        compiler_params=_cparams(("arbitrary",)),
        name="merge_out",
    )(x, ya, yb, yc, gate, g1, wa, wb, wc, wo, lg, lb)


def _lane_first(mask, lane_f):
    return jnp.min(jnp.where(mask, lane_f, 1e9), axis=-1, keepdims=True)


def _moe_kernel(x_ref, sc_ref, sh_ref, g2_ref, wr_ref, w1_ref, w3_ref, w2_ref, lg_ref, lb_ref,
                o_ref, u_scr, gate_scr, acc_scr):
    e = pl.program_id(1)
    tm = x_ref.shape[0]
    lane = _iota((tm, LANE), 1)
    lane_f = lane.astype(F32)

    @pl.when(e == 0)
    def _():
        u = x_ref[...] * (1.0 + sc_ref[...]) + sh_ref[...]
        u_scr[...] = u.astype(BF16)
        logits = _dot_hi(u, wr_ref[...])
        lg = jnp.where(lane < N_GROUPS, logits, -jnp.inf)
        gmax = jnp.max(lg, axis=-1, keepdims=True)
        gstar = _lane_first(lg == gmax, lane_f)
        pg = 1.0 / jnp.sum(jnp.exp(lg - gmax), axis=-1, keepdims=True)
        in_grp = (lane >= N_GROUPS) & (lane < N_GROUPS + N_EXPERTS) & (
            lax.shift_right_logical(lane - N_GROUPS, 2).astype(F32) == gstar)
        le = jnp.where(in_grp, logits, -jnp.inf)
        v1 = jnp.max(le, axis=-1, keepdims=True)
        i1 = _lane_first(le == v1, lane_f)
        le2 = jnp.where(lane_f == i1, -jnp.inf, le)
        v2 = jnp.max(le2, axis=-1, keepdims=True)
        i2 = _lane_first(le2 == v2, lane_f)
        e2 = jnp.exp(v2 - v1)
        den = 1.0 / (1.0 + e2)
        gate_scr[...] = jnp.where(lane_f == i1, den * pg, jnp.where(lane_f == i2, e2 * den * pg, 0.0))
        acc_scr[...] = jnp.zeros(acc_scr.shape, F32)

    u = u_scr[...]
    ge = jnp.sum(jnp.where(lane == e + N_GROUPS, gate_scr[...], 0.0), axis=-1, keepdims=True)
    h = _silu(_dot(u, w1_ref[...])) * _dot(u, w3_ref[...])
    acc_scr[...] += _dot(h * ge, w2_ref[...])

    @pl.when(e == N_EXPERTS - 1)
    def _():
        o_ref[...] = _layer_norm(ALPHA * x_ref[...] + g2_ref[...] * acc_scr[...], lg_ref[...], lb_ref[...])


def moe_out(x, sc, sh, g2, wr, w1, w3, w2, lg, lb, tm=512):
    m, d = x.shape
    tm = min(tm, m)
    per_row = sc.shape[0] != 1
    row = pl.BlockSpec((tm, d), lambda i, e: (i, 0))
    mspec = row if per_row else pl.BlockSpec((1, d), lambda i, e: (0, 0))
    vspec = pl.BlockSpec((1, d), lambda i, e: (0, 0))
    return pl.pallas_call(
        _moe_kernel,
        grid=(m // tm, N_EXPERTS),
        in_specs=[row, mspec, mspec, mspec, pl.BlockSpec((d, LANE), lambda i, e: (0, 0)),
                  pl.BlockSpec((None, d, EXPERT_HIDDEN), lambda i, e: (e, 0, 0)),
                  pl.BlockSpec((None, d, EXPERT_HIDDEN), lambda i, e: (e, 0, 0)),
                  pl.BlockSpec((None, EXPERT_HIDDEN, d), lambda i, e: (e, 0, 0)), vspec, vspec],
        out_specs=row,
        out_shape=jax.ShapeDtypeStruct((m, d), F32),
        scratch_shapes=[pltpu.VMEM((tm, d), BF16), pltpu.VMEM((tm, LANE), F32), pltpu.VMEM((tm, d), F32)],
        compiler_params=_cparams(("arbitrary", "arbitrary")),
        name="moe_out",
    )(x, sc, sh, g2, wr, w1, w3, w2, lg, lb)


def _pad_rows(a, n):
    return a if a.shape[1] == n else jnp.pad(a, ((0, 0), (0, n - a.shape[1]), (0, 0)))


def _layer_weights(l, p):
    w_in = p['w_in'][l]
    o1, o2, o3 = SSD_COLS, SSD_COLS + RWKV_COLS, SSD_COLS + RWKV_COLS + NSA_COLS
    padc = lambda w, n: jnp.pad(w, ((0, 0), (0, n - w.shape[1])))
    w_ssd = padc(w_in[:, :o1], SSD_W)
    w_nsa = padc(w_in[:, o2:o3], NSA_W)
    z64 = jnp.zeros((64, RWKV_DIM), F32)
    seg = _head_onehot(RWKV_HEADS, RWKV_HD, LANE).T
    rwkv_prm = (p['rwkv_mu'][l][None], p['rwkv_w0'][l][None], p['rwkv_a0'][l][None], p['rwkv_k_k'][l][None],
                p['rwkv_k_a'][l][None], p['rwkv_r_k'][l].reshape(1, RWKV_DIM),
                jnp.concatenate([p['rwkv_w2'][l], z64], 0), jnp.concatenate([z64, p['rwkv_a2'][l]], 0),
                p['rwkv_g2'][l], seg, seg.T)
    wr = jnp.pad(jnp.concatenate([p['w_group'][l], p['w_router'][l]], axis=1),
                 ((0, 0), (0, LANE - N_GROUPS - N_EXPERTS)))
    row = lambda v: v[None]
    return dict(
        w_ssd=w_ssd.astype(BF16), w_rwkv=w_in[:, o1:o2].astype(BF16), w_nsa=w_nsa.astype(BF16),
        w_gate=w_in[:, o3:].astype(BF16),
        ssd=(jnp.pad(p['ssd_conv_w'][l], ((0, 4), (0, 0))), row(p['ssd_conv_b'][l]),
             row(jnp.pad(p['ssd_dt_bias'][l], (0, LANE - SSD_HEADS))), row(jnp.pad(p['ssd_a_log'][l], (0, LANE - SSD_HEADS))),
             row(jnp.repeat(p['ssd_d'][l], SSD_HD)), row(p['ssd_norm_w'][l])),
        rwkv=rwkv_prm, lnx=(row(p['rwkv_lnx_w'][l]), row(p['rwkv_lnx_b'][l])),
        cmp=_cmp_weights(p['cmp_pe'][l], p['cmp_w1'][l], p['cmp_w2'][l]),
        wo=tuple(p[k][l].astype(BF16) for k in ('w_o_ssd', 'w_o_rwkv', 'w_o_nsa', 'w_out')),
        ln1=(row(p['ln1_g'][l]), row(p['ln1_b'][l])), ln2=(row(p['ln2_g'][l]), row(p['ln2_b'][l])),
        wr=wr, w1=p['moe_w1'][l].astype(BF16), w3=p['moe_w3'][l].astype(BF16), w2=p['moe_w2'][l].astype(BF16))


def _trunk_layer(x, mod, l, w, pos0, conv_st, ssm_st, shift_st, rwkv_st, cache_win, cache_cmp, cache_sel,
                 page_table):
    b, L, d = x.shape
    m = b * L
    x2 = x.reshape(m, d)
    sh1, sc1, g1, sh2, sc2, g2 = [mod[:, k * d:(k + 1) * d] for k in range(6)]
    if b > 1:
        sh1, sc1, g1, sh2, sc2, g2 = [jnp.repeat(t, L, axis=0) for t in (sh1, sc1, g1, sh2, sc2, g2)]
    proj = lambda wt: mod_proj(x2, sc1, sh1, wt).reshape(b, L, -1)
    c_ssd, c_rwkv, c_nsa, c_gate = proj(w['w_ssd']), proj(w['w_rwkv']), proj(w['w_nsa']), proj(w['w_gate'])

    lp = -(-L // SSD_Q) * SSD_Q
    cst8 = jnp.pad(conv_st, ((0, 0), (8 - (SSD_CONV - 1), 0), (0, 0)))
    h0t = jnp.transpose(ssm_st.reshape(b, SSD_INNER, SSD_STATE), (0, 2, 1))
    y_a, ht = ssd_mixer(_pad_rows(c_ssd, lp), cst8, h0t, L, *w['ssd'])
    y_a = y_a[:, :L]
    ssm_new = jnp.transpose(ht, (0, 2, 1)).reshape(b, SSD_HEADS, SSD_HD, SSD_STATE)
    conv_new = c_ssd[:, L - (SSD_CONV - 1):, SSD_INNER:SSD_INNER + SSD_CONV_DIM]

    lp = -(-L // RWKV_C) * RWKV_C
    sh8 = jnp.pad(shift_st[:, None, :], ((0, 0), (7, 0), (0, 0)))
    pre = rwkv_prep(_pad_rows(c_rwkv, lp), sh8, L, w['rwkv'])
    y_b, sp = rwkv_scan(*pre, _pair_blockdiag(rwkv_st), *w['lnx'])
    y_b = y_b[:, :L]
    rwkv_new = _pair_unblock(sp)
    shift_new = c_rwkv[:, -1]

    pek, pev, wk, wv, w2k, w2v = w['cmp']
    kvrow = (2, NSA_KVH, NSA_HD)
    rows = lambda tile: c_nsa[:, :, tile * LANE:(tile + 2) * LANE]
    if cache_cmp is None:
        hp = cmp_h_rows(c_nsa, wk, wv, NSA_CMP_TILE)
        hp = _pad_rows(hp, NCP)
        hn = jnp.zeros((b, 8, CMP_HW), F32)
        ks, vst = kvprep_rows(c_nsa, SEL_TK, NSA_SEL_TILE)
        kwin, vwt = kvprep_rows(c_nsa, WIN_TK, NSA_WIN_TILE)
        cq, n_buf, n_win = c_nsa, 0, L
        win_new = rows(NSA_WIN_TILE)[:, max(L - WINDOW, 0):]
    else:
        cq = _pad_rows(c_nsa, TQ)
        hp = _pad_rows(cmp_h_pages(cache_cmp, page_table, l, wk, wv), NCP)
        hn = cmp_h_rows(cq, wk, wv, NSA_CMP_TILE)
        ks, vst = kvprep_pages(cache_sel, page_table, l, cq, NSA_SEL_TILE)
        n_buf = cache_win.shape[1]
        win_all = jnp.concatenate([cache_win.reshape(b, n_buf, 2 * LANE), rows(NSA_WIN_TILE)], axis=1)
        n_win = n_buf + L
        kwin, vwt = kvprep_rows(_pad_rows(win_all, -(-(n_buf + TQ) // WIN_TK) * WIN_TK), WIN_TK)
        win_new = win_all[:, n_win - min(WINDOW, n_win):]
    hn_row = NCP - 1 if cache_cmp is None else pos0 // CMP_STRIDE - 1
    ckh, ckl, cvt = cmp_finish(hp, hn, hn_row, pek, pev, wk, wv, w2k, w2v)
    y_c = nsa_attention(cq, ckh, ckl, cvt, ks, vst, kwin, vwt, pos0, L, n_buf, n_win)[:, :L]
    cmp_rows = rows(NSA_CMP_TILE).reshape((b, L) + kvrow)
    sel_rows = rows(NSA_SEL_TILE).reshape((b, L) + kvrow)
    win_new = win_new.reshape(win_new.shape[:2] + kvrow)

    flat = lambda t: t.reshape(m, -1)
    x1 = merge_out(x2, flat(y_a), flat(y_b), flat(y_c), flat(c_gate), g1, *w['wo'], *w['ln1'])
    x_out = moe_out(x1, sc2, sh2, g2, w['wr'], w['w1'], w['w3'], w['w2'], *w['ln2'])
    return x_out.reshape(b, L, d), (cmp_rows, sel_rows, win_new, ssm_new, conv_new, rwkv_new, shift_new)


def kernel(x_prompt, x_sample, c_prompt, c_sample, cache_cmp, cache_sel, cache_win, state_ssm, state_ssm_conv,
           state_rwkv, state_rwkv_shift, page_table, w_ada, b_ada, w_in, ssd_conv_w, ssd_conv_b, ssd_dt_bias,
           ssd_a_log, ssd_d, ssd_norm_w, rwkv_mu, rwkv_w0, rwkv_w2, rwkv_a0, rwkv_a2, rwkv_g2, rwkv_k_k, rwkv_k_a,
           rwkv_r_k, rwkv_lnx_w, rwkv_lnx_b, cmp_pe, cmp_w1, cmp_w2, w_o_ssd, w_o_rwkv, w_o_nsa, w_out, ln1_g,
           ln1_b, ln2_g, ln2_b, w_group, w_router, moe_w1, moe_w3, moe_w2):
    p = dict(w_in=w_in, ssd_conv_w=ssd_conv_w, ssd_conv_b=ssd_conv_b, ssd_dt_bias=ssd_dt_bias, ssd_a_log=ssd_a_log,
             ssd_d=ssd_d, ssd_norm_w=ssd_norm_w, rwkv_mu=rwkv_mu, rwkv_w0=rwkv_w0, rwkv_w2=rwkv_w2, rwkv_a0=rwkv_a0,
             rwkv_a2=rwkv_a2, rwkv_g2=rwkv_g2, rwkv_k_k=rwkv_k_k, rwkv_k_a=rwkv_k_a, rwkv_r_k=rwkv_r_k,
             rwkv_lnx_w=rwkv_lnx_w, rwkv_lnx_b=rwkv_lnx_b, cmp_pe=cmp_pe, cmp_w1=cmp_w1, cmp_w2=cmp_w2,
             w_o_ssd=w_o_ssd, w_o_rwkv=w_o_rwkv, w_o_nsa=w_o_nsa, w_out=w_out, ln1_g=ln1_g, ln1_b=ln1_b,
             ln2_g=ln2_g, ln2_b=ln2_b, w_group=w_group, w_router=w_router, moe_w1=moe_w1, moe_w3=moe_w3,
             moe_w2=moe_w2)
    bp, bs = x_prompt.shape[0], x_sample.shape[0]
    past_len = page_table.shape[1] * PAGE
    nb = -(-(bp + bs) // SUBLANE) * SUBLANE
    c_all = jnp.pad(jnp.concatenate([c_prompt, c_sample], axis=0), ((0, nb - bp - bs), (0, 0)))
    mod = ada_mod(c_all, w_ada, b_ada)
    n_phys = cache_cmp.shape[1]
    cmp_pages = cache_cmp.reshape(DEPTH, n_phys, PAGE, 2 * LANE)
    sel_pages = cache_sel.reshape(DEPTH, n_phys, PAGE, 2 * LANE)
    zeros = lambda *s: jnp.zeros(s, F32)
    xp, xs = x_prompt, x_sample
    st_p, st_s = [], []
    for l in range(DEPTH):
        w = _layer_weights(l, p)
        xp, sp_l = _trunk_layer(xp, mod[l, :bp], l, w, 0, zeros(bp, SSD_CONV - 1, SSD_CONV_DIM),
                                zeros(bp, SSD_HEADS, SSD_HD, SSD_STATE), zeros(bp, RWKV_COLS),
                                zeros(bp, RWKV_HEADS, RWKV_HD, RWKV_HD), None, None, None, None)
        xs, ss_l = _trunk_layer(xs, mod[l, bp:bp + bs], l, w, past_len, state_ssm_conv[l], state_ssm[l],
                                state_rwkv_shift[l], state_rwkv[l], cache_win[l], cmp_pages, sel_pages, page_table)
        st_p.append(sp_l)
        st_s.append(ss_l)
    sp = [jnp.stack(z) for z in zip(*st_p)]
    ss = [jnp.stack(z) for z in zip(*st_s)]
    return (xp, xs, sp[0], sp[1], sp[2], sp[3], sp[4], sp[5], sp[6], ss[0], ss[1], ss[2], ss[3], ss[4], ss[5], ss[6])
```

```python
import functools
import math

import jax
import jax.numpy as jnp
from jax import lax
from jax.experimental import pallas as pl
from jax.experimental.pallas import tpu as pltpu

F32 = jnp.float32
BF16 = jnp.bfloat16
HIGHEST = lax.Precision.HIGHEST

D_MODEL = 1024
DEPTH = 2
PAGE = 128
SSD_HEADS, SSD_HD, SSD_GROUPS, SSD_STATE, SSD_CONV = 16, 64, 2, 128, 4
SSD_INNER = SSD_HEADS * SSD_HD
SSD_CONV_DIM = SSD_INNER + 2 * SSD_GROUPS * SSD_STATE
SSD_COLS = SSD_INNER + SSD_CONV_DIM + SSD_HEADS
RWKV_HEADS, RWKV_HD = 16, 64
RWKV_DIM = RWKV_HEADS * RWKV_HD
RWKV_COLS = 3 * RWKV_DIM + 64 + 64 + 128
RWKV_LNX_EPS = 64e-5
NSA_HEADS, NSA_KVH, NSA_HPG, NSA_HD = 16, 2, 8, 64
NSA_DIM = NSA_HEADS * NSA_HD
NSA_COLS = NSA_DIM + 3 * 2 * NSA_KVH * NSA_HD + 3 * NSA_HEADS
CMP_BLOCK, CMP_STRIDE, CMP_HIDDEN = 32, 16, 128
SEL_BLOCK, N_SEL, WINDOW = 64, 16, 512
FORCE_BONUS = 1e4
N_GROUPS, EPG, N_EXPERTS, EXPERT_HIDDEN = 4, 4, 16, 256
ALPHA = (2 * DEPTH) ** 0.25
LN_EPS = 1e-5
RMS_EPS = 1e-5
NEG = -1e30

LANE = 128
SUBLANE = 8
VMEM_LIMIT = 56 * 1024 * 1024


def _cparams(sem):
    return pltpu.CompilerParams(dimension_semantics=sem, vmem_limit_bytes=VMEM_LIMIT)


def _dot(a, b):
    return jnp.dot(a.astype(BF16), b.astype(BF16), preferred_element_type=F32)


def _dot_hi(a, b):
    return jnp.dot(a, b, precision=HIGHEST, preferred_element_type=F32)


def _dot_nt(a, b):
    return lax.dot_general(a.astype(BF16), b.astype(BF16), (((1,), (1,)), ((), ())),
                           preferred_element_type=F32)


def _dot_nt_hi(a, b):
    return lax.dot_general(a, b, (((1,), (1,)), ((), ())), precision=HIGHEST,
                           preferred_element_type=F32)


def _split16(a):
    hi = a.astype(BF16)
    return hi, (a - hi.astype(F32)).astype(BF16)


def _split24(a):
    h1 = a.astype(BF16)
    r1 = a - h1.astype(F32)
    h2 = r1.astype(BF16)
    return h1, h2, (r1 - h2.astype(F32)).astype(BF16)


def _dot_w01(x, w16):
    return sum(_dot(piece, w16) for piece in _split24(x))


def _dot3(a, b):
    (ah, al), (bh, bl) = a, b
    return _dot(ah, bh) + _dot(ah, bl) + _dot(al, bh)


def _dot3_nt(a, b):
    (ah, al), (bh, bl) = a, b
    return _dot_nt(ah, bh) + _dot_nt(ah, bl) + _dot_nt(al, bh)


def _sigmoid(x):
    return 1.0 / (1.0 + jnp.exp(-x))


def _silu(x):
    return x * _sigmoid(x)


def _softplus(x):
    return jnp.maximum(x, 0.0) + jnp.log(1.0 + jnp.exp(-jnp.abs(x)))


def _iota(shape, dim):
    return lax.broadcasted_iota(jnp.int32, shape, dim)


def _head_onehot(n_heads, hd, pad_rows):
    r = jnp.arange(pad_rows)[:, None]
    c = jnp.arange(n_heads * hd)[None, :] // hd
    return (r == c).astype(F32)


def _ada_kernel(c_ref, w_ref, b_ref, o_ref):
    o_ref[...] = _dot_hi(_silu(c_ref[...]), w_ref[...]) + b_ref[...]


def ada_mod(c_all, w_ada, b_ada):
    nb = c_all.shape[0]
    return pl.pallas_call(
        _ada_kernel,
        grid=(DEPTH, 6),
        in_specs=[pl.BlockSpec((nb, D_MODEL), lambda l, j: (0, 0)),
                  pl.BlockSpec((None, D_MODEL, D_MODEL), lambda l, j: (l, 0, j)),
                  pl.BlockSpec((None, 1, D_MODEL), lambda l, j: (l, 0, j))],
        out_specs=pl.BlockSpec((None, nb, D_MODEL), lambda l, j: (l, 0, j)),
        out_shape=jax.ShapeDtypeStruct((DEPTH, nb, 6 * D_MODEL), F32),
        compiler_params=_cparams(("arbitrary", "arbitrary")),
        name="ada_mod",
    )(c_all, w_ada, b_ada.reshape(DEPTH, 1, 6 * D_MODEL))


def _inproj_kernel(x_ref, sc_ref, sh_ref, w_ref, o_ref):
    u = x_ref[...] * (1.0 + sc_ref[...]) + sh_ref[...]
    o_ref[...] = _dot(u, w_ref[...])


def mod_proj(x, sc, sh, w, tm=256):
    m, k = x.shape
    n = w.shape[1]
    tm = min(tm, m)
    per_row = sc.shape[0] != 1
    mspec = (pl.BlockSpec((tm, k), lambda i: (i, 0)) if per_row
             else pl.BlockSpec((1, k), lambda i: (0, 0)))
    return pl.pallas_call(
        _inproj_kernel,
        grid=(m // tm,),
        in_specs=[pl.BlockSpec((tm, k), lambda i: (i, 0)), mspec, mspec,
                  pl.BlockSpec((k, n), lambda i: (0, 0))],
        out_specs=pl.BlockSpec((tm, n), lambda i: (i, 0)),
        out_shape=jax.ShapeDtypeStruct((m, n), F32),
        compiler_params=_cparams(("arbitrary",)),
        name="mod_proj",
    )(x, sc, sh, w)


SSD_Q = 128
SSD_W = SSD_INNER + SSD_CONV_DIM + LANE


def _ssd_kernel(n_valid, zxd_ref, cst_ref, h0_ref, cw_ref, cb_ref, dtb_ref, alog_ref, dexp_ref,
                nw_ref, e16_ref, tri_ref, y_ref, hout_ref, ext, hT):
    j = pl.program_id(1)
    q = SSD_Q

    @pl.when(j == 0)
    def _():
        ext[0:8, :] = cst_ref[...]
        hT[...] = h0_ref[...]

    ext[8:8 + q, :] = zxd_ref[:, SSD_INNER:SSD_INNER + SSD_CONV_DIM]
    conv = (cb_ref[...] + ext[5:5 + q, :] * cw_ref[0:1, :] + ext[6:6 + q, :] * cw_ref[1:2, :]
            + ext[7:7 + q, :] * cw_ref[2:3, :] + ext[8:8 + q, :] * cw_ref[3:4, :])
    ext[0:8, :] = ext[q:q + 8, :]
    xbc = _silu(conv)
    xs = xbc[:, :SSD_INNER]
    bm = xbc[:, SSD_INNER:SSD_INNER + 2 * SSD_STATE]
    cm = xbc[:, SSD_INNER + 2 * SSD_STATE:]

    row = _iota((q, LANE), 0)
    lane = _iota((q, LANE), 1)
    dt = _softplus(zxd_ref[:, SSD_INNER + SSD_CONV_DIM:] + dtb_ref[...])
    dt = jnp.where((lane < SSD_HEADS) & (row + j * q < n_valid), dt, 0.0)
    a = -jnp.exp(alog_ref[...])
    acum = _dot_hi(tri_ref[...], dt * a)
    a_last = acum[q - 1:q, :]
    e16 = e16_ref[...]
    dt_e = _dot_hi(dt, e16)
    ea_e = _dot_hi(jnp.exp(acum), e16)
    dte_e = _dot_hi(jnp.exp(a_last - acum), e16)
    elast_e = ea_e[q - 1:q, :]
    xdt = xs * dt_e
    acum_t = acum.T
    tril = row >= lane
    lo_half = lane < SSD_HD

    y_tiles = []
    for g in range(SSD_GROUPS):
        cc = cm[:, g * SSD_STATE:(g + 1) * SSD_STATE]
        bc = bm[:, g * SSD_STATE:(g + 1) * SSD_STATE]
        cb = _dot_nt(cc, bc)
        gs = slice(g * 512, (g + 1) * 512)
        h_in = hT[:, gs]
        y_off = _dot(cc, h_in) * ea_e[:, gs]
        hT[:, gs] = h_in * elast_e[:, gs] + _dot(bc.T, xdt[:, gs] * dte_e[:, gs])
        for tl in range(4):
            t = 4 * g + tl
            xt = xdt[:, t * LANE:(t + 1) * LANE]
            yd = jnp.zeros((q, LANE), F32)
            for sub in range(2):
                h = 2 * t + sub
                diff = acum[:, h:h + 1] - acum_t[h:h + 1, :]
                m = cb * jnp.exp(jnp.where(tril, diff, NEG))
                yd = yd + _dot(m, jnp.where(lo_half if sub == 0 else ~lo_half, xt, 0.0))
            y_tiles.append(yd + y_off[:, tl * LANE:(tl + 1) * LANE])
    y = jnp.concatenate(y_tiles, axis=1) + dexp_ref[...] * xs
    y = y * _silu(zxd_ref[:, :SSD_INNER])
    outs = []
    for g in range(SSD_GROUPS):
        yg = y[:, g * 512:(g + 1) * 512]
        ms = jnp.sum(yg * yg, axis=-1, keepdims=True) * (1.0 / 512.0)
        outs.append(yg * lax.rsqrt(ms + RMS_EPS))
    y_ref[...] = jnp.concatenate(outs, axis=1) * nw_ref[...]

    @pl.when(j == pl.num_programs(1) - 1)
    def _():
        hout_ref[...] = hT[...]


def ssd_mixer(zxd, conv_st8, h0t, n_valid, cw8, cb, dtb, alog, dexp, nw):
    b, lp, _ = zxd.shape
    nj = lp // SSD_Q
    e16 = _head_onehot(SSD_HEADS, SSD_HD, LANE)
    tri = (jnp.arange(SSD_Q)[:, None] >= jnp.arange(SSD_Q)[None, :]).astype(F32)
    full = lambda shp: pl.BlockSpec(shp, lambda bi, j: (0,) * len(shp))
    return pl.pallas_call(
        functools.partial(_ssd_kernel, n_valid),
        grid=(b, nj),
        in_specs=[pl.BlockSpec((None, SSD_Q, SSD_W), lambda bi, j: (bi, j, 0)),
                  pl.BlockSpec((None, 8, SSD_CONV_DIM), lambda bi, j: (bi, 0, 0)),
                  pl.BlockSpec((None, SSD_STATE, SSD_INNER), lambda bi, j: (bi, 0, 0)),
                  full((8, SSD_CONV_DIM)), full((1, SSD_CONV_DIM)), full((1, LANE)), full((1, LANE)),
                  full((1, SSD_INNER)), full((1, SSD_INNER)), full((LANE, SSD_INNER)),
                  full((SSD_Q, SSD_Q))],
        out_specs=[pl.BlockSpec((None, SSD_Q, SSD_INNER), lambda bi, j: (bi, j, 0)),
                   pl.BlockSpec((None, SSD_STATE, SSD_INNER), lambda bi, j: (bi, 0, 0))],
        out_shape=[jax.ShapeDtypeStruct((b, lp, SSD_INNER), F32),
                   jax.ShapeDtypeStruct((b, SSD_STATE, SSD_INNER), F32)],
        scratch_shapes=[pltpu.VMEM((SSD_Q + 8, SSD_CONV_DIM), F32),
                        pltpu.VMEM((SSD_STATE, SSD_INNER), F32)],
        compiler_params=_cparams(("arbitrary", "arbitrary")),
        name="ssd_mixer",
    )(zxd, conv_st8, h0t, cw8, cb, dtb, alog, dexp, nw, e16, tri)


def _rwkv_prep_kernel(n_valid, tr, x_ref, sh8_ref, mu_ref, w0_ref, a0_ref, kk_ref, ka_ref, rk_ref,
                      w2_ref, a2_ref, g2_ref, seg_ref, e16_ref,
                      r_o, ld_o, k_o, v_o, kk_o, bb_o, g_o, bv_o, ext):
    j = pl.program_id(1)

    @pl.when(j == 0)
    def _():
        ext[0:8, :] = sh8_ref[...]

    x = x_ref[...]
    ext[8:8 + tr, :] = x
    prev = ext[7:7 + tr, :]
    ext[0:8, :] = ext[tr:tr + 8, :]
    xm = x + (prev - x) * mu_ref[...]
    d = RWKV_DIM
    r, k, v = xm[:, :d], xm[:, d:2 * d], xm[:, 2 * d:3 * d]
    lo = xm[:, 3 * d:3 * d + LANE]
    glo = xm[:, 3 * d + LANE:]
    w = w0_ref[...] + _dot3(_split16(jnp.tanh(lo)), _split16(w2_ref[...]))
    ld = -jnp.exp(-_softplus(-w) - 0.5)
    a = _sigmoid(a0_ref[...] + _dot3(_split16(lo), _split16(a2_ref[...])))
    g = _dot(_sigmoid(glo), g2_ref[...])
    seg, e16 = seg_ref[...].astype(BF16), e16_ref[...].astype(BF16)
    kk = k * kk_ref[...]
    ss = _dot_w01(kk * kk, seg)
    kk = kk * _dot_w01(lax.rsqrt(jnp.maximum(ss, 1e-24)), e16)
    k2 = k * (1.0 + (a - 1.0) * ka_ref[...])
    bonus = _dot_w01(_dot_w01(r * k2 * rk_ref[...], seg), e16)
    valid = (_iota((tr, 1), 0) + j * tr) < n_valid
    zero = lambda t: jnp.where(valid, t, 0.0)
    r_o[...] = r
    ld_o[...] = zero(ld)
    k_o[...] = zero(k2)
    v_o[...] = zero(v)
    kk_o[...] = zero(kk)
    bb_o[...] = zero(kk * a)
    g_o[...] = g
    bv_o[...] = bonus * v


def rwkv_prep(cols, shift8, n_valid, prm):
    b, lp, _ = cols.shape
    tr = min(128, lp)
    d = RWKV_DIM
    full = lambda shp: pl.BlockSpec(shp, lambda bi, j: (0,) * len(shp))
    row = lambda: pl.BlockSpec((None, tr, d), lambda bi, j: (bi, j, 0))
    return pl.pallas_call(
        functools.partial(_rwkv_prep_kernel, n_valid, tr),
        grid=(b, lp // tr),
        in_specs=[pl.BlockSpec((None, tr, RWKV_COLS), lambda bi, j: (bi, j, 0)),
                  pl.BlockSpec((None, 8, RWKV_COLS), lambda bi, j: (bi, 0, 0)),
                  full((1, RWKV_COLS))] + [full((1, d))] * 5 + [full((LANE, d))] * 3
                 + [full((d, LANE)), full((LANE, d))],
        out_specs=[row() for _ in range(8)],
        out_shape=[jax.ShapeDtypeStruct((b, lp, d), F32)] * 8,
        scratch_shapes=[pltpu.VMEM((tr + 8, RWKV_COLS), F32)],
        compiler_params=_cparams(("arbitrary", "arbitrary")),
        name="rwkv_prep",
    )(cols, shift8, *prm)


RWKV_C = 64


RWKV_NPP = 2


def _rwkv_scan_kernel(nch, r_ref, ld_ref, k_ref, v_ref, kk_ref, bb_ref, g_ref, bv_ref, s0_ref,
                      lnw_ref, lnb_ref, y_ref, sout_ref, st):
    i = pl.program_id(2)
    c = RWKV_C

    @pl.when(i == 0)
    def _():
        st[...] = s0_ref[...]

    n2 = 2 * c
    rr = _iota((n2, n2), 0)
    cc = _iota((n2, n2), 1)
    eye = (rr == cc).astype(F32)
    upper = rr < cc
    upper_eq = rr <= cc
    tri = (_iota((c, c), 0) >= _iota((c, c), 1)).astype(BF16)
    lane = _iota((c, LANE), 1)
    m0 = lane < RWKV_HD
    hmean = jnp.where((rr < RWKV_HD) == (cc < RWKV_HD), 1.0 / RWKV_HD, 0.0).astype(BF16)

    def stack(t):
        return jnp.concatenate([jnp.where(m0, t, 0.0), jnp.where(m0, 0.0, t)], axis=0)

    def head_mean(t):
        return sum(_dot(piece, hmean) for piece in _split24(t))

    units = [(ci, pi) for ci in range(nch) for pi in range(RWKV_NPP)]
    pre = []
    for ci, pi in units:
        sl = pl.ds(ci * c, c)
        ls = slice(pi * LANE, (pi + 1) * LANE)
        r, ld, k, v, kk, bb = (ref[sl, ls] for ref in (r_ref, ld_ref, k_ref, v_ref, kk_ref, bb_ref))
        cum = sum(_dot(tri, piece) for piece in _split24(ld))
        p_in = jnp.exp(cum)
        p_inv = jnp.exp(-cum)
        p_c = p_in[c - 1:c, :]
        kks = stack(kk * jnp.exp(cum - ld))
        rs = stack(r * p_in)
        bs = stack(bb * p_inv)
        ks = stack(k * p_inv)
        kks16, rs16, bs16, ks16 = (t.astype(BF16) for t in (kks, rs, bs, ks))
        pre.append(dict(
            p_c=p_c, vs_t=_split16(stack(v).T), kks_t=kks.T.astype(BF16), rs_t=rs.T.astype(BF16),
            upd_r=_split16(jnp.concatenate([bs * p_c, ks * p_c], axis=0)),
            nt=jnp.where(upper, _dot_nt(bs16, kks16), 0.0),
            avk=jnp.where(upper, _dot_nt(ks16, kks16), 0.0).astype(BF16),
            arb=jnp.where(upper_eq, _dot_nt(bs16, rs16), 0.0).astype(BF16),
            ark=jnp.where(upper_eq, _dot_nt(ks16, rs16), 0.0).astype(BF16)))
    tinv = [eye - u['nt'] for u in pre]
    pw = [u['nt'].astype(BF16) for u in pre]
    for _ in range(5):
        pw = [_dot(w, w).astype(BF16) for w in pw]
        tinv = [t + _dot(t, w) for t, w in zip(tinv, pw)]
    tinv = [_split16(t) for t in tinv]

    for ci in range(nch):
        sl = pl.ds(ci * c, c)
        for pi in range(RWKV_NPP):
            ls = slice(pi * LANE, (pi + 1) * LANE)
            u = pre[ci * RWKV_NPP + pi]
            s = st[pi]
            s16 = s.astype(BF16)
            vs_t16 = u['vs_t'][0]
            ut = _dot3(_split16(_dot(s16, u['kks_t']) + _dot(vs_t16, u['avk'])), tinv[ci * RWKV_NPP + pi])
            ut_p = _split16(ut)
            yt = _dot(s16, u['rs_t']) - _dot(ut_p[0], u['arb']) + _dot(vs_t16, u['ark'])
            upd_l = tuple(jnp.concatenate([-a, b], axis=1) for a, b in zip(ut_p, u['vs_t']))
            st[pi] = s * u['p_c'] + _dot3(upd_l, u['upd_r'])
            ys = yt.T
            y = ys[:c, :] + ys[c:, :]
            yc = y - head_mean(y)
            var = head_mean(yc * yc)
            y = yc * lax.rsqrt(var + RWKV_LNX_EPS) * lnw_ref[:, ls] + lnb_ref[:, ls] + bv_ref[sl, ls]
            y_ref[sl, ls] = y * g_ref[sl, ls]

    @pl.when(i == pl.num_programs(2) - 1)
    def _():
        sout_ref[...] = st[...]


def rwkv_scan(r, ld, k, v, kk, bb, g, bv, s0p, lnw, lnb):
    b, lp, d = r.shape
    nch = min(4, lp // RWKV_C)
    rb = nch * RWKV_C
    npp = RWKV_NPP
    npair = d // LANE
    row = lambda: pl.BlockSpec((None, rb, npp * LANE), lambda bi, p, i: (bi, i, p))
    st = lambda: pl.BlockSpec((None, npp, LANE, LANE), lambda bi, p, i: (bi, p, 0, 0))
    vec = lambda: pl.BlockSpec((1, npp * LANE), lambda bi, p, i: (0, p))
    return pl.pallas_call(
        functools.partial(_rwkv_scan_kernel, nch),
        grid=(b, npair // npp, lp // rb),
        in_specs=[row() for _ in range(8)] + [st(), vec(), vec()],
        out_specs=[row(), st()],
        out_shape=[jax.ShapeDtypeStruct((b, lp, d), F32),
                   jax.ShapeDtypeStruct((b, npair, LANE, LANE), F32)],
        scratch_shapes=[pltpu.VMEM((npp, LANE, LANE), F32)],
        compiler_params=_cparams(("arbitrary", "arbitrary", "arbitrary")),
        name="rwkv_scan",
    )(r, ld, k, v, kk, bb, g, bv, s0p, lnw, lnb)


def _pair_blockdiag(s):
    b = s.shape[0]
    s = s.reshape(b, 8, 2, RWKV_HD, RWKV_HD)
    z = jnp.zeros_like(s[:, :, 0])
    top = jnp.concatenate([s[:, :, 0], z], axis=-1)
    bot = jnp.concatenate([z, s[:, :, 1]], axis=-1)
    return jnp.concatenate([top, bot], axis=-2)


def _pair_unblock(sp):
    b = sp.shape[0]
    return jnp.stack([sp[:, :, :RWKV_HD, :RWKV_HD], sp[:, :, RWKV_HD:, RWKV_HD:]], axis=2).reshape(
        b, RWKV_HEADS, RWKV_HD, RWKV_HD)


CMP_HW = 4 * NSA_KVH * CMP_HIDDEN
NSEG_PAGE = PAGE // CMP_STRIDE


def _cmp_h_rows_kernel(nsr, xk_ref, xv_ref, wk_ref, wv_ref, o_ref):
    half = CMP_HW // 2
    acc_k = jnp.zeros((nsr, half), F32)
    acc_v = jnp.zeros((nsr, half), F32)
    for s in range(CMP_STRIDE):
        rows = pl.ds(s, nsr, stride=CMP_STRIDE)
        acc_k = acc_k + _dot(xk_ref[rows, :], wk_ref[s])
        acc_v = acc_v + _dot(xv_ref[rows, :], wv_ref[s])
    o_ref[...] = jnp.concatenate([acc_k, acc_v], axis=1)


def cmp_h_rows(rows, wk, wv, lt=0):
    b, t, _ = rows.shape
    tb = 2048 if t % 2048 == 0 else t
    nsr = tb // CMP_STRIDE
    wspec = pl.BlockSpec((CMP_STRIDE, LANE, CMP_HW // 2), lambda bi, i: (0, 0, 0))
    return pl.pallas_call(
        functools.partial(_cmp_h_rows_kernel, nsr),
        grid=(b, t // tb),
        in_specs=[pl.BlockSpec((None, tb, LANE), lambda bi, i: (bi, i, lt)),
                  pl.BlockSpec((None, tb, LANE), lambda bi, i: (bi, i, lt + 1)), wspec, wspec],
        out_specs=pl.BlockSpec((None, nsr, CMP_HW), lambda bi, i: (bi, i, 0)),
        out_shape=jax.ShapeDtypeStruct((b, t // CMP_STRIDE, CMP_HW), F32),
        compiler_params=_cparams(("arbitrary", "arbitrary")),
        name="cmp_h_rows",
    )(rows, rows, wk, wv)


PAGES_PER_STEP = 16
CMP_PAGES_PER_STEP = 32


def _cmp_h_pages_kernel(pt_ref, *refs):
    pp = CMP_PAGES_PER_STEP
    pk, pv = refs[:pp], refs[pp:2 * pp]
    wk_ref, wv_ref, o_ref = refs[2 * pp:]
    half = CMP_HW // 2
    acc_k = jnp.zeros((pp * NSEG_PAGE, half), F32)
    acc_v = jnp.zeros((pp * NSEG_PAGE, half), F32)
    for s in range(CMP_STRIDE):
        rows = pl.ds(s, NSEG_PAGE, stride=CMP_STRIDE)
        xk = jnp.concatenate([r[rows, :] for r in pk], axis=0)
        xv = jnp.concatenate([r[rows, :] for r in pv], axis=0)
        acc_k = acc_k + _dot(xk, wk_ref[s])
        acc_v = acc_v + _dot(xv, wv_ref[s])
    o_ref[...] = jnp.concatenate([acc_k, acc_v], axis=1)


def _page_specs(layer, lane_tile, n_pages, pp=PAGES_PER_STEP):
    def spec(p):
        def imap(bi, i, pt):
            return (layer, pt[bi, jnp.minimum(i * pp + p, n_pages - 1)], 0, lane_tile)
        return pl.BlockSpec((None, None, PAGE, LANE), imap)
    return [spec(p) for p in range(pp)]


def cmp_h_pages(cache, page_table, layer, wk, wv):
    b, n_pages = page_table.shape
    pp = CMP_PAGES_PER_STEP
    wspec = pl.BlockSpec((CMP_STRIDE, LANE, CMP_HW // 2), lambda bi, i, pt: (0, 0, 0))
    gs = pltpu.PrefetchScalarGridSpec(
        num_scalar_prefetch=1, grid=(b, n_pages // pp),
        in_specs=_page_specs(layer, 0, n_pages, pp) + _page_specs(layer, 1, n_pages, pp) + [wspec, wspec],
        out_specs=pl.BlockSpec((None, pp * NSEG_PAGE, CMP_HW), lambda bi, i, pt: (bi, i, 0)))
    return pl.pallas_call(
        _cmp_h_pages_kernel, grid_spec=gs,
        out_shape=jax.ShapeDtypeStruct((b, n_pages * NSEG_PAGE, CMP_HW), F32),
        compiler_params=_cparams(("arbitrary", "arbitrary")),
        name="cmp_h_pages",
    )(page_table, *([cache] * (2 * pp)), wk, wv)


NCP = 1024


def _cmp_finish_kernel(hn_row, hp_ref, hn_ref, pek_ref, pev_ref, wk_ref, wv_ref, w2k_ref, w2v_ref,
                       ckh_ref, ckl_ref, cvt_ref):
    qw = CMP_HW // 4
    rk = jnp.zeros((8, 2 * qw), F32)
    rv = jnp.zeros((8, 2 * qw), F32)
    for s in range(CMP_STRIDE):
        rk = rk + _dot(pek_ref[s], wk_ref[s])
        rv = rv + _dot(pev_ref[s], wv_ref[s])
    last = _iota((NCP, qw), 0) == hn_row

    def hidden(off, rb):
        h0 = hp_ref[:, off:off + qw]
        h1 = pltpu.roll(hp_ref[:, off + qw:off + 2 * qw], NCP - 1, 0)
        h1 = jnp.where(last, hn_ref[0:1, off + qw:off + 2 * qw], h1)
        return _silu(h0 + h1 + rb[0:1, :qw] + rb[1:2, qw:])

    ckh_ref[...], ckl_ref[...] = _split16(_dot(hidden(0, rk), w2k_ref[...]))
    cvt_ref[...] = _dot(hidden(2 * qw, rv), w2v_ref[...]).T.astype(BF16)


def cmp_finish(hp, hn, hn_row, pek, pev, wk, wv, w2k, w2v):
    b = hp.shape[0]
    full = lambda shp: pl.BlockSpec(shp, lambda bi: (0,) * len(shp))
    return pl.pallas_call(
        functools.partial(_cmp_finish_kernel, hn_row),
        grid=(b,),
        in_specs=[pl.BlockSpec((None, NCP, CMP_HW), lambda bi: (bi, 0, 0)),
                  pl.BlockSpec((None, 8, CMP_HW), lambda bi: (bi, 0, 0)),
                  full((CMP_STRIDE, 8, LANE)), full((CMP_STRIDE, 8, LANE)),
                  full((CMP_STRIDE, LANE, CMP_HW // 2)), full((CMP_STRIDE, LANE, CMP_HW // 2)),
                  full((CMP_HW // 4, LANE)), full((CMP_HW // 4, LANE))],
        out_specs=[pl.BlockSpec((None, NCP, LANE), lambda bi: (bi, 0, 0)),
                   pl.BlockSpec((None, NCP, LANE), lambda bi: (bi, 0, 0)),
                   pl.BlockSpec((None, LANE, NCP), lambda bi: (bi, 0, 0))],
        out_shape=[jax.ShapeDtypeStruct((b, NCP, LANE), BF16),
                   jax.ShapeDtypeStruct((b, NCP, LANE), BF16),
                   jax.ShapeDtypeStruct((b, LANE, NCP), BF16)],
        compiler_params=_cparams(("arbitrary",)),
        name="cmp_finish",
    )(hp, hn, pek, pev, wk, wv, w2k, w2v)


def _cmp_weights(pe, w1, w2):
    eye_g = jnp.eye(NSA_KVH, dtype=F32)

    def first(e):
        w = w1[e].reshape(2, CMP_STRIDE, NSA_HD, CMP_HIDDEN)
        w = jnp.einsum('isdf,gh->sgdihf', w, eye_g)
        return w.reshape(CMP_STRIDE, LANE, CMP_HW // 2).astype(BF16)

    def second(e):
        return jnp.einsum('fd,gh->gfhd', w2[e], eye_g).reshape(CMP_HW // 4, LANE).astype(BF16)

    def pos(e):
        p = pe[e].reshape(2, CMP_STRIDE, NSA_HD)
        p = jnp.tile(jnp.transpose(p, (1, 0, 2)), (1, 1, NSA_KVH))
        return jnp.pad(p, ((0, 0), (0, 6), (0, 0)))

    return pos(0), pos(1), first(0), first(1), second(0), second(1)


def _kvprep_rows_kernel(nsub, tk, xk_ref, xv_ref, k_ref, vt_ref):
    k_ref[...] = xk_ref[...].astype(BF16)
    for u in range(nsub):
        vt_ref[u] = xv_ref[u * tk:(u + 1) * tk, :].T.astype(BF16)


def kvprep_rows(rows, tk, lt=0):
    b, t, _ = rows.shape
    tt = 512 if t % 512 == 0 else tk
    nsub = tt // tk
    return pl.pallas_call(
        functools.partial(_kvprep_rows_kernel, nsub, tk),
        grid=(b, t // tt),
        in_specs=[pl.BlockSpec((None, tt, LANE), lambda bi, i: (bi, i, lt)),
                  pl.BlockSpec((None, tt, LANE), lambda bi, i: (bi, i, lt + 1))],
        out_specs=[pl.BlockSpec((None, tt, LANE), lambda bi, i: (bi, i, 0)),
                   pl.BlockSpec((None, nsub, LANE, tk), lambda bi, i: (bi, i, 0, 0))],
        out_shape=[jax.ShapeDtypeStruct((b, t, LANE), BF16),
                   jax.ShapeDtypeStruct((b, t // tk, LANE, tk), BF16)],
        compiler_params=_cparams(("arbitrary", "arbitrary")),
        name="kvprep_rows",
    )(rows, rows)


SEL_TK = 512
WIN_TK = 128


def _kvprep_pages_kernel(n_steps, pt_ref, *refs):
    pp = PAGES_PER_STEP
    pk, pv = refs[:pp], refs[pp:2 * pp]
    nk_ref, nv_ref, k_ref, vt_ref = refs[2 * pp:]
    i = pl.program_id(1)
    per = SEL_TK // PAGE

    @pl.when(i < n_steps - 1)
    def _():
        for p in range(pp):
            k_ref[p * PAGE:(p + 1) * PAGE, :] = pk[p][...].astype(BF16)
            vt_ref[p // per, :, (p % per) * PAGE:(p % per + 1) * PAGE] = pv[p][...].T.astype(BF16)

    @pl.when(i == n_steps - 1)
    def _():
        k_ref[...] = jnp.zeros(k_ref.shape, BF16)
        vt_ref[...] = jnp.zeros(vt_ref.shape, BF16)
        k_ref[0:PAGE, :] = nk_ref[...].astype(BF16)
        vt_ref[0, :, 0:PAGE] = nv_ref[...].T.astype(BF16)


def kvprep_pages(cache, page_table, layer, new_rows, lt0=0):
    b, n_pages = page_table.shape
    pp = PAGES_PER_STEP
    n_steps = n_pages // pp + 1
    t = n_steps * pp * PAGE
    new = lambda lt: pl.BlockSpec((None, PAGE, LANE), lambda bi, i, pt: (bi, 0, lt0 + lt))
    gs = pltpu.PrefetchScalarGridSpec(
        num_scalar_prefetch=1, grid=(b, n_steps),
        in_specs=_page_specs(layer, 0, n_pages) + _page_specs(layer, 1, n_pages) + [new(0), new(1)],
        out_specs=[pl.BlockSpec((None, pp * PAGE, LANE), lambda bi, i, pt: (bi, i, 0)),
                   pl.BlockSpec((None, pp * PAGE // SEL_TK, LANE, SEL_TK), lambda bi, i, pt: (bi, i, 0, 0))])
    return pl.pallas_call(
        functools.partial(_kvprep_pages_kernel, n_steps), grid_spec=gs,
        out_shape=[jax.ShapeDtypeStruct((b, t, LANE), BF16),
                   jax.ShapeDtypeStruct((b, t // SEL_TK, LANE, SEL_TK), BF16)],
        compiler_params=_cparams(("arbitrary", "arbitrary")),
        name="kvprep_pages",
    )(page_table, *([cache] * (2 * pp)), new_rows, new_rows)


TQ = 128
NQL = NSA_HEADS * TQ
NSA_CMP_TILE, NSA_SEL_TILE, NSA_WIN_TILE, NSA_GL_TILE = 8, 10, 12, 14
NSA_W = (NSA_GL_TILE + 1) * LANE
WIN_TILES = WINDOW // WIN_TK + 1


def _rowmax(x):
    return jnp.max(x, axis=0, keepdims=True)


def _rowsum(x):
    return jnp.sum(x, axis=0, keepdims=True)


LOG2E = 1.4426950408889634
SEL_PER = SEL_TK // SEL_BLOCK


def _real(m):
    return jnp.where(m > 0.5 * NEG, m, 0.0)


def _nsa_attn_kernel(pos0, nc, ns, wt0, wpos0, n_win,
                     q_ref, gl_ref, ckh_ref, ckl_ref, cvt_ref, ks_ref, vst_ref, *rest):
    kw = rest[:WIN_TILES]
    vw = rest[WIN_TILES:2 * WIN_TILES]
    msel_ref, o_ref, sel_scr, work_scr, sel3_scr, imp_scr, oc_scr, acc_scr, ml_scr, ot_scr = rest[2 * WIN_TILES:]
    i = pl.program_id(1)
    t0 = pos0 + i * TQ
    nsp = sel_scr.shape[0]
    hd = NSA_HD

    qt = (q_ref[...] * (hd ** -0.5 * LOG2E)).T
    zero = jnp.zeros((hd, TQ), F32)
    pieces = []
    for jh in range(NSA_HEADS):
        blk = qt[jh * hd:(jh + 1) * hd, :]
        pieces.append(jnp.concatenate([blk, zero] if jh < NSA_HPG else [zero, blk], axis=0))
    qbd = jnp.concatenate(pieces, axis=1)
    qbd16 = qbd.astype(BF16)
    t_q = t0 + _iota((1, TQ), 1)
    gw = NSA_HPG * TQ

    def heads(x, n=NSA_HPG):
        return jnp.concatenate([x] * n, axis=1)

    c_idx = _iota((NCP, TQ), 0)
    cbias = jnp.where((CMP_STRIDE * c_idx + (CMP_BLOCK - 1) <= t_q) & (c_idx < nc), 0.0, NEG)
    for g in range(NSA_KVH):
        gs = slice(g * gw, (g + 1) * gw)
        qh, ql = _split16(qbd[:, gs])
        ckh = ckh_ref[...]
        s = _dot(ckh, qh) + _dot(ckh, ql) + _dot(ckl_ref[...], qh) + heads(cbias)
        p = jnp.exp2(s - _real(_rowmax(s)))
        l = _rowsum(p)
        inv = 1.0 / jnp.where(l > 0.0, l, 1.0)
        oc_scr[g] = _dot(cvt_ref[g * hd:(g + 1) * hd, :], p) * inv
        pc = p * inv
        imp = pc[:, :TQ]
        for h in range(1, NSA_HPG):
            imp = imp + pc[:, h * TQ:(h + 1) * TQ]
        imp_scr[:, g * TQ:(g + 1) * TQ] = imp
    pslc = _dot_hi(msel_ref[...], imp_scr[...])

    jrow = _iota((nsp, 2 * TQ), 0)
    t_gq = t0 + (_iota((1, 2 * TQ), 1) % TQ)
    cur = lax.shift_right_logical(t_gq, 6)
    forced = (jrow == 0) | (jrow == cur) | (jrow == cur - 1)
    score = jnp.where(jrow * SEL_BLOCK <= t_gq, pslc + jnp.where(forced, FORCE_BONUS, 0.0), NEG)
    work_scr[...] = jnp.where(jrow < ns, score, -3e38)
    sel_scr[...] = jnp.zeros(sel_scr.shape, F32)
    jrow_f = jrow.astype(F32)

    def pick(_, carry):
        w = work_scr[...]
        best = _rowmax(w)
        first = jnp.min(jnp.where(w == best, jrow_f, 1e9), axis=0, keepdims=True)
        hit = jrow_f == first
        sel_scr[...] = jnp.where(hit, 1.0, sel_scr[...])
        work_scr[...] = jnp.where(hit, -jnp.inf, w)
        return carry

    lax.fori_loop(0, N_SEL, pick, 0)
    for u in range(nsp // SEL_PER):
        sel3_scr[u] = jnp.where(sel_scr[u * SEL_PER:(u + 1) * SEL_PER, :] > 0.5, 0.0, NEG)

    last = (t0 + TQ + SEL_TK - 1) // SEL_TK - 1
    ml_scr[0:1, :] = jnp.full((1, NQL), NEG, F32)
    ml_scr[1:2, :] = jnp.zeros((1, NQL), F32)
    acc_scr[...] = jnp.zeros(acc_scr.shape, F32)

    def sel_tile(kt, extra):
        s = _dot(ks_ref[pl.ds(pl.multiple_of(kt * SEL_TK, SEL_TK), SEL_TK), :], qbd16)
        sb = sel3_scr[kt]
        s = jnp.concatenate(
            [s[jj * SEL_BLOCK:(jj + 1) * SEL_BLOCK, :]
             + jnp.concatenate([heads(sb[jj:jj + 1, :TQ]), heads(sb[jj:jj + 1, TQ:])], axis=1)
             for jj in range(SEL_PER)], axis=0)
        if extra is not None:
            s = s + extra
        m_old = ml_scr[0:1, :]
        m_new = jnp.maximum(m_old, _rowmax(s))
        alpha = jnp.exp2(m_old - m_new)
        p = jnp.exp2(s - m_new)
        ml_scr[0:1, :] = m_new
        ml_scr[1:2, :] = alpha * ml_scr[1:2, :] + _rowsum(p)
        p16 = p.astype(BF16)
        for g in range(NSA_KVH):
            gs = slice(g * gw, (g + 1) * gw)
            acc_scr[g] = acc_scr[g] * alpha[:, gs] + _dot(vst_ref[kt, g * hd:(g + 1) * hd, :], p16[:, gs])

    def sel_body(kt, carry):
        sel_tile(kt, None)
        return carry

    lax.fori_loop(0, last, sel_body, 0)
    krow = _iota((SEL_TK, TQ), 0)
    sel_tile(last, heads(jnp.where(last * SEL_TK + krow <= t_q, 0.0, NEG), NSA_HEADS))
    l_sel = ml_scr[1:2, :]
    inv_sel = 1.0 / jnp.where(l_sel > 0.0, l_sel, 1.0)

    wrow = _iota((WIN_TK, TQ), 0)
    m = jnp.full((1, NQL), NEG, F32)
    l = jnp.zeros((1, NQL), F32)
    acc_w = [jnp.zeros((hd, gw), F32) for _ in range(NSA_KVH)]
    for w in range(WIN_TILES):
        tile = i + (wt0 - (WIN_TILES - 1) + w)
        idx = tile * WIN_TK + wrow
        wpos = wpos0 + idx
        dlt = t_q - wpos
        ok = (dlt >= 0) & (dlt < WINDOW) & (wpos >= 0) & (idx < n_win) & (tile >= 0)
        s = _dot(kw[w][...], qbd16) + heads(jnp.where(ok, 0.0, NEG), NSA_HEADS)
        m_new = jnp.maximum(m, _rowmax(s))
        alpha = jnp.where(m > 0.5 * NEG, jnp.exp2(m - _real(m_new)), 0.0)
        p = jnp.exp2(s - _real(m_new))
        l = alpha * l + _rowsum(p)
        p16 = p.astype(BF16)
        for g in range(NSA_KVH):
            gs = slice(g * gw, (g + 1) * gw)
            acc_w[g] = acc_w[g] * alpha[:, gs] + _dot(vw[w][g * hd:(g + 1) * hd, :], p16[:, gs])
        m = m_new
    inv_win = 1.0 / jnp.where(l > 0.0, l, 1.0)

    gt = _sigmoid(gl_ref[...]).T
    for jh in range(NSA_HEADS):
        g, h = divmod(jh, NSA_HPG)
        hl = slice(h * TQ, (h + 1) * TQ)
        ls = slice(jh * TQ, (jh + 1) * TQ)
        ot_scr[jh * hd:(jh + 1) * hd, :] = (
            gt[3 * jh:3 * jh + 1, :] * oc_scr[g, :, hl]
            + gt[3 * jh + 1:3 * jh + 2, :] * (acc_scr[g, :, hl] * inv_sel[:, ls])
            + gt[3 * jh + 2:3 * jh + 3, :] * (acc_w[g][:, hl] * inv_win[:, ls]))
    o_ref[...] = ot_scr[...].T


def nsa_attention(cols, ckh, ckl, cvt, ks, vst, kwin, vwt, pos0, n_new, n_buf, n_win):
    b, lq, _ = cols.shape
    q = gl = cols
    tk_total = ks.shape[1]
    n_sel_tiles = tk_total // SEL_TK
    nsp = tk_total // SEL_BLOCK
    t_total = pos0 + n_new
    nseg = -(-t_total // CMP_STRIDE)
    nc = nseg - CMP_BLOCK // CMP_STRIDE + 1
    ns = -(-t_total // SEL_BLOCK)
    wt0 = n_buf // WIN_TK
    n_wtiles = kwin.shape[1] // WIN_TK
    j = jnp.arange(nsp)[:, None]
    c = jnp.arange(NCP)[None, :]
    msel = ((c >= 4 * j - 1) & (c <= 4 * j + 3)).astype(F32)

    def wspec(w, vt):
        def imap(bi, i):
            tile = jnp.clip(i + (wt0 - (WIN_TILES - 1) + w), 0, n_wtiles - 1)
            return (bi, tile, 0, 0) if vt else (bi, tile, 0)
        return pl.BlockSpec((None, None, LANE, WIN_TK) if vt else (None, WIN_TK, LANE), imap)

    kern = functools.partial(_nsa_attn_kernel, pos0, nc, ns, wt0, pos0 - n_buf, n_win)
    return pl.pallas_call(
        kern,
        grid=(b, lq // TQ),
        in_specs=[pl.BlockSpec((None, TQ, NSA_DIM), lambda bi, i: (bi, i, 0)),
                  pl.BlockSpec((None, TQ, LANE), lambda bi, i: (bi, i, NSA_GL_TILE)),
                  pl.BlockSpec((None, NCP, LANE), lambda bi, i: (bi, 0, 0)),
                  pl.BlockSpec((None, NCP, LANE), lambda bi, i: (bi, 0, 0)),
                  pl.BlockSpec((None, LANE, NCP), lambda bi, i: (bi, 0, 0)),
                  pl.BlockSpec((None, tk_total, LANE), lambda bi, i: (bi, 0, 0)),
                  pl.BlockSpec((None, n_sel_tiles, LANE, SEL_TK), lambda bi, i: (bi, 0, 0, 0))]
                 + [wspec(w, False) for w in range(WIN_TILES)]
                 + [wspec(w, True) for w in range(WIN_TILES)]
                 + [pl.BlockSpec((nsp, NCP), lambda bi, i: (0, 0))],
        out_specs=pl.BlockSpec((None, TQ, NSA_DIM), lambda bi, i: (bi, i, 0)),
        out_shape=jax.ShapeDtypeStruct((b, lq, NSA_DIM), F32),
        scratch_shapes=[pltpu.VMEM((nsp, 2 * TQ), F32), pltpu.VMEM((nsp, 2 * TQ), F32),
                        pltpu.VMEM((nsp // SEL_PER, SEL_PER, 2 * TQ), F32),
                        pltpu.VMEM((NCP, 2 * TQ), F32),
                        pltpu.VMEM((NSA_KVH, NSA_HD, NSA_HPG * TQ), F32),
                        pltpu.VMEM((NSA_KVH, NSA_HD, NSA_HPG * TQ), F32),
                        pltpu.VMEM((8, NQL), F32), pltpu.VMEM((NSA_DIM, TQ), F32)],
        compiler_params=_cparams(("arbitrary", "arbitrary")),
        name="nsa_attention",
    )(q, gl, ckh, ckl, cvt, ks, vst, *([kwin] * WIN_TILES), *([vwt] * WIN_TILES), msel)


TQS = 8


def _nsa_attn_small_kernel(pos0, nc, ns, wpos0, n_win, q_ref, gl_ref, ckh_ref, ckl_ref, cvt_ref, ks_ref,
                           vst_ref, kw_ref, vw_ref, msel_ref, eh_ref, eht_ref, gx_ref, o_ref,
                           sel_scr, work_scr, sel3_scr):
    hd = NSA_HD
    nsp = sel_scr.shape[0]
    lane8 = _iota((TQS, LANE), 1)
    lane = _iota((1, LANE), 1)
    t_lane = pos0 + (lane % TQS)
    grp1 = lane >= NSA_HPG * TQS

    q = q_ref[...] * (hd ** -0.5 * LOG2E)
    pieces = []
    for jh in range(NSA_HEADS):
        tile = q[:, (jh // 2) * LANE:(jh // 2 + 1) * LANE]
        dst_hi = jh >= NSA_HPG
        if (jh % 2 == 1) != dst_hi:
            tile = pltpu.roll(tile, hd, 1)
        pieces.append(jnp.where((lane8 >= hd) if dst_hi else (lane8 < hd), tile, 0.0))
    z = jnp.concatenate(pieces, axis=0).T
    z16 = z.astype(BF16)
    zh, zl = _split16(z)

    def own_group(full):
        return jnp.where(grp1, full[hd:, :], full[:hd, :])

    c_idx = _iota((NCP, LANE), 0)
    ckh = ckh_ref[...]
    s = _dot(ckh, zh) + _dot(ckh, zl) + _dot(ckl_ref[...], zh)
    s = s + jnp.where((CMP_STRIDE * c_idx + (CMP_BLOCK - 1) <= t_lane) & (c_idx < nc), 0.0, NEG)
    p = jnp.exp2(s - _real(_rowmax(s)))
    l = _rowsum(p)
    inv = 1.0 / jnp.where(l > 0.0, l, 1.0)
    o_cmp = own_group(_dot(cvt_ref[...], p) * inv)
    pslc = _dot_hi(msel_ref[...], _dot_hi(p * inv, eht_ref[...]))

    jrow = _iota((nsp, LANE), 0)
    cur = lax.shift_right_logical(t_lane, 6)
    forced = (jrow == 0) | (jrow == cur) | (jrow == cur - 1)
    score = jnp.where(jrow * SEL_BLOCK <= t_lane, pslc + jnp.where(forced, FORCE_BONUS, 0.0), NEG)
    work_scr[...] = jnp.where(jrow < ns, score, -3e38)
    sel_scr[...] = jnp.zeros(sel_scr.shape, F32)
    jrow_f = jrow.astype(F32)

    def pick(_, carry):
        w = work_scr[...]
        best = _rowmax(w)
        first = jnp.min(jnp.where(w == best, jrow_f, 1e9), axis=0, keepdims=True)
        hit = jrow_f == first
        sel_scr[...] = jnp.where(hit, 1.0, sel_scr[...])
        work_scr[...] = jnp.where(hit, -jnp.inf, w)
        return carry

    lax.fori_loop(0, N_SEL, pick, 0)
    on = _dot(sel_scr[...], eh_ref[...])
    for u in range(nsp // SEL_PER):
        sel3_scr[u] = jnp.where(on[u * SEL_PER:(u + 1) * SEL_PER, :] > 0.5, 0.0, NEG)

    last = (pos0 + TQS + SEL_TK - 1) // SEL_TK - 1
    krow = _iota((SEL_TK, LANE), 0)

    def sel_tile(kt, carry, extra):
        m, l, acc = carry
        s = _dot(ks_ref[pl.ds(pl.multiple_of(kt * SEL_TK, SEL_TK), SEL_TK), :], z16)
        sb = sel3_scr[kt]
        s = jnp.concatenate([s[jj * SEL_BLOCK:(jj + 1) * SEL_BLOCK, :] + sb[jj:jj + 1, :]
                             for jj in range(SEL_PER)], axis=0)
        if extra is not None:
            s = s + extra
        m_new = jnp.maximum(m, _rowmax(s))
        alpha = jnp.exp2(m - m_new)
        p = jnp.exp2(s - m_new)
        return m_new, alpha * l + _rowsum(p), acc * alpha + _dot(vst_ref[kt], p)

    init = (jnp.full((1, LANE), NEG, F32), jnp.zeros((1, LANE), F32), jnp.zeros((2 * hd, LANE), F32))
    carry = lax.fori_loop(0, last, lambda kt, c: sel_tile(kt, c, None), init)
    _, l, acc = sel_tile(last, carry, jnp.where(last * SEL_TK + krow <= t_lane, 0.0, NEG))
    o_sel = own_group(acc * (1.0 / jnp.where(l > 0.0, l, 1.0)))

    wrow = _iota((WIN_TK, LANE), 0)
    m = jnp.full((1, LANE), NEG, F32)
    l = jnp.zeros((1, LANE), F32)
    acc = jnp.zeros((2 * hd, LANE), F32)
    for w in range(kw_ref.shape[0] // WIN_TK):
        idx = w * WIN_TK + wrow
        wpos = wpos0 + idx
        dlt = t_lane - wpos
        ok = (dlt >= 0) & (dlt < WINDOW) & (wpos >= 0) & (idx < n_win)
        s = _dot(kw_ref[w * WIN_TK:(w + 1) * WIN_TK, :], z16) + jnp.where(ok, 0.0, NEG)
        m_new = jnp.maximum(m, _rowmax(s))
        alpha = jnp.where(m > 0.5 * NEG, jnp.exp2(m - _real(m_new)), 0.0)
        p = jnp.exp2(s - _real(m_new))
        l = alpha * l + _rowsum(p)
        acc = acc * alpha + _dot(vw_ref[w], p)
        m = m_new
    o_win = own_group(acc * (1.0 / jnp.where(l > 0.0, l, 1.0)))

    def to_rows(o):
        ot = jnp.concatenate([o, o], axis=0).T
        tiles = []
        for k in range(NSA_HEADS // 2):
            a = ot[(2 * k) * TQS:(2 * k + 1) * TQS, :]
            b = ot[(2 * k + 1) * TQS:(2 * k + 2) * TQS, :]
            tiles.append(jnp.where(lane8 < hd, a, b))
        return jnp.concatenate(tiles, axis=1)

    sg = _sigmoid(gl_ref[...])
    o_ref[...] = (_dot_hi(sg, gx_ref[0]) * to_rows(o_cmp) + _dot_hi(sg, gx_ref[1]) * to_rows(o_sel)
                  + _dot_hi(sg, gx_ref[2]) * to_rows(o_win))


def nsa_attention_small(cols, ckh, ckl, cvt, ks, vst, kwin, vwt, pos0, n_buf, n_win):
    b, lq, _ = cols.shape
    assert lq == TQS
    tk_total = ks.shape[1]
    nsp = tk_total // SEL_BLOCK
    t_total = pos0 + lq
    nc = -(-t_total // CMP_STRIDE) - CMP_BLOCK // CMP_STRIDE + 1
    ns = -(-t_total // SEL_BLOCK)
    j = jnp.arange(nsp)[:, None]
    c = jnp.arange(NCP)[None, :]
    msel = ((c >= 4 * j - 1) & (c <= 4 * j + 3)).astype(F32)
    n = jnp.arange(LANE)
    head, qi = n // TQS, n % TQS
    gq = (head // NSA_HPG) * TQS + qi
    eh = (jnp.arange(LANE)[:, None] == gq[None, :]).astype(F32)
    col = jnp.arange(NSA_DIM) // NSA_HD
    gx = jnp.stack([(jnp.arange(LANE)[:, None] == (3 * col + br)[None, :]).astype(F32) for br in range(3)])
    full = lambda a: pl.BlockSpec(a.shape, lambda bi: (0,) * a.ndim)
    per_b = lambda a: pl.BlockSpec((None,) + a.shape[1:], lambda bi: (bi,) + (0,) * (a.ndim - 1))
    kern = functools.partial(_nsa_attn_small_kernel, pos0, nc, ns, pos0 - n_buf, n_win)
    return pl.pallas_call(
        kern,
        grid=(b,),
        in_specs=[pl.BlockSpec((None, TQS, NSA_DIM), lambda bi: (bi, 0, 0)),
                  pl.BlockSpec((None, TQS, LANE), lambda bi: (bi, 0, NSA_GL_TILE)),
                  per_b(ckh), per_b(ckl), per_b(cvt), per_b(ks), per_b(vst), per_b(kwin), per_b(vwt),
                  full(msel), full(eh), full(eh), full(gx)],
        out_specs=pl.BlockSpec((None, TQS, NSA_DIM), lambda bi: (bi, 0, 0)),
        out_shape=jax.ShapeDtypeStruct((b, TQS, NSA_DIM), F32),
        scratch_shapes=[pltpu.VMEM((nsp, LANE), F32), pltpu.VMEM((nsp, LANE), F32),
                        pltpu.VMEM((nsp // SEL_PER, SEL_PER, LANE), F32)],
        compiler_params=_cparams(("arbitrary",)),
        name="nsa_attention_small",
    )(cols, cols, ckh, ckl, cvt, ks, vst, kwin, vwt, msel, eh, eh.T, gx)


def _layer_norm(v, g, b):
    mu = jnp.mean(v, axis=-1, keepdims=True)
    vc = v - mu
    var = jnp.mean(vc * vc, axis=-1, keepdims=True)
    return vc * lax.rsqrt(var + LN_EPS) * g + b


def _merge_kernel(x_ref, ya_ref, yb_ref, yc_ref, gate_ref, g1_ref, wa_ref, wb_ref, wc_ref, wo_ref,
                  lg_ref, lb_ref, o_ref):
    d = D_MODEL
    merged = (_sigmoid(gate_ref[:, :d]) * _dot(ya_ref[...], wa_ref[...])
              + _sigmoid(gate_ref[:, d:2 * d]) * _dot(yb_ref[...], wb_ref[...])
              + _sigmoid(gate_ref[:, 2 * d:]) * _dot(yc_ref[...], wc_ref[...]))
    o = _dot(merged, wo_ref[...])
    o_ref[...] = _layer_norm(ALPHA * x_ref[...] + g1_ref[...] * o, lg_ref[...], lb_ref[...])


def merge_out(x, ya, yb, yc, gate, g1, wa, wb, wc, wo, lg, lb, tm=256):
    m, d = x.shape
    tm = min(tm, m)
    per_row = g1.shape[0] != 1
    row = lambda w: pl.BlockSpec((tm, w), lambda i: (i, 0))
    mspec = row(d) if per_row else pl.BlockSpec((1, d), lambda i: (0, 0))
    wspec = pl.BlockSpec((d, d), lambda i: (0, 0))
    vspec = pl.BlockSpec((1, d), lambda i: (0, 0))
    return pl.pallas_call(
        _merge_kernel,
        grid=(m // tm,),
        in_specs=[row(d), row(d), row(d), row(d), row(3 * d), mspec, wspec, wspec, wspec, wspec, vspec, vspec],
        out_specs=row(d),
        out_shape=jax.ShapeDtypeStruct((m, d), F32),
        compiler_params=_cparams(("arbitrary",)),
        name="merge_out",
    )(x, ya, yb, yc, gate, g1, wa, wb, wc, wo, lg, lb)


def _lane_first(mask, lane_f):
    return jnp.min(jnp.where(mask, lane_f, 1e9), axis=-1, keepdims=True)


def _moe_kernel(x_ref, sc_ref, sh_ref, g2_ref, wr_ref, w1_ref, w3_ref, w2_ref, lg_ref, lb_ref,
                o_ref, u_scr, gate_scr, acc_scr):
    e = pl.program_id(1)
    tm = x_ref.shape[0]
    lane = _iota((tm, LANE), 1)
    lane_f = lane.astype(F32)

    @pl.when(e == 0)
    def _():
        u = x_ref[...] * (1.0 + sc_ref[...]) + sh_ref[...]
        u_scr[...] = u.astype(BF16)
        logits = _dot_hi(u, wr_ref[...])
        lg = jnp.where(lane < N_GROUPS, logits, -jnp.inf)
        gmax = jnp.max(lg, axis=-1, keepdims=True)
        gstar = _lane_first(lg == gmax, lane_f)
        pg = 1.0 / jnp.sum(jnp.exp(lg - gmax), axis=-1, keepdims=True)
        in_grp = (lane >= N_GROUPS) & (lane < N_GROUPS + N_EXPERTS) & (
            lax.shift_right_logical(lane - N_GROUPS, 2).astype(F32) == gstar)
        le = jnp.where(in_grp, logits, -jnp.inf)
        v1 = jnp.max(le, axis=-1, keepdims=True)
        i1 = _lane_first(le == v1, lane_f)
        le2 = jnp.where(lane_f == i1, -jnp.inf, le)
        v2 = jnp.max(le2, axis=-1, keepdims=True)
        i2 = _lane_first(le2 == v2, lane_f)
        e2 = jnp.exp(v2 - v1)
        den = 1.0 / (1.0 + e2)
        gate_scr[...] = jnp.where(lane_f == i1, den * pg, jnp.where(lane_f == i2, e2 * den * pg, 0.0))
        acc_scr[...] = jnp.zeros(acc_scr.shape, F32)

    u = u_scr[...]
    ge = jnp.sum(jnp.where(lane == e + N_GROUPS, gate_scr[...], 0.0), axis=-1, keepdims=True)
    h = _silu(_dot(u, w1_ref[...])) * _dot(u, w3_ref[...])
    acc_scr[...] += _dot(h * ge, w2_ref[...])

    @pl.when(e == N_EXPERTS - 1)
    def _():
        o_ref[...] = _layer_norm(ALPHA * x_ref[...] + g2_ref[...] * acc_scr[...], lg_ref[...], lb_ref[...])


def moe_out(x, sc, sh, g2, wr, w1, w3, w2, lg, lb, tm=512):
    m, d = x.shape
    tm = min(tm, m)
    per_row = sc.shape[0] != 1
    row = pl.BlockSpec((tm, d), lambda i, e: (i, 0))
    mspec = row if per_row else pl.BlockSpec((1, d), lambda i, e: (0, 0))
    vspec = pl.BlockSpec((1, d), lambda i, e: (0, 0))
    return pl.pallas_call(
        _moe_kernel,
        grid=(m // tm, N_EXPERTS),
        in_specs=[row, mspec, mspec, mspec, pl.BlockSpec((d, LANE), lambda i, e: (0, 0)),
                  pl.BlockSpec((None, d, EXPERT_HIDDEN), lambda i, e: (e, 0, 0)),
                  pl.BlockSpec((None, d, EXPERT_HIDDEN), lambda i, e: (e, 0, 0)),
                  pl.BlockSpec((None, EXPERT_HIDDEN, d), lambda i, e: (e, 0, 0)), vspec, vspec],
        out_specs=row,
        out_shape=jax.ShapeDtypeStruct((m, d), F32),
        scratch_shapes=[pltpu.VMEM((tm, d), BF16), pltpu.VMEM((tm, LANE), F32), pltpu.VMEM((tm, d), F32)],
        compiler_params=_cparams(("arbitrary", "arbitrary")),
        name="moe_out",
    )(x, sc, sh, g2, wr, w1, w3, w2, lg, lb)


def _pad_rows(a, n):
    return a if a.shape[1] == n else jnp.pad(a, ((0, 0), (0, n - a.shape[1]), (0, 0)))


def _layer_weights(l, p):
    w_in = p['w_in'][l]
    o1, o2, o3 = SSD_COLS, SSD_COLS + RWKV_COLS, SSD_COLS + RWKV_COLS + NSA_COLS
    padc = lambda w, n: jnp.pad(w, ((0, 0), (0, n - w.shape[1])))
    w_ssd = padc(w_in[:, :o1], SSD_W)
    w_nsa = padc(w_in[:, o2:o3], NSA_W)
    z64 = jnp.zeros((64, RWKV_DIM), F32)
    seg = _head_onehot(RWKV_HEADS, RWKV_HD, LANE).T
    rwkv_prm = (p['rwkv_mu'][l][None], p['rwkv_w0'][l][None], p['rwkv_a0'][l][None], p['rwkv_k_k'][l][None],
                p['rwkv_k_a'][l][None], p['rwkv_r_k'][l].reshape(1, RWKV_DIM),
                jnp.concatenate([p['rwkv_w2'][l], z64], 0), jnp.concatenate([z64, p['rwkv_a2'][l]], 0),
                p['rwkv_g2'][l], seg, seg.T)
    wr = jnp.pad(jnp.concatenate([p['w_group'][l], p['w_router'][l]], axis=1),
                 ((0, 0), (0, LANE - N_GROUPS - N_EXPERTS)))
    row = lambda v: v[None]
    return dict(
        w_ssd=w_ssd.astype(BF16), w_rwkv=w_in[:, o1:o2].astype(BF16), w_nsa=w_nsa.astype(BF16),
        w_gate=w_in[:, o3:].astype(BF16),
        ssd=(jnp.pad(p['ssd_conv_w'][l], ((0, 4), (0, 0))), row(p['ssd_conv_b'][l]),
             row(jnp.pad(p['ssd_dt_bias'][l], (0, LANE - SSD_HEADS))), row(jnp.pad(p['ssd_a_log'][l], (0, LANE - SSD_HEADS))),
             row(jnp.repeat(p['ssd_d'][l], SSD_HD)), row(p['ssd_norm_w'][l])),
        rwkv=rwkv_prm, lnx=(row(p['rwkv_lnx_w'][l]), row(p['rwkv_lnx_b'][l])),
        cmp=_cmp_weights(p['cmp_pe'][l], p['cmp_w1'][l], p['cmp_w2'][l]),
        wo=tuple(p[k][l].astype(BF16) for k in ('w_o_ssd', 'w_o_rwkv', 'w_o_nsa', 'w_out')),
        ln1=(row(p['ln1_g'][l]), row(p['ln1_b'][l])), ln2=(row(p['ln2_g'][l]), row(p['ln2_b'][l])),
        wr=wr, w1=p['moe_w1'][l].astype(BF16), w3=p['moe_w3'][l].astype(BF16), w2=p['moe_w2'][l].astype(BF16))


def _trunk_layer(x, mod, l, w, pos0, conv_st, ssm_st, shift_st, rwkv_st, cache_win, cache_cmp, cache_sel,
                 page_table):
    b, L, d = x.shape
    m = b * L
    x2 = x.reshape(m, d)
    sh1, sc1, g1, sh2, sc2, g2 = [mod[:, k * d:(k + 1) * d] for k in range(6)]
    if b > 1:
        sh1, sc1, g1, sh2, sc2, g2 = [jnp.repeat(t, L, axis=0) for t in (sh1, sc1, g1, sh2, sc2, g2)]
    proj = lambda wt: mod_proj(x2, sc1, sh1, wt).reshape(b, L, -1)
    c_ssd, c_rwkv, c_nsa, c_gate = proj(w['w_ssd']), proj(w['w_rwkv']), proj(w['w_nsa']), proj(w['w_gate'])

    lp = -(-L // SSD_Q) * SSD_Q
    cst8 = jnp.pad(conv_st, ((0, 0), (8 - (SSD_CONV - 1), 0), (0, 0)))
    h0t = jnp.transpose(ssm_st.reshape(b, SSD_INNER, SSD_STATE), (0, 2, 1))
    y_a, ht = ssd_mixer(_pad_rows(c_ssd, lp), cst8, h0t, L, *w['ssd'])
    y_a = y_a[:, :L]
    ssm_new = jnp.transpose(ht, (0, 2, 1)).reshape(b, SSD_HEADS, SSD_HD, SSD_STATE)
    conv_new = c_ssd[:, L - (SSD_CONV - 1):, SSD_INNER:SSD_INNER + SSD_CONV_DIM]

    lp = -(-L // RWKV_C) * RWKV_C
    sh8 = jnp.pad(shift_st[:, None, :], ((0, 0), (7, 0), (0, 0)))
    pre = rwkv_prep(_pad_rows(c_rwkv, lp), sh8, L, w['rwkv'])
    y_b, sp = rwkv_scan(*pre, _pair_blockdiag(rwkv_st), *w['lnx'])
    y_b = y_b[:, :L]
    rwkv_new = _pair_unblock(sp)
    shift_new = c_rwkv[:, -1]

    pek, pev, wk, wv, w2k, w2v = w['cmp']
    kvrow = (2, NSA_KVH, NSA_HD)
    rows = lambda tile: c_nsa[:, :, tile * LANE:(tile + 2) * LANE]
    if cache_cmp is None:
        hp = cmp_h_rows(c_nsa, wk, wv, NSA_CMP_TILE)
        hp = _pad_rows(hp, NCP)
        hn = jnp.zeros((b, 8, CMP_HW), F32)
        ks, vst = kvprep_rows(c_nsa, SEL_TK, NSA_SEL_TILE)
        kwin, vwt = kvprep_rows(c_nsa, WIN_TK, NSA_WIN_TILE)
        cq, n_buf, n_win = c_nsa, 0, L
        win_new = rows(NSA_WIN_TILE)[:, max(L - WINDOW, 0):]
    else:
        cq = _pad_rows(c_nsa, TQ)
        hp = _pad_rows(cmp_h_pages(cache_cmp, page_table, l, wk, wv), NCP)
        hn = cmp_h_rows(cq, wk, wv, NSA_CMP_TILE)
        ks, vst = kvprep_pages(cache_sel, page_table, l, cq, NSA_SEL_TILE)
        n_buf = cache_win.shape[1]
        win_all = jnp.concatenate([cache_win.reshape(b, n_buf, 2 * LANE), rows(NSA_WIN_TILE)], axis=1)
        n_win = n_buf + L
        kwin, vwt = kvprep_rows(_pad_rows(win_all, -(-(n_buf + TQ) // WIN_TK) * WIN_TK), WIN_TK)
        win_new = win_all[:, n_win - min(WINDOW, n_win):]
    hn_row = NCP - 1 if cache_cmp is None else pos0 // CMP_STRIDE - 1
    ckh, ckl, cvt = cmp_finish(hp, hn, hn_row, pek, pev, wk, wv, w2k, w2v)
    if L == TQS:
        y_c = nsa_attention_small(c_nsa, ckh, ckl, cvt, ks, vst, kwin, vwt, pos0, n_buf, n_win)
    else:
        y_c = nsa_attention(cq, ckh, ckl, cvt, ks, vst, kwin, vwt, pos0, L, n_buf, n_win)[:, :L]
    cmp_rows = rows(NSA_CMP_TILE).reshape((b, L) + kvrow)
    sel_rows = rows(NSA_SEL_TILE).reshape((b, L) + kvrow)
    win_new = win_new.reshape(win_new.shape[:2] + kvrow)

    flat = lambda t: t.reshape(m, -1)
    x1 = merge_out(x2, flat(y_a), flat(y_b), flat(y_c), flat(c_gate), g1, *w['wo'], *w['ln1'])
    x_out = moe_out(x1, sc2, sh2, g2, w['wr'], w['w1'], w['w3'], w['w2'], *w['ln2'])
    return x_out.reshape(b, L, d), (cmp_rows, sel_rows, win_new, ssm_new, conv_new, rwkv_new, shift_new)


def kernel(x_prompt, x_sample, c_prompt, c_sample, cache_cmp, cache_sel, cache_win, state_ssm, state_ssm_conv,
           state_rwkv, state_rwkv_shift, page_table, w_ada, b_ada, w_in, ssd_conv_w, ssd_conv_b, ssd_dt_bias,
           ssd_a_log, ssd_d, ssd_norm_w, rwkv_mu, rwkv_w0, rwkv_w2, rwkv_a0, rwkv_a2, rwkv_g2, rwkv_k_k, rwkv_k_a,
           rwkv_r_k, rwkv_lnx_w, rwkv_lnx_b, cmp_pe, cmp_w1, cmp_w2, w_o_ssd, w_o_rwkv, w_o_nsa, w_out, ln1_g,
           ln1_b, ln2_g, ln2_b, w_group, w_router, moe_w1, moe_w3, moe_w2):
    p = dict(w_in=w_in, ssd_conv_w=ssd_conv_w, ssd_conv_b=ssd_conv_b, ssd_dt_bias=ssd_dt_bias, ssd_a_log=ssd_a_log,
             ssd_d=ssd_d, ssd_norm_w=ssd_norm_w, rwkv_mu=rwkv_mu, rwkv_w0=rwkv_w0, rwkv_w2=rwkv_w2, rwkv_a0=rwkv_a0,
             rwkv_a2=rwkv_a2, rwkv_g2=rwkv_g2, rwkv_k_k=rwkv_k_k, rwkv_k_a=rwkv_k_a, rwkv_r_k=rwkv_r_k,
             rwkv_lnx_w=rwkv_lnx_w, rwkv_lnx_b=rwkv_lnx_b, cmp_pe=cmp_pe, cmp_w1=cmp_w1, cmp_w2=cmp_w2,
             w_o_ssd=w_o_ssd, w_o_rwkv=w_o_rwkv, w_o_nsa=w_o_nsa, w_out=w_out, ln1_g=ln1_g, ln1_b=ln1_b,
             ln2_g=ln2_g, ln2_b=ln2_b, w_group=w_group, w_router=w_router, moe_w1=moe_w1, moe_w3=moe_w3,
             moe_w2=moe_w2)
    bp, bs = x_prompt.shape[0], x_sample.shape[0]
    past_len = page_table.shape[1] * PAGE
    nb = -(-(bp + bs) // SUBLANE) * SUBLANE
    c_all = jnp.pad(jnp.concatenate([c_prompt, c_sample], axis=0), ((0, nb - bp - bs), (0, 0)))
    mod = ada_mod(c_all, w_ada, b_ada)
    n_phys = cache_cmp.shape[1]
    cmp_pages = cache_cmp.reshape(DEPTH, n_phys, PAGE, 2 * LANE)
    sel_pages = cache_sel.reshape(DEPTH, n_phys, PAGE, 2 * LANE)
    zeros = lambda *s: jnp.zeros(s, F32)
    xp, xs = x_prompt, x_sample
    st_p, st_s = [], []
    for l in range(DEPTH):
        w = _layer_weights(l, p)
        xp, sp_l = _trunk_layer(xp, mod[l, :bp], l, w, 0, zeros(bp, SSD_CONV - 1, SSD_CONV_DIM),
                                zeros(bp, SSD_HEADS, SSD_HD, SSD_STATE), zeros(bp, RWKV_COLS),
                                zeros(bp, RWKV_HEADS, RWKV_HD, RWKV_HD), None, None, None, None)
        xs, ss_l = _trunk_layer(xs, mod[l, bp:bp + bs], l, w, past_len, state_ssm_conv[l], state_ssm[l],
                                state_rwkv_shift[l], state_rwkv[l], cache_win[l], cmp_pages, sel_pages, page_table)
        st_p.append(sp_l)
        st_s.append(ss_l)
    sp = [jnp.stack(z) for z in zip(*st_p)]
    ss = [jnp.stack(z) for z in zip(*st_s)]
    return (xp, xs, sp[0], sp[1], sp[2], sp[3], sp[4], sp[5], sp[6], ss[0], ss[1], ss[2], ss[3], ss[4], ss[5], ss[6])
```

```python
import functools
import math

import jax
import jax.numpy as jnp
from jax import lax
from jax.experimental import pallas as pl
from jax.experimental.pallas import tpu as pltpu

F32 = jnp.float32
BF16 = jnp.bfloat16
HIGHEST = lax.Precision.HIGHEST

D_MODEL = 1024
DEPTH = 2
PAGE = 128
SSD_HEADS, SSD_HD, SSD_GROUPS, SSD_STATE, SSD_CONV = 16, 64, 2, 128, 4
SSD_INNER = SSD_HEADS * SSD_HD
SSD_CONV_DIM = SSD_INNER + 2 * SSD_GROUPS * SSD_STATE
SSD_COLS = SSD_INNER + SSD_CONV_DIM + SSD_HEADS
RWKV_HEADS, RWKV_HD = 16, 64
RWKV_DIM = RWKV_HEADS * RWKV_HD
RWKV_COLS = 3 * RWKV_DIM + 64 + 64 + 128
RWKV_LNX_EPS = 64e-5
NSA_HEADS, NSA_KVH, NSA_HPG, NSA_HD = 16, 2, 8, 64
NSA_DIM = NSA_HEADS * NSA_HD
NSA_COLS = NSA_DIM + 3 * 2 * NSA_KVH * NSA_HD + 3 * NSA_HEADS
CMP_BLOCK, CMP_STRIDE, CMP_HIDDEN = 32, 16, 128
SEL_BLOCK, N_SEL, WINDOW = 64, 16, 512
FORCE_BONUS = 1e4
N_GROUPS, EPG, N_EXPERTS, EXPERT_HIDDEN = 4, 4, 16, 256
ALPHA = (2 * DEPTH) ** 0.25
LN_EPS = 1e-5
RMS_EPS = 1e-5
NEG = -1e30

LANE = 128
SUBLANE = 8
VMEM_LIMIT = 56 * 1024 * 1024


def _cparams(sem):
    return pltpu.CompilerParams(dimension_semantics=sem, vmem_limit_bytes=VMEM_LIMIT)


def _dot(a, b):
    return jnp.dot(a.astype(BF16), b.astype(BF16), preferred_element_type=F32)


def _dot_hi(a, b):
    return jnp.dot(a, b, precision=HIGHEST, preferred_element_type=F32)


def _dot_nt(a, b):
    return lax.dot_general(a.astype(BF16), b.astype(BF16), (((1,), (1,)), ((), ())),
                           preferred_element_type=F32)


def _dot_nt_hi(a, b):
    return lax.dot_general(a, b, (((1,), (1,)), ((), ())), precision=HIGHEST,
                           preferred_element_type=F32)


def _split16(a):
    hi = a.astype(BF16)
    return hi, (a - hi.astype(F32)).astype(BF16)


def _split24(a):
    h1 = a.astype(BF16)
    r1 = a - h1.astype(F32)
    h2 = r1.astype(BF16)
    return h1, h2, (r1 - h2.astype(F32)).astype(BF16)


def _dot_w01(x, w16):
    return sum(_dot(piece, w16) for piece in _split24(x))


def _dot3(a, b):
    (ah, al), (bh, bl) = a, b
    return _dot(ah, bh) + _dot(ah, bl) + _dot(al, bh)


def _dot3_nt(a, b):
    (ah, al), (bh, bl) = a, b
    return _dot_nt(ah, bh) + _dot_nt(ah, bl) + _dot_nt(al, bh)


def _sigmoid(x):
    return 1.0 / (1.0 + jnp.exp(-x))


def _silu(x):
    return x * _sigmoid(x)


def _softplus(x):
    return jnp.maximum(x, 0.0) + jnp.log(1.0 + jnp.exp(-jnp.abs(x)))


def _iota(shape, dim):
    return lax.broadcasted_iota(jnp.int32, shape, dim)


def _head_onehot(n_heads, hd, pad_rows):
    r = jnp.arange(pad_rows)[:, None]
    c = jnp.arange(n_heads * hd)[None, :] // hd
    return (r == c).astype(F32)


def _ada_kernel(c_ref, w_ref, b_ref, o_ref):
    o_ref[...] = _dot_hi(_silu(c_ref[...]), w_ref[...]) + b_ref[...]


def ada_mod(c_all, w_ada, b_ada):
    nb = c_all.shape[0]
    return pl.pallas_call(
        _ada_kernel,
        grid=(DEPTH, 6),
        in_specs=[pl.BlockSpec((nb, D_MODEL), lambda l, j: (0, 0)),
                  pl.BlockSpec((None, D_MODEL, D_MODEL), lambda l, j: (l, 0, j)),
                  pl.BlockSpec((None, 1, D_MODEL), lambda l, j: (l, 0, j))],
        out_specs=pl.BlockSpec((None, nb, D_MODEL), lambda l, j: (l, 0, j)),
        out_shape=jax.ShapeDtypeStruct((DEPTH, nb, 6 * D_MODEL), F32),
        compiler_params=_cparams(("arbitrary", "arbitrary")),
        name="ada_mod",
    )(c_all, w_ada, b_ada.reshape(DEPTH, 1, 6 * D_MODEL))


def _inproj_kernel(x_ref, sc_ref, sh_ref, w_ref, o_ref):
    u = x_ref[...] * (1.0 + sc_ref[...]) + sh_ref[...]
    o_ref[...] = _dot(u, w_ref[...])


def mod_proj(x, sc, sh, w, tm=256):
    m, k = x.shape
    n = w.shape[1]
    tm = min(tm, m)
    per_row = sc.shape[0] != 1
    mspec = (pl.BlockSpec((tm, k), lambda i: (i, 0)) if per_row
             else pl.BlockSpec((1, k), lambda i: (0, 0)))
    return pl.pallas_call(
        _inproj_kernel,
        grid=(m // tm,),
        in_specs=[pl.BlockSpec((tm, k), lambda i: (i, 0)), mspec, mspec,
                  pl.BlockSpec((k, n), lambda i: (0, 0))],
        out_specs=pl.BlockSpec((tm, n), lambda i: (i, 0)),
        out_shape=jax.ShapeDtypeStruct((m, n), F32),
        compiler_params=_cparams(("arbitrary",)),
        name="mod_proj",
    )(x, sc, sh, w)


SSD_Q = 128
SSD_W = SSD_INNER + SSD_CONV_DIM + LANE


def _ssd_kernel(n_valid, zxd_ref, cst_ref, h0_ref, cw_ref, cb_ref, dtb_ref, alog_ref, dexp_ref,
                nw_ref, e16_ref, tri_ref, y_ref, hout_ref, ext, hT):
    j = pl.program_id(1)
    q = SSD_Q

    @pl.when(j == 0)
    def _():
        ext[0:8, :] = cst_ref[...]
        hT[...] = h0_ref[...]

    ext[8:8 + q, :] = zxd_ref[:, SSD_INNER:SSD_INNER + SSD_CONV_DIM]
    conv = (cb_ref[...] + ext[5:5 + q, :] * cw_ref[0:1, :] + ext[6:6 + q, :] * cw_ref[1:2, :]
            + ext[7:7 + q, :] * cw_ref[2:3, :] + ext[8:8 + q, :] * cw_ref[3:4, :])
    ext[0:8, :] = ext[q:q + 8, :]
    xbc = _silu(conv)
    xs = xbc[:, :SSD_INNER]
    bm = xbc[:, SSD_INNER:SSD_INNER + 2 * SSD_STATE]
    cm = xbc[:, SSD_INNER + 2 * SSD_STATE:]

    row = _iota((q, LANE), 0)
    lane = _iota((q, LANE), 1)
    dt = _softplus(zxd_ref[:, SSD_INNER + SSD_CONV_DIM:] + dtb_ref[...])
    dt = jnp.where((lane < SSD_HEADS) & (row + j * q < n_valid), dt, 0.0)
    a = -jnp.exp(alog_ref[...])
    acum = _dot_hi(tri_ref[...], dt * a)
    a_last = acum[q - 1:q, :]
    e16 = e16_ref[...]
    dt_e = _dot_hi(dt, e16)
    ea_e = _dot_hi(jnp.exp(acum), e16)
    dte_e = _dot_hi(jnp.exp(a_last - acum), e16)
    elast_e = ea_e[q - 1:q, :]
    xdt = xs * dt_e
    acum_t = acum.T
    tril = row >= lane
    lo_half = lane < SSD_HD

    y_tiles = []
    for g in range(SSD_GROUPS):
        cc = cm[:, g * SSD_STATE:(g + 1) * SSD_STATE]
        bc = bm[:, g * SSD_STATE:(g + 1) * SSD_STATE]
        cb = _dot_nt(cc, bc)
        gs = slice(g * 512, (g + 1) * 512)
        h_in = hT[:, gs]
        y_off = _dot(cc, h_in) * ea_e[:, gs]
        hT[:, gs] = h_in * elast_e[:, gs] + _dot(bc.T, xdt[:, gs] * dte_e[:, gs])
        for tl in range(4):
            t = 4 * g + tl
            xt = xdt[:, t * LANE:(t + 1) * LANE]
            yd = jnp.zeros((q, LANE), F32)
            for sub in range(2):
                h = 2 * t + sub
                diff = acum[:, h:h + 1] - acum_t[h:h + 1, :]
                m = cb * jnp.exp(jnp.where(tril, diff, NEG))
                yd = yd + _dot(m, jnp.where(lo_half if sub == 0 else ~lo_half, xt, 0.0))
            y_tiles.append(yd + y_off[:, tl * LANE:(tl + 1) * LANE])
    y = jnp.concatenate(y_tiles, axis=1) + dexp_ref[...] * xs
    y = y * _silu(zxd_ref[:, :SSD_INNER])
    outs = []
    for g in range(SSD_GROUPS):
        yg = y[:, g * 512:(g + 1) * 512]
        ms = jnp.sum(yg * yg, axis=-1, keepdims=True) * (1.0 / 512.0)
        outs.append(yg * lax.rsqrt(ms + RMS_EPS))
    y_ref[...] = jnp.concatenate(outs, axis=1) * nw_ref[...]

    @pl.when(j == pl.num_programs(1) - 1)
    def _():
        hout_ref[...] = hT[...]


def ssd_mixer(zxd, conv_st8, h0t, n_valid, cw8, cb, dtb, alog, dexp, nw):
    b, lp, _ = zxd.shape
    nj = lp // SSD_Q
    e16 = _head_onehot(SSD_HEADS, SSD_HD, LANE)
    tri = (jnp.arange(SSD_Q)[:, None] >= jnp.arange(SSD_Q)[None, :]).astype(F32)
    full = lambda shp: pl.BlockSpec(shp, lambda bi, j: (0,) * len(shp))
    return pl.pallas_call(
        functools.partial(_ssd_kernel, n_valid),
        grid=(b, nj),
        in_specs=[pl.BlockSpec((None, SSD_Q, SSD_W), lambda bi, j: (bi, j, 0)),
                  pl.BlockSpec((None, 8, SSD_CONV_DIM), lambda bi, j: (bi, 0, 0)),
                  pl.BlockSpec((None, SSD_STATE, SSD_INNER), lambda bi, j: (bi, 0, 0)),
                  full((8, SSD_CONV_DIM)), full((1, SSD_CONV_DIM)), full((1, LANE)), full((1, LANE)),
                  full((1, SSD_INNER)), full((1, SSD_INNER)), full((LANE, SSD_INNER)),
                  full((SSD_Q, SSD_Q))],
        out_specs=[pl.BlockSpec((None, SSD_Q, SSD_INNER), lambda bi, j: (bi, j, 0)),
                   pl.BlockSpec((None, SSD_STATE, SSD_INNER), lambda bi, j: (bi, 0, 0))],
        out_shape=[jax.ShapeDtypeStruct((b, lp, SSD_INNER), F32),
                   jax.ShapeDtypeStruct((b, SSD_STATE, SSD_INNER), F32)],
        scratch_shapes=[pltpu.VMEM((SSD_Q + 8, SSD_CONV_DIM), F32),
                        pltpu.VMEM((SSD_STATE, SSD_INNER), F32)],
        compiler_params=_cparams(("arbitrary", "arbitrary")),
        name="ssd_mixer",
    )(zxd, conv_st8, h0t, cw8, cb, dtb, alog, dexp, nw, e16, tri)


def _rwkv_prep_kernel(n_valid, tr, x_ref, sh8_ref, mu_ref, w0_ref, a0_ref, kk_ref, ka_ref, rk_ref,
                      w2_ref, a2_ref, g2_ref, seg_ref, e16_ref,
                      r_o, ld_o, k_o, v_o, kk_o, bb_o, g_o, bv_o, ext):
    j = pl.program_id(1)

    @pl.when(j == 0)
    def _():
        ext[0:8, :] = sh8_ref[...]

    x = x_ref[...]
    ext[8:8 + tr, :] = x
    prev = ext[7:7 + tr, :]
    ext[0:8, :] = ext[tr:tr + 8, :]
    xm = x + (prev - x) * mu_ref[...]
    d = RWKV_DIM
    r, k, v = xm[:, :d], xm[:, d:2 * d], xm[:, 2 * d:3 * d]
    lo = xm[:, 3 * d:3 * d + LANE]
    glo = xm[:, 3 * d + LANE:]
    w = w0_ref[...] + _dot3(_split16(jnp.tanh(lo)), _split16(w2_ref[...]))
    ld = -jnp.exp(-_softplus(-w) - 0.5)
    a = _sigmoid(a0_ref[...] + _dot3(_split16(lo), _split16(a2_ref[...])))
    g = _dot(_sigmoid(glo), g2_ref[...])
    seg, e16 = seg_ref[...].astype(BF16), e16_ref[...].astype(BF16)
    kk = k * kk_ref[...]
    ss = _dot_w01(kk * kk, seg)
    kk = kk * _dot_w01(lax.rsqrt(jnp.maximum(ss, 1e-24)), e16)
    k2 = k * (1.0 + (a - 1.0) * ka_ref[...])
    bonus = _dot_w01(_dot_w01(r * k2 * rk_ref[...], seg), e16)
    valid = (_iota((tr, 1), 0) + j * tr) < n_valid
    zero = lambda t: jnp.where(valid, t, 0.0)
    r_o[...] = r
    ld_o[...] = zero(ld)
    k_o[...] = zero(k2)
    v_o[...] = zero(v)
    kk_o[...] = zero(kk)
    bb_o[...] = zero(kk * a)
    g_o[...] = g
    bv_o[...] = bonus * v


def rwkv_prep(cols, shift8, n_valid, prm):
    b, lp, _ = cols.shape
    tr = min(128, lp)
    d = RWKV_DIM
    full = lambda shp: pl.BlockSpec(shp, lambda bi, j: (0,) * len(shp))
    row = lambda: pl.BlockSpec((None, tr, d), lambda bi, j: (bi, j, 0))
    return pl.pallas_call(
        functools.partial(_rwkv_prep_kernel, n_valid, tr),
        grid=(b, lp // tr),
        in_specs=[pl.BlockSpec((None, tr, RWKV_COLS), lambda bi, j: (bi, j, 0)),
                  pl.BlockSpec((None, 8, RWKV_COLS), lambda bi, j: (bi, 0, 0)),
                  full((1, RWKV_COLS))] + [full((1, d))] * 5 + [full((LANE, d))] * 3
                 + [full((d, LANE)), full((LANE, d))],
        out_specs=[row() for _ in range(8)],
        out_shape=[jax.ShapeDtypeStruct((b, lp, d), F32)] * 8,
        scratch_shapes=[pltpu.VMEM((tr + 8, RWKV_COLS), F32)],
        compiler_params=_cparams(("arbitrary", "arbitrary")),
        name="rwkv_prep",
    )(cols, shift8, *prm)


RWKV_C = 64


RWKV_NPP = 2


def _rwkv_scan_kernel(nch, r_ref, ld_ref, k_ref, v_ref, kk_ref, bb_ref, g_ref, bv_ref, s0_ref,
                      lnw_ref, lnb_ref, y_ref, sout_ref, st):
    i = pl.program_id(2)
    c = RWKV_C

    @pl.when(i == 0)
    def _():
        st[...] = s0_ref[...]

    n2 = 2 * c
    rr = _iota((n2, n2), 0)
    cc = _iota((n2, n2), 1)
    eye = (rr == cc).astype(F32)
    upper = rr < cc
    upper_eq = rr <= cc
    tri = (_iota((c, c), 0) >= _iota((c, c), 1)).astype(BF16)
    lane = _iota((c, LANE), 1)
    m0 = lane < RWKV_HD
    hmean = jnp.where((rr < RWKV_HD) == (cc < RWKV_HD), 1.0 / RWKV_HD, 0.0).astype(BF16)

    def stack(t):
        return jnp.concatenate([jnp.where(m0, t, 0.0), jnp.where(m0, 0.0, t)], axis=0)

    def head_mean(t):
        return sum(_dot(piece, hmean) for piece in _split24(t))

    units = [(ci, pi) for ci in range(nch) for pi in range(RWKV_NPP)]
    pre = []
    for ci, pi in units:
        sl = pl.ds(ci * c, c)
        ls = slice(pi * LANE, (pi + 1) * LANE)
        r, ld, k, v, kk, bb = (ref[sl, ls] for ref in (r_ref, ld_ref, k_ref, v_ref, kk_ref, bb_ref))
        cum = sum(_dot(tri, piece) for piece in _split24(ld))
        p_in = jnp.exp(cum)
        p_inv = jnp.exp(-cum)
        p_c = p_in[c - 1:c, :]
        kks = stack(kk * jnp.exp(cum - ld))
        rs = stack(r * p_in)
        bs = stack(bb * p_inv)
        ks = stack(k * p_inv)
        kks16, rs16, bs16, ks16 = (t.astype(BF16) for t in (kks, rs, bs, ks))
        pre.append(dict(
            p_c=p_c, vs_t=_split16(stack(v).T), kks_t=kks.T.astype(BF16), rs_t=rs.T.astype(BF16),
            upd_r=_split16(jnp.concatenate([bs * p_c, ks * p_c], axis=0)),
            nt=jnp.where(upper, _dot_nt(bs16, kks16), 0.0),
            avk=jnp.where(upper, _dot_nt(ks16, kks16), 0.0).astype(BF16),
            arb=jnp.where(upper_eq, _dot_nt(bs16, rs16), 0.0).astype(BF16),
            ark=jnp.where(upper_eq, _dot_nt(ks16, rs16), 0.0).astype(BF16)))
    tinv = [eye - u['nt'] for u in pre]
    pw = [u['nt'].astype(BF16) for u in pre]
    for _ in range(5):
        pw = [_dot(w, w).astype(BF16) for w in pw]
        tinv = [t + _dot(t, w) for t, w in zip(tinv, pw)]
    tinv = [_split16(t) for t in tinv]

    for ci in range(nch):
        sl = pl.ds(ci * c, c)
        for pi in range(RWKV_NPP):
            ls = slice(pi * LANE, (pi + 1) * LANE)
            u = pre[ci * RWKV_NPP + pi]
            s = st[pi]
            s16 = s.astype(BF16)
            vs_t16 = u['vs_t'][0]
            ut = _dot3(_split16(_dot(s16, u['kks_t']) + _dot(vs_t16, u['avk'])), tinv[ci * RWKV_NPP + pi])
            ut_p = _split16(ut)
            yt = _dot(s16, u['rs_t']) - _dot(ut_p[0], u['arb']) + _dot(vs_t16, u['ark'])
            upd_l = tuple(jnp.concatenate([-a, b], axis=1) for a, b in zip(ut_p, u['vs_t']))
            st[pi] = s * u['p_c'] + _dot3(upd_l, u['upd_r'])
            ys = yt.T
            y = ys[:c, :] + ys[c:, :]
            yc = y - head_mean(y)
            var = head_mean(yc * yc)
            y = yc * lax.rsqrt(var + RWKV_LNX_EPS) * lnw_ref[:, ls] + lnb_ref[:, ls] + bv_ref[sl, ls]
            y_ref[sl, ls] = y * g_ref[sl, ls]

    @pl.when(i == pl.num_programs(2) - 1)
    def _():
        sout_ref[...] = st[...]


def rwkv_scan(r, ld, k, v, kk, bb, g, bv, s0p, lnw, lnb):
    b, lp, d = r.shape
    nch = min(4, lp // RWKV_C)
    rb = nch * RWKV_C
    npp = RWKV_NPP
    npair = d // LANE
    row = lambda: pl.BlockSpec((None, rb, npp * LANE), lambda bi, p, i: (bi, i, p))
    st = lambda: pl.BlockSpec((None, npp, LANE, LANE), lambda bi, p, i: (bi, p, 0, 0))
    vec = lambda: pl.BlockSpec((1, npp * LANE), lambda bi, p, i: (0, p))
    return pl.pallas_call(
        functools.partial(_rwkv_scan_kernel, nch),
        grid=(b, npair // npp, lp // rb),
        in_specs=[row() for _ in range(8)] + [st(), vec(), vec()],
        out_specs=[row(), st()],
        out_shape=[jax.ShapeDtypeStruct((b, lp, d), F32),
                   jax.ShapeDtypeStruct((b, npair, LANE, LANE), F32)],
        scratch_shapes=[pltpu.VMEM((npp, LANE, LANE), F32)],
        compiler_params=_cparams(("arbitrary", "arbitrary", "arbitrary")),
        name="rwkv_scan",
    )(r, ld, k, v, kk, bb, g, bv, s0p, lnw, lnb)


def _pair_blockdiag(s):
    b = s.shape[0]
    s = s.reshape(b, 8, 2, RWKV_HD, RWKV_HD)
    z = jnp.zeros_like(s[:, :, 0])
    top = jnp.concatenate([s[:, :, 0], z], axis=-1)
    bot = jnp.concatenate([z, s[:, :, 1]], axis=-1)
    return jnp.concatenate([top, bot], axis=-2)


def _pair_unblock(sp):
    b = sp.shape[0]
    return jnp.stack([sp[:, :, :RWKV_HD, :RWKV_HD], sp[:, :, RWKV_HD:, RWKV_HD:]], axis=2).reshape(
        b, RWKV_HEADS, RWKV_HD, RWKV_HD)


CMP_HW = 4 * NSA_KVH * CMP_HIDDEN
NSEG_PAGE = PAGE // CMP_STRIDE


def _cmp_h_rows_kernel(nsr, xk_ref, xv_ref, wk_ref, wv_ref, o_ref):
    half = CMP_HW // 2
    acc_k = jnp.zeros((nsr, half), F32)
    acc_v = jnp.zeros((nsr, half), F32)
    for s in range(CMP_STRIDE):
        rows = pl.ds(s, nsr, stride=CMP_STRIDE)
        acc_k = acc_k + _dot(xk_ref[rows, :], wk_ref[s])
        acc_v = acc_v + _dot(xv_ref[rows, :], wv_ref[s])
    o_ref[...] = jnp.concatenate([acc_k, acc_v], axis=1)


def cmp_h_rows(rows, wk, wv, lt=0):
    b, t, _ = rows.shape
    tb = 2048 if t % 2048 == 0 else t
    nsr = tb // CMP_STRIDE
    wspec = pl.BlockSpec((CMP_STRIDE, LANE, CMP_HW // 2), lambda bi, i: (0, 0, 0))
    return pl.pallas_call(
        functools.partial(_cmp_h_rows_kernel, nsr),
        grid=(b, t // tb),
        in_specs=[pl.BlockSpec((None, tb, LANE), lambda bi, i: (bi, i, lt)),
                  pl.BlockSpec((None, tb, LANE), lambda bi, i: (bi, i, lt + 1)), wspec, wspec],
        out_specs=pl.BlockSpec((None, nsr, CMP_HW), lambda bi, i: (bi, i, 0)),
        out_shape=jax.ShapeDtypeStruct((b, t // CMP_STRIDE, CMP_HW), F32),
        compiler_params=_cparams(("arbitrary", "arbitrary")),
        name="cmp_h_rows",
    )(rows, rows, wk, wv)


PAGES_PER_STEP = 16
CMP_PAGES_PER_STEP = 32


def _cmp_h_pages_kernel(pt_ref, *refs):
    pp = CMP_PAGES_PER_STEP
    pk, pv = refs[:pp], refs[pp:2 * pp]
    wk_ref, wv_ref, o_ref = refs[2 * pp:]
    half = CMP_HW // 2
    acc_k = jnp.zeros((pp * NSEG_PAGE, half), F32)
    acc_v = jnp.zeros((pp * NSEG_PAGE, half), F32)
    for s in range(CMP_STRIDE):
        rows = pl.ds(s, NSEG_PAGE, stride=CMP_STRIDE)
        xk = jnp.concatenate([r[rows, :] for r in pk], axis=0)
        xv = jnp.concatenate([r[rows, :] for r in pv], axis=0)
        acc_k = acc_k + _dot(xk, wk_ref[s])
        acc_v = acc_v + _dot(xv, wv_ref[s])
    o_ref[...] = jnp.concatenate([acc_k, acc_v], axis=1)


def _page_specs(layer, lane_tile, n_pages, pp=PAGES_PER_STEP):
    def spec(p):
        def imap(bi, i, pt):
            return (layer, pt[bi, jnp.minimum(i * pp + p, n_pages - 1)], 0, lane_tile)
        return pl.BlockSpec((None, None, PAGE, LANE), imap)
    return [spec(p) for p in range(pp)]


def cmp_h_pages(cache, page_table, layer, wk, wv):
    b, n_pages = page_table.shape
    pp = CMP_PAGES_PER_STEP
    wspec = pl.BlockSpec((CMP_STRIDE, LANE, CMP_HW // 2), lambda bi, i, pt: (0, 0, 0))
    gs = pltpu.PrefetchScalarGridSpec(
        num_scalar_prefetch=1, grid=(b, n_pages // pp),
        in_specs=_page_specs(layer, 0, n_pages, pp) + _page_specs(layer, 1, n_pages, pp) + [wspec, wspec],
        out_specs=pl.BlockSpec((None, pp * NSEG_PAGE, CMP_HW), lambda bi, i, pt: (bi, i, 0)))
    return pl.pallas_call(
        _cmp_h_pages_kernel, grid_spec=gs,
        out_shape=jax.ShapeDtypeStruct((b, n_pages * NSEG_PAGE, CMP_HW), F32),
        compiler_params=_cparams(("arbitrary", "arbitrary")),
        name="cmp_h_pages",
    )(page_table, *([cache] * (2 * pp)), wk, wv)


NCP = 1024


def _cmp_finish_kernel(hn_row, hp_ref, hn_ref, pek_ref, pev_ref, wk_ref, wv_ref, w2k_ref, w2v_ref,
                       ckh_ref, ckl_ref, cvt_ref):
    qw = CMP_HW // 4
    rk = jnp.zeros((8, 2 * qw), F32)
    rv = jnp.zeros((8, 2 * qw), F32)
    for s in range(CMP_STRIDE):
        rk = rk + _dot(pek_ref[s], wk_ref[s])
        rv = rv + _dot(pev_ref[s], wv_ref[s])
    last = _iota((NCP, qw), 0) == hn_row

    def hidden(off, rb):
        h0 = hp_ref[:, off:off + qw]
        h1 = pltpu.roll(hp_ref[:, off + qw:off + 2 * qw], NCP - 1, 0)
        h1 = jnp.where(last, hn_ref[0:1, off + qw:off + 2 * qw], h1)
        return _silu(h0 + h1 + rb[0:1, :qw] + rb[1:2, qw:])

    ckh_ref[...], ckl_ref[...] = _split16(_dot(hidden(0, rk), w2k_ref[...]))
    cvt_ref[...] = _dot(hidden(2 * qw, rv), w2v_ref[...]).T.astype(BF16)


def cmp_finish(hp, hn, hn_row, pek, pev, wk, wv, w2k, w2v):
    b = hp.shape[0]
    full = lambda shp: pl.BlockSpec(shp, lambda bi: (0,) * len(shp))
    return pl.pallas_call(
        functools.partial(_cmp_finish_kernel, hn_row),
        grid=(b,),
        in_specs=[pl.BlockSpec((None, NCP, CMP_HW), lambda bi: (bi, 0, 0)),
                  pl.BlockSpec((None, 8, CMP_HW), lambda bi: (bi, 0, 0)),
                  full((CMP_STRIDE, 8, LANE)), full((CMP_STRIDE, 8, LANE)),
                  full((CMP_STRIDE, LANE, CMP_HW // 2)), full((CMP_STRIDE, LANE, CMP_HW // 2)),
                  full((CMP_HW // 4, LANE)), full((CMP_HW // 4, LANE))],
        out_specs=[pl.BlockSpec((None, NCP, LANE), lambda bi: (bi, 0, 0)),
                   pl.BlockSpec((None, NCP, LANE), lambda bi: (bi, 0, 0)),
                   pl.BlockSpec((None, LANE, NCP), lambda bi: (bi, 0, 0))],
        out_shape=[jax.ShapeDtypeStruct((b, NCP, LANE), BF16),
                   jax.ShapeDtypeStruct((b, NCP, LANE), BF16),
                   jax.ShapeDtypeStruct((b, LANE, NCP), BF16)],
        compiler_params=_cparams(("arbitrary",)),
        name="cmp_finish",
    )(hp, hn, pek, pev, wk, wv, w2k, w2v)


def _cmp_weights(pe, w1, w2):
    eye_g = jnp.eye(NSA_KVH, dtype=F32)

    def first(e):
        w = w1[e].reshape(2, CMP_STRIDE, NSA_HD, CMP_HIDDEN)
        w = jnp.einsum('isdf,gh->sgdihf', w, eye_g)
        return w.reshape(CMP_STRIDE, LANE, CMP_HW // 2).astype(BF16)

    def second(e):
        return jnp.einsum('fd,gh->gfhd', w2[e], eye_g).reshape(CMP_HW // 4, LANE).astype(BF16)

    def pos(e):
        p = pe[e].reshape(2, CMP_STRIDE, NSA_HD)
        p = jnp.tile(jnp.transpose(p, (1, 0, 2)), (1, 1, NSA_KVH))
        return jnp.pad(p, ((0, 0), (0, 6), (0, 0)))

    return pos(0), pos(1), first(0), first(1), second(0), second(1)


def _kvprep_rows_kernel(nsub, tk, xk_ref, xv_ref, k_ref, vt_ref):
    k_ref[...] = xk_ref[...].astype(BF16)
    for u in range(nsub):
        vt_ref[u] = xv_ref[u * tk:(u + 1) * tk, :].T.astype(BF16)


def kvprep_rows(rows, tk, lt=0):
    b, t, _ = rows.shape
    tt = 512 if t % 512 == 0 else tk
    nsub = tt // tk
    return pl.pallas_call(
        functools.partial(_kvprep_rows_kernel, nsub, tk),
        grid=(b, t // tt),
        in_specs=[pl.BlockSpec((None, tt, LANE), lambda bi, i: (bi, i, lt)),
                  pl.BlockSpec((None, tt, LANE), lambda bi, i: (bi, i, lt + 1))],
        out_specs=[pl.BlockSpec((None, tt, LANE), lambda bi, i: (bi, i, 0)),
                   pl.BlockSpec((None, nsub, LANE, tk), lambda bi, i: (bi, i, 0, 0))],
        out_shape=[jax.ShapeDtypeStruct((b, t, LANE), BF16),
                   jax.ShapeDtypeStruct((b, t // tk, LANE, tk), BF16)],
        compiler_params=_cparams(("arbitrary", "arbitrary")),
        name="kvprep_rows",
    )(rows, rows)


SEL_TK = 512
WIN_TK = 128


def _kvprep_pages_kernel(n_steps, pt_ref, *refs):
    pp = PAGES_PER_STEP
    pk, pv = refs[:pp], refs[pp:2 * pp]
    nk_ref, nv_ref, k_ref, vt_ref = refs[2 * pp:]
    i = pl.program_id(1)
    per = SEL_TK // PAGE

    @pl.when(i < n_steps - 1)
    def _():
        for p in range(pp):
            k_ref[p * PAGE:(p + 1) * PAGE, :] = pk[p][...].astype(BF16)
            vt_ref[p // per, :, (p % per) * PAGE:(p % per + 1) * PAGE] = pv[p][...].T.astype(BF16)

    @pl.when(i == n_steps - 1)
    def _():
        k_ref[...] = jnp.zeros(k_ref.shape, BF16)
        vt_ref[...] = jnp.zeros(vt_ref.shape, BF16)
        k_ref[0:PAGE, :] = nk_ref[...].astype(BF16)
        vt_ref[0, :, 0:PAGE] = nv_ref[...].T.astype(BF16)


def kvprep_pages(cache, page_table, layer, new_rows, lt0=0):
    b, n_pages = page_table.shape
    pp = PAGES_PER_STEP
    n_steps = n_pages // pp + 1
    t = n_steps * pp * PAGE
    new = lambda lt: pl.BlockSpec((None, PAGE, LANE), lambda bi, i, pt: (bi, 0, lt0 + lt))
    gs = pltpu.PrefetchScalarGridSpec(
        num_scalar_prefetch=1, grid=(b, n_steps),
        in_specs=_page_specs(layer, 0, n_pages) + _page_specs(layer, 1, n_pages) + [new(0), new(1)],
        out_specs=[pl.BlockSpec((None, pp * PAGE, LANE), lambda bi, i, pt: (bi, i, 0)),
                   pl.BlockSpec((None, pp * PAGE // SEL_TK, LANE, SEL_TK), lambda bi, i, pt: (bi, i, 0, 0))])
    return pl.pallas_call(
        functools.partial(_kvprep_pages_kernel, n_steps), grid_spec=gs,
        out_shape=[jax.ShapeDtypeStruct((b, t, LANE), BF16),
                   jax.ShapeDtypeStruct((b, t // SEL_TK, LANE, SEL_TK), BF16)],
        compiler_params=_cparams(("arbitrary", "arbitrary")),
        name="kvprep_pages",
    )(page_table, *([cache] * (2 * pp)), new_rows, new_rows)


TQ = 128
NQL = NSA_HEADS * TQ
NSA_CMP_TILE, NSA_SEL_TILE, NSA_WIN_TILE, NSA_GL_TILE = 8, 10, 12, 14
NSA_W = (NSA_GL_TILE + 1) * LANE
WIN_TILES = WINDOW // WIN_TK + 1


def _rowmax(x):
    return jnp.max(x, axis=0, keepdims=True)


def _rowsum(x):
    return jnp.sum(x, axis=0, keepdims=True)


LOG2E = 1.4426950408889634
SEL_PER = SEL_TK // SEL_BLOCK


def _real(m):
    return jnp.where(m > 0.5 * NEG, m, 0.0)


def _nsa_attn_kernel(pos0, nc, ns, wt0, wpos0, n_win,
                     q_ref, gl_ref, ckh_ref, ckl_ref, cvt_ref, ks_ref, vst_ref, *rest):
    kw = rest[:WIN_TILES]
    vw = rest[WIN_TILES:2 * WIN_TILES]
    msel_ref, o_ref, sel_scr, work_scr, sel3_scr, imp_scr, oc_scr, acc_scr, ml_scr, ot_scr = rest[2 * WIN_TILES:]
    i = pl.program_id(1)
    t0 = pos0 + i * TQ
    nsp = sel_scr.shape[0]
    hd = NSA_HD

    qt = (q_ref[...] * (hd ** -0.5 * LOG2E)).T
    zero = jnp.zeros((hd, TQ), F32)
    pieces = []
    for jh in range(NSA_HEADS):
        blk = qt[jh * hd:(jh + 1) * hd, :]
        pieces.append(jnp.concatenate([blk, zero] if jh < NSA_HPG else [zero, blk], axis=0))
    qbd = jnp.concatenate(pieces, axis=1)
    qbd16 = qbd.astype(BF16)
    t_q = t0 + _iota((1, TQ), 1)
    gw = NSA_HPG * TQ

    def heads(x, n=NSA_HPG):
        return jnp.concatenate([x] * n, axis=1)

    c_idx = _iota((NCP, TQ), 0)
    cbias = jnp.where((CMP_STRIDE * c_idx + (CMP_BLOCK - 1) <= t_q) & (c_idx < nc), 0.0, NEG)
    for g in range(NSA_KVH):
        gs = slice(g * gw, (g + 1) * gw)
        qh, ql = _split16(qbd[:, gs])
        ckh = ckh_ref[...]
        s = _dot(ckh, qh) + _dot(ckh, ql) + _dot(ckl_ref[...], qh) + heads(cbias)
        p = jnp.exp2(s - _real(_rowmax(s)))
        l = _rowsum(p)
        inv = 1.0 / jnp.where(l > 0.0, l, 1.0)
        oc_scr[g] = _dot(cvt_ref[g * hd:(g + 1) * hd, :], p) * inv
        pc = p * inv
        imp = pc[:, :TQ]
        for h in range(1, NSA_HPG):
            imp = imp + pc[:, h * TQ:(h + 1) * TQ]
        imp_scr[:, g * TQ:(g + 1) * TQ] = imp
    pslc = _dot_hi(msel_ref[...], imp_scr[...])

    jrow = _iota((nsp, 2 * TQ), 0)
    t_gq = t0 + (_iota((1, 2 * TQ), 1) % TQ)
    cur = lax.shift_right_logical(t_gq, 6)
    forced = (jrow == 0) | (jrow == cur) | (jrow == cur - 1)
    score = jnp.where(jrow * SEL_BLOCK <= t_gq, pslc + jnp.where(forced, FORCE_BONUS, 0.0), NEG)
    work_scr[...] = jnp.where(jrow < ns, score, -3e38)
    sel_scr[...] = jnp.zeros(sel_scr.shape, F32)
    jrow_f = jrow.astype(F32)

    def pick(_, carry):
        w = work_scr[...]
        best = _rowmax(w)
        first = jnp.min(jnp.where(w == best, jrow_f, 1e9), axis=0, keepdims=True)
        hit = jrow_f == first
        sel_scr[...] = jnp.where(hit, 1.0, sel_scr[...])
        work_scr[...] = jnp.where(hit, -jnp.inf, w)
        return carry

    lax.fori_loop(0, N_SEL, pick, 0)
    for u in range(nsp // SEL_PER):
        sel3_scr[u] = jnp.where(sel_scr[u * SEL_PER:(u + 1) * SEL_PER, :] > 0.5, 0.0, NEG)

    last = (t0 + TQ + SEL_TK - 1) // SEL_TK - 1
    ml_scr[0:1, :] = jnp.full((1, NQL), NEG, F32)
    ml_scr[1:2, :] = jnp.zeros((1, NQL), F32)
    acc_scr[...] = jnp.zeros(acc_scr.shape, F32)

    ones16 = jnp.ones((16, SEL_TK), BF16)

    def sel_tile(kt, extra):
        s = _dot(ks_ref[pl.ds(pl.multiple_of(kt * SEL_TK, SEL_TK), SEL_TK), :], qbd16).astype(BF16)
        sb = sel3_scr[kt].astype(BF16)
        s = jnp.concatenate(
            [s[jj * SEL_BLOCK:(jj + 1) * SEL_BLOCK, :]
             + jnp.concatenate([heads(sb[jj:jj + 1, :TQ]), heads(sb[jj:jj + 1, TQ:])], axis=1)
             for jj in range(SEL_PER)], axis=0)
        if extra is not None:
            s = s + extra
        m_old = ml_scr[0:1, :]
        m_new = jnp.maximum(m_old, _rowmax(s).astype(F32))
        alpha = jnp.exp2(m_old - m_new)
        p = jnp.exp2(s - m_new.astype(BF16))
        ml_scr[0:1, :] = m_new
        for g in range(NSA_KVH):
            gs = slice(g * gw, (g + 1) * gw)
            vt1 = jnp.concatenate([vst_ref[kt, g * hd:(g + 1) * hd, :], ones16], axis=0)
            pv = _dot(vt1, p[:, gs])
            acc_scr[g] = acc_scr[g] * alpha[:, gs] + pv[:hd, :]
            ml_scr[1:2, gs] = alpha[:, gs] * ml_scr[1:2, gs] + pv[hd:hd + 1, :]

    def sel_body(kt, carry):
        sel_tile(kt, None)
        return carry

    lax.fori_loop(0, last, sel_body, 0)
    krow = _iota((SEL_TK, TQ), 0)
    sel_tile(last, heads(jnp.where(last * SEL_TK + krow <= t_q, 0.0, NEG).astype(BF16), NSA_HEADS))
    l_sel = ml_scr[1:2, :]
    inv_sel = 1.0 / jnp.where(l_sel > 0.0, l_sel, 1.0)

    wrow = _iota((WIN_TK, TQ), 0)
    m = jnp.full((1, NQL), NEG, F32)
    l = jnp.zeros((1, NQL), F32)
    acc_w = [jnp.zeros((hd, gw), F32) for _ in range(NSA_KVH)]
    for w in range(WIN_TILES):
        tile = i + (wt0 - (WIN_TILES - 1) + w)
        idx = tile * WIN_TK + wrow
        wpos = wpos0 + idx
        dlt = t_q - wpos
        ok = (dlt >= 0) & (dlt < WINDOW) & (wpos >= 0) & (idx < n_win) & (tile >= 0)
        s = _dot(kw[w][...], qbd16) + heads(jnp.where(ok, 0.0, NEG), NSA_HEADS)
        m_new = jnp.maximum(m, _rowmax(s))
        alpha = jnp.where(m > 0.5 * NEG, jnp.exp2(m - _real(m_new)), 0.0)
        p = jnp.exp2(s - _real(m_new))
        l = alpha * l + _rowsum(p)
        p16 = p.astype(BF16)
        for g in range(NSA_KVH):
            gs = slice(g * gw, (g + 1) * gw)
            acc_w[g] = acc_w[g] * alpha[:, gs] + _dot(vw[w][g * hd:(g + 1) * hd, :], p16[:, gs])
        m = m_new
    inv_win = 1.0 / jnp.where(l > 0.0, l, 1.0)

    gt = _sigmoid(gl_ref[...]).T
    for jh in range(NSA_HEADS):
        g, h = divmod(jh, NSA_HPG)
        hl = slice(h * TQ, (h + 1) * TQ)
        ls = slice(jh * TQ, (jh + 1) * TQ)
        ot_scr[jh * hd:(jh + 1) * hd, :] = (
            gt[3 * jh:3 * jh + 1, :] * oc_scr[g, :, hl]
            + gt[3 * jh + 1:3 * jh + 2, :] * (acc_scr[g, :, hl] * inv_sel[:, ls])
            + gt[3 * jh + 2:3 * jh + 3, :] * (acc_w[g][:, hl] * inv_win[:, ls]))
    o_ref[...] = ot_scr[...].T


def nsa_attention(cols, ckh, ckl, cvt, ks, vst, kwin, vwt, pos0, n_new, n_buf, n_win):
    b, lq, _ = cols.shape
    q = gl = cols
    tk_total = ks.shape[1]
    n_sel_tiles = tk_total // SEL_TK
    nsp = tk_total // SEL_BLOCK
    t_total = pos0 + n_new
    nseg = -(-t_total // CMP_STRIDE)
    nc = nseg - CMP_BLOCK // CMP_STRIDE + 1
    ns = -(-t_total // SEL_BLOCK)
    wt0 = n_buf // WIN_TK
    n_wtiles = kwin.shape[1] // WIN_TK
    j = jnp.arange(nsp)[:, None]
    c = jnp.arange(NCP)[None, :]
    msel = ((c >= 4 * j - 1) & (c <= 4 * j + 3)).astype(F32)

    def wspec(w, vt):
        def imap(bi, i):
            tile = jnp.clip(i + (wt0 - (WIN_TILES - 1) + w), 0, n_wtiles - 1)
            return (bi, tile, 0, 0) if vt else (bi, tile, 0)
        return pl.BlockSpec((None, None, LANE, WIN_TK) if vt else (None, WIN_TK, LANE), imap)

    kern = functools.partial(_nsa_attn_kernel, pos0, nc, ns, wt0, pos0 - n_buf, n_win)
    return pl.pallas_call(
        kern,
        grid=(b, lq // TQ),
        in_specs=[pl.BlockSpec((None, TQ, NSA_DIM), lambda bi, i: (bi, i, 0)),
                  pl.BlockSpec((None, TQ, LANE), lambda bi, i: (bi, i, NSA_GL_TILE)),
                  pl.BlockSpec((None, NCP, LANE), lambda bi, i: (bi, 0, 0)),
                  pl.BlockSpec((None, NCP, LANE), lambda bi, i: (bi, 0, 0)),
                  pl.BlockSpec((None, LANE, NCP), lambda bi, i: (bi, 0, 0)),
                  pl.BlockSpec((None, tk_total, LANE), lambda bi, i: (bi, 0, 0)),
                  pl.BlockSpec((None, n_sel_tiles, LANE, SEL_TK), lambda bi, i: (bi, 0, 0, 0))]
                 + [wspec(w, False) for w in range(WIN_TILES)]
                 + [wspec(w, True) for w in range(WIN_TILES)]
                 + [pl.BlockSpec((nsp, NCP), lambda bi, i: (0, 0))],
        out_specs=pl.BlockSpec((None, TQ, NSA_DIM), lambda bi, i: (bi, i, 0)),
        out_shape=jax.ShapeDtypeStruct((b, lq, NSA_DIM), F32),
        scratch_shapes=[pltpu.VMEM((nsp, 2 * TQ), F32), pltpu.VMEM((nsp, 2 * TQ), F32),
                        pltpu.VMEM((nsp // SEL_PER, SEL_PER, 2 * TQ), F32),
                        pltpu.VMEM((NCP, 2 * TQ), F32),
                        pltpu.VMEM((NSA_KVH, NSA_HD, NSA_HPG * TQ), F32),
                        pltpu.VMEM((NSA_KVH, NSA_HD, NSA_HPG * TQ), F32),
                        pltpu.VMEM((8, NQL), F32), pltpu.VMEM((NSA_DIM, TQ), F32)],
        compiler_params=_cparams(("arbitrary", "arbitrary")),
        name="nsa_attention",
    )(q, gl, ckh, ckl, cvt, ks, vst, *([kwin] * WIN_TILES), *([vwt] * WIN_TILES), msel)


TQS = 8


def _nsa_attn_small_kernel(pos0, nc, ns, wpos0, n_win, q_ref, gl_ref, ckh_ref, ckl_ref, cvt_ref, ks_ref,
                           vst_ref, kw_ref, vw_ref, msel_ref, eh_ref, eht_ref, gx_ref, o_ref,
                           sel_scr, work_scr, sel3_scr):
    hd = NSA_HD
    nsp = sel_scr.shape[0]
    lane8 = _iota((TQS, LANE), 1)
    lane = _iota((1, LANE), 1)
    t_lane = pos0 + (lane % TQS)
    grp1 = lane >= NSA_HPG * TQS

    q = q_ref[...] * (hd ** -0.5 * LOG2E)
    pieces = []
    for jh in range(NSA_HEADS):
        tile = q[:, (jh // 2) * LANE:(jh // 2 + 1) * LANE]
        dst_hi = jh >= NSA_HPG
        if (jh % 2 == 1) != dst_hi:
            tile = pltpu.roll(tile, hd, 1)
        pieces.append(jnp.where((lane8 >= hd) if dst_hi else (lane8 < hd), tile, 0.0))
    z = jnp.concatenate(pieces, axis=0).T
    z16 = z.astype(BF16)
    zh, zl = _split16(z)

    def own_group(full):
        return jnp.where(grp1, full[hd:, :], full[:hd, :])

    c_idx = _iota((NCP, LANE), 0)
    ckh = ckh_ref[...]
    s = _dot(ckh, zh) + _dot(ckh, zl) + _dot(ckl_ref[...], zh)
    s = s + jnp.where((CMP_STRIDE * c_idx + (CMP_BLOCK - 1) <= t_lane) & (c_idx < nc), 0.0, NEG)
    p = jnp.exp2(s - _real(_rowmax(s)))
    l = _rowsum(p)
    inv = 1.0 / jnp.where(l > 0.0, l, 1.0)
    o_cmp = own_group(_dot(cvt_ref[...], p) * inv)
    pslc = _dot_hi(msel_ref[...], _dot_hi(p * inv, eht_ref[...]))

    jrow = _iota((nsp, LANE), 0)
    cur = lax.shift_right_logical(t_lane, 6)
    forced = (jrow == 0) | (jrow == cur) | (jrow == cur - 1)
    score = jnp.where(jrow * SEL_BLOCK <= t_lane, pslc + jnp.where(forced, FORCE_BONUS, 0.0), NEG)
    work_scr[...] = jnp.where(jrow < ns, score, -3e38)
    sel_scr[...] = jnp.zeros(sel_scr.shape, F32)
    jrow_f = jrow.astype(F32)

    def pick(_, carry):
        w = work_scr[...]
        best = _rowmax(w)
        first = jnp.min(jnp.where(w == best, jrow_f, 1e9), axis=0, keepdims=True)
        hit = jrow_f == first
        sel_scr[...] = jnp.where(hit, 1.0, sel_scr[...])
        work_scr[...] = jnp.where(hit, -jnp.inf, w)
        return carry

    lax.fori_loop(0, N_SEL, pick, 0)
    on = _dot(sel_scr[...], eh_ref[...])
    for u in range(nsp // SEL_PER):
        sel3_scr[u] = jnp.where(on[u * SEL_PER:(u + 1) * SEL_PER, :] > 0.5, 0.0, NEG)

    last = (pos0 + TQS + SEL_TK - 1) // SEL_TK - 1
    krow = _iota((SEL_TK, LANE), 0)

    def sel_tile(kt, carry, extra):
        m, l, acc = carry
        s = _dot(ks_ref[pl.ds(pl.multiple_of(kt * SEL_TK, SEL_TK), SEL_TK), :], z16)
        sb = sel3_scr[kt]
        s = jnp.concatenate([s[jj * SEL_BLOCK:(jj + 1) * SEL_BLOCK, :] + sb[jj:jj + 1, :]
                             for jj in range(SEL_PER)], axis=0)
        if extra is not None:
            s = s + extra
        m_new = jnp.maximum(m, _rowmax(s))
        alpha = jnp.exp2(m - m_new)
        p = jnp.exp2(s - m_new)
        return m_new, alpha * l + _rowsum(p), acc * alpha + _dot(vst_ref[kt], p)

    init = (jnp.full((1, LANE), NEG, F32), jnp.zeros((1, LANE), F32), jnp.zeros((2 * hd, LANE), F32))
    carry = lax.fori_loop(0, last, lambda kt, c: sel_tile(kt, c, None), init)
    _, l, acc = sel_tile(last, carry, jnp.where(last * SEL_TK + krow <= t_lane, 0.0, NEG))
    o_sel = own_group(acc * (1.0 / jnp.where(l > 0.0, l, 1.0)))

    wrow = _iota((WIN_TK, LANE), 0)
    m = jnp.full((1, LANE), NEG, F32)
    l = jnp.zeros((1, LANE), F32)
    acc = jnp.zeros((2 * hd, LANE), F32)
    for w in range(kw_ref.shape[0] // WIN_TK):
        idx = w * WIN_TK + wrow
        wpos = wpos0 + idx
        dlt = t_lane - wpos
        ok = (dlt >= 0) & (dlt < WINDOW) & (wpos >= 0) & (idx < n_win)
        s = _dot(kw_ref[w * WIN_TK:(w + 1) * WIN_TK, :], z16) + jnp.where(ok, 0.0, NEG)
        m_new = jnp.maximum(m, _rowmax(s))
        alpha = jnp.where(m > 0.5 * NEG, jnp.exp2(m - _real(m_new)), 0.0)
        p = jnp.exp2(s - _real(m_new))
        l = alpha * l + _rowsum(p)
        acc = acc * alpha + _dot(vw_ref[w], p)
        m = m_new
    o_win = own_group(acc * (1.0 / jnp.where(l > 0.0, l, 1.0)))

    def to_rows(o):
        ot = jnp.concatenate([o, o], axis=0).T
        tiles = []
        for k in range(NSA_HEADS // 2):
            a = ot[(2 * k) * TQS:(2 * k + 1) * TQS, :]
            b = ot[(2 * k + 1) * TQS:(2 * k + 2) * TQS, :]
            tiles.append(jnp.where(lane8 < hd, a, b))
        return jnp.concatenate(tiles, axis=1)

    sg = _sigmoid(gl_ref[...])
    o_ref[...] = (_dot_hi(sg, gx_ref[0]) * to_rows(o_cmp) + _dot_hi(sg, gx_ref[1]) * to_rows(o_sel)
                  + _dot_hi(sg, gx_ref[2]) * to_rows(o_win))


def nsa_attention_small(cols, ckh, ckl, cvt, ks, vst, kwin, vwt, pos0, n_buf, n_win):
    b, lq, _ = cols.shape
    assert lq == TQS
    tk_total = ks.shape[1]
    nsp = tk_total // SEL_BLOCK
    t_total = pos0 + lq
    nc = -(-t_total // CMP_STRIDE) - CMP_BLOCK // CMP_STRIDE + 1
    ns = -(-t_total // SEL_BLOCK)
    j = jnp.arange(nsp)[:, None]
    c = jnp.arange(NCP)[None, :]
    msel = ((c >= 4 * j - 1) & (c <= 4 * j + 3)).astype(F32)
    n = jnp.arange(LANE)
    head, qi = n // TQS, n % TQS
    gq = (head // NSA_HPG) * TQS + qi
    eh = (jnp.arange(LANE)[:, None] == gq[None, :]).astype(F32)
    col = jnp.arange(NSA_DIM) // NSA_HD
    gx = jnp.stack([(jnp.arange(LANE)[:, None] == (3 * col + br)[None, :]).astype(F32) for br in range(3)])
    full = lambda a: pl.BlockSpec(a.shape, lambda bi: (0,) * a.ndim)
    per_b = lambda a: pl.BlockSpec((None,) + a.shape[1:], lambda bi: (bi,) + (0,) * (a.ndim - 1))
    kern = functools.partial(_nsa_attn_small_kernel, pos0, nc, ns, pos0 - n_buf, n_win)
    return pl.pallas_call(
        kern,
        grid=(b,),
        in_specs=[pl.BlockSpec((None, TQS, NSA_DIM), lambda bi: (bi, 0, 0)),
                  pl.BlockSpec((None, TQS, LANE), lambda bi: (bi, 0, NSA_GL_TILE)),
                  per_b(ckh), per_b(ckl), per_b(cvt), per_b(ks), per_b(vst), per_b(kwin), per_b(vwt),
                  full(msel), full(eh), full(eh), full(gx)],
        out_specs=pl.BlockSpec((None, TQS, NSA_DIM), lambda bi: (bi, 0, 0)),
        out_shape=jax.ShapeDtypeStruct((b, TQS, NSA_DIM), F32),
        scratch_shapes=[pltpu.VMEM((nsp, LANE), F32), pltpu.VMEM((nsp, LANE), F32),
                        pltpu.VMEM((nsp // SEL_PER, SEL_PER, LANE), F32)],
        compiler_params=_cparams(("arbitrary",)),
        name="nsa_attention_small",
    )(cols, cols, ckh, ckl, cvt, ks, vst, kwin, vwt, msel, eh, eh.T, gx)


def _layer_norm(v, g, b):
    mu = jnp.mean(v, axis=-1, keepdims=True)
    vc = v - mu
    var = jnp.mean(vc * vc, axis=-1, keepdims=True)
    return vc * lax.rsqrt(var + LN_EPS) * g + b


def _merge_kernel(x_ref, ya_ref, yb_ref, yc_ref, gate_ref, g1_ref, wa_ref, wb_ref, wc_ref, wo_ref,
                  lg_ref, lb_ref, o_ref):
    d = D_MODEL
    merged = (_sigmoid(gate_ref[:, :d]) * _dot(ya_ref[...], wa_ref[...])
              + _sigmoid(gate_ref[:, d:2 * d]) * _dot(yb_ref[...], wb_ref[...])
              + _sigmoid(gate_ref[:, 2 * d:]) * _dot(yc_ref[...], wc_ref[...]))
    o = _dot(merged, wo_ref[...])
    o_ref[...] = _layer_norm(ALPHA * x_ref[...] + g1_ref[...] * o, lg_ref[...], lb_ref[...])


def merge_out(x, ya, yb, yc, gate, g1, wa, wb, wc, wo, lg, lb, tm=256):
    m, d = x.shape
    tm = min(tm, m)
    per_row = g1.shape[0] != 1
    row = lambda w: pl.BlockSpec((tm, w), lambda i: (i, 0))
    mspec = row(d) if per_row else pl.BlockSpec((1, d), lambda i: (0, 0))
    wspec = pl.BlockSpec((d, d), lambda i: (0, 0))
    vspec = pl.BlockSpec((1, d), lambda i: (0, 0))
    return pl.pallas_call(
        _merge_kernel,
        grid=(m // tm,),
        in_specs=[row(d), row(d), row(d), row(d), row(3 * d), mspec, wspec, wspec, wspec, wspec, vspec, vspec],
        out_specs=row(d),
        out_shape=jax.ShapeDtypeStruct((m, d), F32),
        compiler_params=_cparams(("arbitrary",)),
        name="merge_out",
    )(x, ya, yb, yc, gate, g1, wa, wb, wc, wo, lg, lb)


def _lane_first(mask, lane_f):
    return jnp.min(jnp.where(mask, lane_f, 1e9), axis=-1, keepdims=True)


def _moe_kernel(x_ref, sc_ref, sh_ref, g2_ref, wr_ref, w1_ref, w3_ref, w2_ref, lg_ref, lb_ref,
                o_ref, u_scr, gate_scr, acc_scr):
    e = pl.program_id(1)
    tm = x_ref.shape[0]
    lane = _iota((tm, LANE), 1)
    lane_f = lane.astype(F32)

    @pl.when(e == 0)
    def _():
        u = x_ref[...] * (1.0 + sc_ref[...]) + sh_ref[...]
        u_scr[...] = u.astype(BF16)
        logits = _dot_hi(u, wr_ref[...])
        lg = jnp.where(lane < N_GROUPS, logits, -jnp.inf)
        gmax = jnp.max(lg, axis=-1, keepdims=True)
        gstar = _lane_first(lg == gmax, lane_f)
        pg = 1.0 / jnp.sum(jnp.exp(lg - gmax), axis=-1, keepdims=True)
        in_grp = (lane >= N_GROUPS) & (lane < N_GROUPS + N_EXPERTS) & (
            lax.shift_right_logical(lane - N_GROUPS, 2).astype(F32) == gstar)
        le = jnp.where(in_grp, logits, -jnp.inf)
        v1 = jnp.max(le, axis=-1, keepdims=True)
        i1 = _lane_first(le == v1, lane_f)
        le2 = jnp.where(lane_f == i1, -jnp.inf, le)
        v2 = jnp.max(le2, axis=-1, keepdims=True)
        i2 = _lane_first(le2 == v2, lane_f)
        e2 = jnp.exp(v2 - v1)
        den = 1.0 / (1.0 + e2)
        gate_scr[...] = jnp.where(lane_f == i1, den * pg, jnp.where(lane_f == i2, e2 * den * pg, 0.0))
        acc_scr[...] = jnp.zeros(acc_scr.shape, F32)

    u = u_scr[...]
    gate = gate_scr[...]
    out = acc_scr[...]
    for k in range(MOE_EPS):
        ge = jnp.sum(jnp.where(lane == e * MOE_EPS + (k + N_GROUPS), gate, 0.0), axis=-1, keepdims=True)
        h = _silu(_dot(u, w1_ref[k])) * _dot(u, w3_ref[k])
        out = out + _dot(h * ge, w2_ref[k])
    acc_scr[...] = out

    @pl.when(e == pl.num_programs(1) - 1)
    def _():
        o_ref[...] = _layer_norm(ALPHA * x_ref[...] + g2_ref[...] * acc_scr[...], lg_ref[...], lb_ref[...])


MOE_EPS = 1


def moe_out(x, sc, sh, g2, wr, w1, w3, w2, lg, lb, tm=512):
    m, d = x.shape
    tm = min(tm, m)
    per_row = sc.shape[0] != 1
    row = pl.BlockSpec((tm, d), lambda i, e: (i, 0))
    mspec = row if per_row else pl.BlockSpec((1, d), lambda i, e: (0, 0))
    vspec = pl.BlockSpec((1, d), lambda i, e: (0, 0))
    return pl.pallas_call(
        _moe_kernel,
        grid=(m // tm, N_EXPERTS // MOE_EPS),
        in_specs=[row, mspec, mspec, mspec, pl.BlockSpec((d, LANE), lambda i, e: (0, 0)),
                  pl.BlockSpec((MOE_EPS, d, EXPERT_HIDDEN), lambda i, e: (e, 0, 0)),
                  pl.BlockSpec((MOE_EPS, d, EXPERT_HIDDEN), lambda i, e: (e, 0, 0)),
                  pl.BlockSpec((MOE_EPS, EXPERT_HIDDEN, d), lambda i, e: (e, 0, 0)), vspec, vspec],
        out_specs=row,
        out_shape=jax.ShapeDtypeStruct((m, d), F32),
        scratch_shapes=[pltpu.VMEM((tm, d), BF16), pltpu.VMEM((tm, LANE), F32), pltpu.VMEM((tm, d), F32)],
        compiler_params=_cparams(("arbitrary", "arbitrary")),
        name="moe_out",
    )(x, sc, sh, g2, wr, w1, w3, w2, lg, lb)


def _pad_rows(a, n):
    return a if a.shape[1] == n else jnp.pad(a, ((0, 0), (0, n - a.shape[1]), (0, 0)))


def _layer_weights(l, p):
    w_in = p['w_in'][l]
    o1, o2, o3 = SSD_COLS, SSD_COLS + RWKV_COLS, SSD_COLS + RWKV_COLS + NSA_COLS
    padc = lambda w, n: jnp.pad(w, ((0, 0), (0, n - w.shape[1])))
    w_ssd = padc(w_in[:, :o1], SSD_W)
    w_nsa = padc(w_in[:, o2:o3], NSA_W)
    z64 = jnp.zeros((64, RWKV_DIM), F32)
    seg = _head_onehot(RWKV_HEADS, RWKV_HD, LANE).T
    rwkv_prm = (p['rwkv_mu'][l][None], p['rwkv_w0'][l][None], p['rwkv_a0'][l][None], p['rwkv_k_k'][l][None],
                p['rwkv_k_a'][l][None], p['rwkv_r_k'][l].reshape(1, RWKV_DIM),
                jnp.concatenate([p['rwkv_w2'][l], z64], 0), jnp.concatenate([z64, p['rwkv_a2'][l]], 0),
                p['rwkv_g2'][l], seg, seg.T)
    wr = jnp.pad(jnp.concatenate([p['w_group'][l], p['w_router'][l]], axis=1),
                 ((0, 0), (0, LANE - N_GROUPS - N_EXPERTS)))
    row = lambda v: v[None]
    return dict(
        w_ssd=w_ssd.astype(BF16), w_rwkv=w_in[:, o1:o2].astype(BF16), w_nsa=w_nsa.astype(BF16),
        w_gate=w_in[:, o3:].astype(BF16),
        ssd=(jnp.pad(p['ssd_conv_w'][l], ((0, 4), (0, 0))), row(p['ssd_conv_b'][l]),
             row(jnp.pad(p['ssd_dt_bias'][l], (0, LANE - SSD_HEADS))), row(jnp.pad(p['ssd_a_log'][l], (0, LANE - SSD_HEADS))),
             row(jnp.repeat(p['ssd_d'][l], SSD_HD)), row(p['ssd_norm_w'][l])),
        rwkv=rwkv_prm, lnx=(row(p['rwkv_lnx_w'][l]), row(p['rwkv_lnx_b'][l])),
        cmp=_cmp_weights(p['cmp_pe'][l], p['cmp_w1'][l], p['cmp_w2'][l]),
        wo=tuple(p[k][l].astype(BF16) for k in ('w_o_ssd', 'w_o_rwkv', 'w_o_nsa', 'w_out')),
        ln1=(row(p['ln1_g'][l]), row(p['ln1_b'][l])), ln2=(row(p['ln2_g'][l]), row(p['ln2_b'][l])),
        wr=wr, w1=p['moe_w1'][l].astype(BF16), w3=p['moe_w3'][l].astype(BF16), w2=p['moe_w2'][l].astype(BF16))


def _trunk_layer(x, mod, l, w, pos0, conv_st, ssm_st, shift_st, rwkv_st, cache_win, cache_cmp, cache_sel,
                 page_table):
    b, L, d = x.shape
    m = b * L
    x2 = x.reshape(m, d)
    sh1, sc1, g1, sh2, sc2, g2 = [mod[:, k * d:(k + 1) * d] for k in range(6)]
    if b > 1:
        sh1, sc1, g1, sh2, sc2, g2 = [jnp.repeat(t, L, axis=0) for t in (sh1, sc1, g1, sh2, sc2, g2)]
    proj = lambda wt: mod_proj(x2, sc1, sh1, wt).reshape(b, L, -1)
    c_ssd, c_rwkv, c_nsa, c_gate = proj(w['w_ssd']), proj(w['w_rwkv']), proj(w['w_nsa']), proj(w['w_gate'])

    lp = -(-L // SSD_Q) * SSD_Q
    cst8 = jnp.pad(conv_st, ((0, 0), (8 - (SSD_CONV - 1), 0), (0, 0)))
    h0t = jnp.transpose(ssm_st.reshape(b, SSD_INNER, SSD_STATE), (0, 2, 1))
    y_a, ht = ssd_mixer(_pad_rows(c_ssd, lp), cst8, h0t, L, *w['ssd'])
    y_a = y_a[:, :L]
    ssm_new = jnp.transpose(ht, (0, 2, 1)).reshape(b, SSD_HEADS, SSD_HD, SSD_STATE)
    conv_new = c_ssd[:, L - (SSD_CONV - 1):, SSD_INNER:SSD_INNER + SSD_CONV_DIM]

    lp = -(-L // RWKV_C) * RWKV_C
    sh8 = jnp.pad(shift_st[:, None, :], ((0, 0), (7, 0), (0, 0)))
    pre = rwkv_prep(_pad_rows(c_rwkv, lp), sh8, L, w['rwkv'])
    y_b, sp = rwkv_scan(*pre, _pair_blockdiag(rwkv_st), *w['lnx'])
    y_b = y_b[:, :L]
    rwkv_new = _pair_unblock(sp)
    shift_new = c_rwkv[:, -1]

    pek, pev, wk, wv, w2k, w2v = w['cmp']
    kvrow = (2, NSA_KVH, NSA_HD)
    rows = lambda tile: c_nsa[:, :, tile * LANE:(tile + 2) * LANE]
    if cache_cmp is None:
        hp = cmp_h_rows(c_nsa, wk, wv, NSA_CMP_TILE)
        hp = _pad_rows(hp, NCP)
        hn = jnp.zeros((b, 8, CMP_HW), F32)
        ks, vst = kvprep_rows(c_nsa, SEL_TK, NSA_SEL_TILE)
        kwin, vwt = kvprep_rows(c_nsa, WIN_TK, NSA_WIN_TILE)
        cq, n_buf, n_win = c_nsa, 0, L
        win_new = rows(NSA_WIN_TILE)[:, max(L - WINDOW, 0):]
    else:
        cq = _pad_rows(c_nsa, TQ)
        hp = _pad_rows(cmp_h_pages(cache_cmp, page_table, l, wk, wv), NCP)
        hn = cmp_h_rows(cq, wk, wv, NSA_CMP_TILE)
        ks, vst = kvprep_pages(cache_sel, page_table, l, cq, NSA_SEL_TILE)
        n_buf = cache_win.shape[1]
        win_all = jnp.concatenate([cache_win.reshape(b, n_buf, 2 * LANE), rows(NSA_WIN_TILE)], axis=1)
        n_win = n_buf + L
        kwin, vwt = kvprep_rows(_pad_rows(win_all, -(-(n_buf + TQ) // WIN_TK) * WIN_TK), WIN_TK)
        win_new = win_all[:, n_win - min(WINDOW, n_win):]
    hn_row = NCP - 1 if cache_cmp is None else pos0 // CMP_STRIDE - 1
    ckh, ckl, cvt = cmp_finish(hp, hn, hn_row, pek, pev, wk, wv, w2k, w2v)
    if L == TQS:
        y_c = nsa_attention_small(c_nsa, ckh, ckl, cvt, ks, vst, kwin, vwt, pos0, n_buf, n_win)
    else:
        y_c = nsa_attention(cq, ckh, ckl, cvt, ks, vst, kwin, vwt, pos0, L, n_buf, n_win)[:, :L]
    cmp_rows = rows(NSA_CMP_TILE).reshape((b, L) + kvrow)
    sel_rows = rows(NSA_SEL_TILE).reshape((b, L) + kvrow)
    win_new = win_new.reshape(win_new.shape[:2] + kvrow)

    flat = lambda t: t.reshape(m, -1)
    x1 = merge_out(x2, flat(y_a), flat(y_b), flat(y_c), flat(c_gate), g1, *w['wo'], *w['ln1'])
    x_out = moe_out(x1, sc2, sh2, g2, w['wr'], w['w1'], w['w3'], w['w2'], *w['ln2'])
    return x_out.reshape(b, L, d), (cmp_rows, sel_rows, win_new, ssm_new, conv_new, rwkv_new, shift_new)


def kernel(x_prompt, x_sample, c_prompt, c_sample, cache_cmp, cache_sel, cache_win, state_ssm, state_ssm_conv,
           state_rwkv, state_rwkv_shift, page_table, w_ada, b_ada, w_in, ssd_conv_w, ssd_conv_b, ssd_dt_bias,
           ssd_a_log, ssd_d, ssd_norm_w, rwkv_mu, rwkv_w0, rwkv_w2, rwkv_a0, rwkv_a2, rwkv_g2, rwkv_k_k, rwkv_k_a,
           rwkv_r_k, rwkv_lnx_w, rwkv_lnx_b, cmp_pe, cmp_w1, cmp_w2, w_o_ssd, w_o_rwkv, w_o_nsa, w_out, ln1_g,
           ln1_b, ln2_g, ln2_b, w_group, w_router, moe_w1, moe_w3, moe_w2):
    p = dict(w_in=w_in, ssd_conv_w=ssd_conv_w, ssd_conv_b=ssd_conv_b, ssd_dt_bias=ssd_dt_bias, ssd_a_log=ssd_a_log,
             ssd_d=ssd_d, ssd_norm_w=ssd_norm_w, rwkv_mu=rwkv_mu, rwkv_w0=rwkv_w0, rwkv_w2=rwkv_w2, rwkv_a0=rwkv_a0,
             rwkv_a2=rwkv_a2, rwkv_g2=rwkv_g2, rwkv_k_k=rwkv_k_k, rwkv_k_a=rwkv_k_a, rwkv_r_k=rwkv_r_k,
             rwkv_lnx_w=rwkv_lnx_w, rwkv_lnx_b=rwkv_lnx_b, cmp_pe=cmp_pe, cmp_w1=cmp_w1, cmp_w2=cmp_w2,
             w_o_ssd=w_o_ssd, w_o_rwkv=w_o_rwkv, w_o_nsa=w_o_nsa, w_out=w_out, ln1_g=ln1_g, ln1_b=ln1_b,
             ln2_g=ln2_g, ln2_b=ln2_b, w_group=w_group, w_router=w_router, moe_w1=moe_w1, moe_w3=moe_w3,
             moe_w2=moe_w2)
    bp, bs = x_prompt.shape[0], x_sample.shape[0]
    past_len = page_table.shape[1] * PAGE
    nb = -(-(bp + bs) // SUBLANE) * SUBLANE
    c_all = jnp.pad(jnp.concatenate([c_prompt, c_sample], axis=0), ((0, nb - bp - bs), (0, 0)))
    mod = ada_mod(c_all, w_ada, b_ada)
    n_phys = cache_cmp.shape[1]
    cmp_pages = cache_cmp.reshape(DEPTH, n_phys, PAGE, 2 * LANE)
    sel_pages = cache_sel.reshape(DEPTH, n_phys, PAGE, 2 * LANE).astype(BF16)
    zeros = lambda *s: jnp.zeros(s, F32)
    xp, xs = x_prompt, x_sample
    st_p, st_s = [], []
    for l in range(DEPTH):
        w = _layer_weights(l, p)
        xp, sp_l = _trunk_layer(xp, mod[l, :bp], l, w, 0, zeros(bp, SSD_CONV - 1, SSD_CONV_DIM),
                                zeros(bp, SSD_HEADS, SSD_HD, SSD_STATE), zeros(bp, RWKV_COLS),
                                zeros(bp, RWKV_HEADS, RWKV_HD, RWKV_HD), None, None, None, None)
        xs, ss_l = _trunk_layer(xs, mod[l, bp:bp + bs], l, w, past_len, state_ssm_conv[l], state_ssm[l],
                                state_rwkv_shift[l], state_rwkv[l], cache_win[l], cmp_pages, sel_pages, page_table)
        st_p.append(sp_l)
        st_s.append(ss_l)
    sp = [jnp.stack(z) for z in zip(*st_p)]
    ss = [jnp.stack(z) for z in zip(*st_s)]
    return (xp, xs, sp[0], sp[1], sp[2], sp[3], sp[4], sp[5], sp[6], ss[0], ss[1], ss[2], ss[3], ss[4], ss[5], ss[6])
```

```python
import functools
import math

import jax
import jax.numpy as jnp
from jax import lax
from jax.experimental import pallas as pl
from jax.experimental.pallas import tpu as pltpu

F32 = jnp.float32
BF16 = jnp.bfloat16
HIGHEST = lax.Precision.HIGHEST

D_MODEL = 1024
DEPTH = 2
PAGE = 128
SSD_HEADS, SSD_HD, SSD_GROUPS, SSD_STATE, SSD_CONV = 16, 64, 2, 128, 4
SSD_INNER = SSD_HEADS * SSD_HD
SSD_CONV_DIM = SSD_INNER + 2 * SSD_GROUPS * SSD_STATE
SSD_COLS = SSD_INNER + SSD_CONV_DIM + SSD_HEADS
RWKV_HEADS, RWKV_HD = 16, 64
RWKV_DIM = RWKV_HEADS * RWKV_HD
RWKV_COLS = 3 * RWKV_DIM + 64 + 64 + 128
RWKV_LNX_EPS = 64e-5
NSA_HEADS, NSA_KVH, NSA_HPG, NSA_HD = 16, 2, 8, 64
NSA_DIM = NSA_HEADS * NSA_HD
NSA_COLS = NSA_DIM + 3 * 2 * NSA_KVH * NSA_HD + 3 * NSA_HEADS
CMP_BLOCK, CMP_STRIDE, CMP_HIDDEN = 32, 16, 128
SEL_BLOCK, N_SEL, WINDOW = 64, 16, 512
FORCE_BONUS = 1e4
N_GROUPS, EPG, N_EXPERTS, EXPERT_HIDDEN = 4, 4, 16, 256
ALPHA = (2 * DEPTH) ** 0.25
LN_EPS = 1e-5
RMS_EPS = 1e-5
NEG = -1e30

LANE = 128
SUBLANE = 8
VMEM_LIMIT = 56 * 1024 * 1024


def _cparams(sem):
    return pltpu.CompilerParams(dimension_semantics=sem, vmem_limit_bytes=VMEM_LIMIT)


def _dot(a, b):
    return jnp.dot(a.astype(BF16), b.astype(BF16), preferred_element_type=F32)


def _dot_hi(a, b):
    return jnp.dot(a, b, precision=HIGHEST, preferred_element_type=F32)


def _dot_nt(a, b):
    return lax.dot_general(a.astype(BF16), b.astype(BF16), (((1,), (1,)), ((), ())),
                           preferred_element_type=F32)


def _dot_nt_hi(a, b):
    return lax.dot_general(a, b, (((1,), (1,)), ((), ())), precision=HIGHEST,
                           preferred_element_type=F32)


def _split16(a):
    hi = a.astype(BF16)
    return hi, (a - hi.astype(F32)).astype(BF16)


def _split24(a):
    h1 = a.astype(BF16)
    r1 = a - h1.astype(F32)
    h2 = r1.astype(BF16)
    return h1, h2, (r1 - h2.astype(F32)).astype(BF16)


def _dot_w01(x, w16):
    return sum(_dot(piece, w16) for piece in _split24(x))


def _dot3(a, b):
    (ah, al), (bh, bl) = a, b
    return _dot(jnp.concatenate([ah, ah, al], axis=1), jnp.concatenate([bh, bl, bh], axis=0))


def _dot3_nt(a, b):
    (ah, al), (bh, bl) = a, b
    return _dot_nt(ah, bh) + _dot_nt(ah, bl) + _dot_nt(al, bh)


def _sigmoid(x):
    return 1.0 / (1.0 + jnp.exp(-x))


def _silu(x):
    return x * _sigmoid(x)


def _softplus(x):
    return jnp.maximum(x, 0.0) + jnp.log(1.0 + jnp.exp(-jnp.abs(x)))


def _iota(shape, dim):
    return lax.broadcasted_iota(jnp.int32, shape, dim)


def _head_onehot(n_heads, hd, pad_rows):
    r = jnp.arange(pad_rows)[:, None]
    c = jnp.arange(n_heads * hd)[None, :] // hd
    return (r == c).astype(F32)


def _ada_kernel(c_ref, w_ref, b_ref, o_ref):
    o_ref[...] = _dot_hi(_silu(c_ref[...]), w_ref[...]) + b_ref[...]


def ada_mod(c_all, w_ada, b_ada):
    nb = c_all.shape[0]
    return pl.pallas_call(
        _ada_kernel,
        grid=(DEPTH, 6),
        in_specs=[pl.BlockSpec((nb, D_MODEL), lambda l, j: (0, 0)),
                  pl.BlockSpec((None, D_MODEL, D_MODEL), lambda l, j: (l, 0, j)),
                  pl.BlockSpec((None, 1, D_MODEL), lambda l, j: (l, 0, j))],
        out_specs=pl.BlockSpec((None, nb, D_MODEL), lambda l, j: (l, 0, j)),
        out_shape=jax.ShapeDtypeStruct((DEPTH, nb, 6 * D_MODEL), F32),
        compiler_params=_cparams(("arbitrary", "arbitrary")),
        name="ada_mod",
    )(c_all, w_ada, b_ada.reshape(DEPTH, 1, 6 * D_MODEL))


def _inproj_kernel(x_ref, sc_ref, sh_ref, w_ref, o_ref):
    u = x_ref[...] * (1.0 + sc_ref[...]) + sh_ref[...]
    o_ref[...] = _dot(u, w_ref[...])


def mod_proj(x, sc, sh, w, tm=256):
    m, k = x.shape
    n = w.shape[1]
    tm = min(tm, m)
    per_row = sc.shape[0] != 1
    mspec = (pl.BlockSpec((tm, k), lambda i: (i, 0)) if per_row
             else pl.BlockSpec((1, k), lambda i: (0, 0)))
    return pl.pallas_call(
        _inproj_kernel,
        grid=(m // tm,),
        in_specs=[pl.BlockSpec((tm, k), lambda i: (i, 0)), mspec, mspec,
                  pl.BlockSpec((k, n), lambda i: (0, 0))],
        out_specs=pl.BlockSpec((tm, n), lambda i: (i, 0)),
        out_shape=jax.ShapeDtypeStruct((m, n), F32),
        compiler_params=_cparams(("arbitrary",)),
        name="mod_proj",
    )(x, sc, sh, w)


SSD_Q = 128
SSD_W = SSD_INNER + SSD_CONV_DIM + LANE


def _ssd_kernel(n_valid, zxd_ref, cst_ref, h0_ref, cw_ref, cb_ref, dtb_ref, alog_ref, dexp_ref,
                nw_ref, e16_ref, tri_ref, y_ref, hout_ref, ext, hT):
    j = pl.program_id(1)
    q = SSD_Q

    @pl.when(j == 0)
    def _():
        ext[0:8, :] = cst_ref[...]
        hT[...] = h0_ref[...]

    ext[8:8 + q, :] = zxd_ref[:, SSD_INNER:SSD_INNER + SSD_CONV_DIM]
    conv = (cb_ref[...] + ext[5:5 + q, :] * cw_ref[0:1, :] + ext[6:6 + q, :] * cw_ref[1:2, :]
            + ext[7:7 + q, :] * cw_ref[2:3, :] + ext[8:8 + q, :] * cw_ref[3:4, :])
    ext[0:8, :] = ext[q:q + 8, :]
    xbc = _silu(conv)
    xs = xbc[:, :SSD_INNER]
    bm = xbc[:, SSD_INNER:SSD_INNER + 2 * SSD_STATE]
    cm = xbc[:, SSD_INNER + 2 * SSD_STATE:]

    row = _iota((q, LANE), 0)
    lane = _iota((q, LANE), 1)
    dt = _softplus(zxd_ref[:, SSD_INNER + SSD_CONV_DIM:] + dtb_ref[...])
    dt = jnp.where((lane < SSD_HEADS) & (row + j * q < n_valid), dt, 0.0)
    a = -jnp.exp(alog_ref[...])
    acum = _dot_hi(tri_ref[...], dt * a)
    a_last = acum[q - 1:q, :]
    e16 = e16_ref[...]
    dt_e = _dot_hi(dt, e16)
    ea_e = _dot_hi(jnp.exp(acum), e16)
    dte_e = _dot_hi(jnp.exp(a_last - acum), e16)
    elast_e = ea_e[q - 1:q, :]
    xdt = xs * dt_e
    acum_t = acum.T
    tril = row >= lane
    lo_half = lane < SSD_HD

    y_tiles = []
    for g in range(SSD_GROUPS):
        cc = cm[:, g * SSD_STATE:(g + 1) * SSD_STATE]
        bc = bm[:, g * SSD_STATE:(g + 1) * SSD_STATE]
        cb = _dot_nt(cc, bc)
        gs = slice(g * 512, (g + 1) * 512)
        h_in = hT[:, gs]
        y_off = _dot(cc, h_in) * ea_e[:, gs]
        hT[:, gs] = h_in * elast_e[:, gs] + _dot(bc.T, xdt[:, gs] * dte_e[:, gs])
        for tl in range(4):
            t = 4 * g + tl
            xt = xdt[:, t * LANE:(t + 1) * LANE]
            yd = jnp.zeros((q, LANE), F32)
            for sub in range(2):
                h = 2 * t + sub
                diff = acum[:, h:h + 1] - acum_t[h:h + 1, :]
                m = cb * jnp.exp(jnp.where(tril, diff, NEG))
                yd = yd + _dot(m, jnp.where(lo_half if sub == 0 else ~lo_half, xt, 0.0))
            y_tiles.append(yd + y_off[:, tl * LANE:(tl + 1) * LANE])
    y = jnp.concatenate(y_tiles, axis=1) + dexp_ref[...] * xs
    y = y * _silu(zxd_ref[:, :SSD_INNER])
    outs = []
    for g in range(SSD_GROUPS):
        yg = y[:, g * 512:(g + 1) * 512]
        ms = jnp.sum(yg * yg, axis=-1, keepdims=True) * (1.0 / 512.0)
        outs.append(yg * lax.rsqrt(ms + RMS_EPS))
    y_ref[...] = jnp.concatenate(outs, axis=1) * nw_ref[...]

    @pl.when(j == pl.num_programs(1) - 1)
    def _():
        hout_ref[...] = hT[...]


def ssd_mixer(zxd, conv_st8, h0t, n_valid, cw8, cb, dtb, alog, dexp, nw):
    b, lp, _ = zxd.shape
    nj = lp // SSD_Q
    e16 = _head_onehot(SSD_HEADS, SSD_HD, LANE)
    tri = (jnp.arange(SSD_Q)[:, None] >= jnp.arange(SSD_Q)[None, :]).astype(F32)
    full = lambda shp: pl.BlockSpec(shp, lambda bi, j: (0,) * len(shp))
    return pl.pallas_call(
        functools.partial(_ssd_kernel, n_valid),
        grid=(b, nj),
        in_specs=[pl.BlockSpec((None, SSD_Q, SSD_W), lambda bi, j: (bi, j, 0)),
                  pl.BlockSpec((None, 8, SSD_CONV_DIM), lambda bi, j: (bi, 0, 0)),
                  pl.BlockSpec((None, SSD_STATE, SSD_INNER), lambda bi, j: (bi, 0, 0)),
                  full((8, SSD_CONV_DIM)), full((1, SSD_CONV_DIM)), full((1, LANE)), full((1, LANE)),
                  full((1, SSD_INNER)), full((1, SSD_INNER)), full((LANE, SSD_INNER)),
                  full((SSD_Q, SSD_Q))],
        out_specs=[pl.BlockSpec((None, SSD_Q, SSD_INNER), lambda bi, j: (bi, j, 0)),
                   pl.BlockSpec((None, SSD_STATE, SSD_INNER), lambda bi, j: (bi, 0, 0))],
        out_shape=[jax.ShapeDtypeStruct((b, lp, SSD_INNER), F32),
                   jax.ShapeDtypeStruct((b, SSD_STATE, SSD_INNER), F32)],
        scratch_shapes=[pltpu.VMEM((SSD_Q + 8, SSD_CONV_DIM), F32),
                        pltpu.VMEM((SSD_STATE, SSD_INNER), F32)],
        compiler_params=_cparams(("arbitrary", "arbitrary")),
        name="ssd_mixer",
    )(zxd, conv_st8, h0t, cw8, cb, dtb, alog, dexp, nw, e16, tri)


def _rwkv_prep_kernel(n_valid, tr, x_ref, sh8_ref, mu_ref, w0_ref, a0_ref, kk_ref, ka_ref, rk_ref,
                      w2_ref, a2_ref, g2_ref, seg_ref, e16_ref,
                      r_o, ld_o, k_o, v_o, kk_o, bb_o, g_o, bv_o, ext):
    j = pl.program_id(1)

    @pl.when(j == 0)
    def _():
        ext[0:8, :] = sh8_ref[...]

    x = x_ref[...]
    ext[8:8 + tr, :] = x
    prev = ext[7:7 + tr, :]
    ext[0:8, :] = ext[tr:tr + 8, :]
    xm = x + (prev - x) * mu_ref[...]
    d = RWKV_DIM
    r, k, v = xm[:, :d], xm[:, d:2 * d], xm[:, 2 * d:3 * d]
    lo = xm[:, 3 * d:3 * d + LANE]
    glo = xm[:, 3 * d + LANE:]
    w = w0_ref[...] + _dot3(_split16(jnp.tanh(lo)), _split16(w2_ref[...]))
    ld = -jnp.exp(-_softplus(-w) - 0.5)
    a = _sigmoid(a0_ref[...] + _dot3(_split16(lo), _split16(a2_ref[...])))
    g = _dot(_sigmoid(glo), g2_ref[...])
    seg, e16 = seg_ref[...].astype(BF16), e16_ref[...].astype(BF16)
    kk = k * kk_ref[...]
    ss = _dot_w01(kk * kk, seg)
    kk = kk * _dot_w01(lax.rsqrt(jnp.maximum(ss, 1e-24)), e16)
    k2 = k * (1.0 + (a - 1.0) * ka_ref[...])
    bonus = _dot_w01(_dot_w01(r * k2 * rk_ref[...], seg), e16)
    valid = (_iota((tr, 1), 0) + j * tr) < n_valid
    zero = lambda t: jnp.where(valid, t, 0.0)
    r_o[...] = r
    ld_o[...] = zero(ld)
    k_o[...] = zero(k2)
    v_o[...] = zero(v)
    kk_o[...] = zero(kk)
    bb_o[...] = zero(kk * a)
    g_o[...] = g
    bv_o[...] = bonus * v


def rwkv_prep(cols, shift8, n_valid, prm):
    b, lp, _ = cols.shape
    tr = min(128, lp)
    d = RWKV_DIM
    full = lambda shp: pl.BlockSpec(shp, lambda bi, j: (0,) * len(shp))
    row = lambda: pl.BlockSpec((None, tr, d), lambda bi, j: (bi, j, 0))
    return pl.pallas_call(
        functools.partial(_rwkv_prep_kernel, n_valid, tr),
        grid=(b, lp // tr),
        in_specs=[pl.BlockSpec((None, tr, RWKV_COLS), lambda bi, j: (bi, j, 0)),
                  pl.BlockSpec((None, 8, RWKV_COLS), lambda bi, j: (bi, 0, 0)),
                  full((1, RWKV_COLS))] + [full((1, d))] * 5 + [full((LANE, d))] * 3
                 + [full((d, LANE)), full((LANE, d))],
        out_specs=[row() for _ in range(8)],
        out_shape=[jax.ShapeDtypeStruct((b, lp, d), F32)] * 8,
        scratch_shapes=[pltpu.VMEM((tr + 8, RWKV_COLS), F32)],
        compiler_params=_cparams(("arbitrary", "arbitrary")),
        name="rwkv_prep",
    )(cols, shift8, *prm)


RWKV_C = 64


def _rwkv_scan_kernel(nch, r_ref, ld_ref, k_ref, v_ref, kk_ref, bb_ref, g_ref, bv_ref, s0_ref,
                      lnw_ref, lnb_ref, y_ref, sout_ref, st):
    i = pl.program_id(2)
    c = RWKV_C
    npp = st.shape[0]

    @pl.when(i == 0)
    def _():
        st[...] = s0_ref[...]

    n2 = 2 * c
    rr = _iota((n2, n2), 0)
    cc = _iota((n2, n2), 1)
    eye = (rr == cc).astype(F32)
    upper = rr < cc
    upper_eq = rr <= cc
    tri = (_iota((c, c), 0) >= _iota((c, c), 1)).astype(BF16)
    lane = _iota((c, LANE), 1)
    m0 = lane < RWKV_HD
    hmean = jnp.where((rr < RWKV_HD) == (cc < RWKV_HD), 1.0 / RWKV_HD, 0.0).astype(BF16)

    def stack(t):
        return jnp.concatenate([jnp.where(m0, t, 0.0), jnp.where(m0, 0.0, t)], axis=0)

    def head_mean(t):
        return sum(_dot(piece, hmean) for piece in _split24(t))

    units = [(ci, pi) for ci in range(nch) for pi in range(npp)]
    pre = []
    for ci, pi in units:
        sl = pl.ds(ci * c, c)
        ls = slice(pi * LANE, (pi + 1) * LANE)
        r, ld, k, v, kk, bb = (ref[sl, ls] for ref in (r_ref, ld_ref, k_ref, v_ref, kk_ref, bb_ref))
        cum = sum(_dot(tri, piece) for piece in _split24(ld))
        p_in = jnp.exp(cum)
        p_inv = jnp.exp(-cum)
        p_c = p_in[c - 1:c, :]
        kks = stack(kk * jnp.exp(cum - ld))
        rs = stack(r * p_in)
        bs = stack(bb * p_inv)
        ks = stack(k * p_inv)
        kks16, rs16, bs16, ks16 = (t.astype(BF16) for t in (kks, rs, bs, ks))
        pre.append(dict(
            p_c=p_c, vs_t=_split16(stack(v).T), kks_t=kks.T.astype(BF16), rs_t=rs.T.astype(BF16),
            upd_r=_split16(jnp.concatenate([bs * p_c, ks * p_c], axis=0)),
            nt=jnp.where(upper, _dot_nt(bs16, kks16), 0.0),
            avk=jnp.where(upper, _dot_nt(ks16, kks16), 0.0).astype(BF16),
            arb=jnp.where(upper_eq, _dot_nt(bs16, rs16), 0.0).astype(BF16),
            ark=jnp.where(upper_eq, _dot_nt(ks16, rs16), 0.0).astype(BF16)))
    tinv = [eye - u['nt'] for u in pre]
    pw = [u['nt'].astype(BF16) for u in pre]
    pw = [_dot(w, w).astype(BF16) for w in pw]
    for _ in range(4):
        both = [_dot(jnp.concatenate([w, t.astype(BF16)], axis=0), w) for w, t in zip(pw, tinv)]
        pw = [b[:n2, :].astype(BF16) for b in both]
        tinv = [t + b[n2:, :] for t, b in zip(tinv, both)]
    tinv = [_split16(t + _dot(t, w)) for t, w in zip(tinv, pw)]

    for ci in range(nch):
        sl = pl.ds(ci * c, c)
        for pi in range(npp):
            ls = slice(pi * LANE, (pi + 1) * LANE)
            u = pre[ci * npp + pi]
            s = st[pi]
            s16 = s.astype(BF16)
            vs_t16 = u['vs_t'][0]
            ut = _dot3(_split16(_dot(s16, u['kks_t']) + _dot(vs_t16, u['avk'])), tinv[ci * npp + pi])
            ut_p = _split16(ut)
            yt = _dot(s16, u['rs_t']) - _dot(ut_p[0], u['arb']) + _dot(vs_t16, u['ark'])
            upd_l = tuple(jnp.concatenate([-a, b], axis=1) for a, b in zip(ut_p, u['vs_t']))
            st[pi] = s * u['p_c'] + _dot3(upd_l, u['upd_r'])
            ys = yt.T
            y = ys[:c, :] + ys[c:, :]
            yc = y - head_mean(y)
            var = head_mean(yc * yc)
            y = yc * lax.rsqrt(var + RWKV_LNX_EPS) * lnw_ref[:, ls] + lnb_ref[:, ls] + bv_ref[sl, ls]
            y_ref[sl, ls] = y * g_ref[sl, ls]

    @pl.when(i == pl.num_programs(2) - 1)
    def _():
        sout_ref[...] = st[...]


def rwkv_scan(r, ld, k, v, kk, bb, g, bv, s0p, lnw, lnb):
    b, lp, d = r.shape
    npair = d // LANE
    nch = min(8, lp // RWKV_C)
    npp = min(npair, 16 // nch)
    rb = nch * RWKV_C
    row = lambda: pl.BlockSpec((None, rb, npp * LANE), lambda bi, p, i: (bi, i, p))
    st = lambda: pl.BlockSpec((None, npp, LANE, LANE), lambda bi, p, i: (bi, p, 0, 0))
    vec = lambda: pl.BlockSpec((1, npp * LANE), lambda bi, p, i: (0, p))
    return pl.pallas_call(
        functools.partial(_rwkv_scan_kernel, nch),
        grid=(b, npair // npp, lp // rb),
        in_specs=[row() for _ in range(8)] + [st(), vec(), vec()],
        out_specs=[row(), st()],
        out_shape=[jax.ShapeDtypeStruct((b, lp, d), F32),
                   jax.ShapeDtypeStruct((b, npair, LANE, LANE), F32)],
        scratch_shapes=[pltpu.VMEM((npp, LANE, LANE), F32)],
        compiler_params=_cparams(("arbitrary", "arbitrary", "arbitrary")),
        name="rwkv_scan",
    )(r, ld, k, v, kk, bb, g, bv, s0p, lnw, lnb)


def _pair_blockdiag(s):
    b = s.shape[0]
    s = s.reshape(b, 8, 2, RWKV_HD, RWKV_HD)
    z = jnp.zeros_like(s[:, :, 0])
    top = jnp.concatenate([s[:, :, 0], z], axis=-1)
    bot = jnp.concatenate([z, s[:, :, 1]], axis=-1)
    return jnp.concatenate([top, bot], axis=-2)


def _pair_unblock(sp):
    b = sp.shape[0]
    return jnp.stack([sp[:, :, :RWKV_HD, :RWKV_HD], sp[:, :, RWKV_HD:, RWKV_HD:]], axis=2).reshape(
        b, RWKV_HEADS, RWKV_HD, RWKV_HD)


CMP_HW = 4 * NSA_KVH * CMP_HIDDEN
NSEG_PAGE = PAGE // CMP_STRIDE


def _cmp_h_rows_kernel(nsr, xk_ref, xv_ref, wk_ref, wv_ref, o_ref):
    half = CMP_HW // 2
    acc_k = jnp.zeros((nsr, half), F32)
    acc_v = jnp.zeros((nsr, half), F32)
    for s in range(CMP_STRIDE):
        rows = pl.ds(s, nsr, stride=CMP_STRIDE)
        acc_k = acc_k + _dot(xk_ref[rows, :], wk_ref[s])
        acc_v = acc_v + _dot(xv_ref[rows, :], wv_ref[s])
    o_ref[...] = jnp.concatenate([acc_k, acc_v], axis=1)


def cmp_h_rows(rows, wk, wv, lt=0):
    b, t, _ = rows.shape
    tb = 2048 if t % 2048 == 0 else t
    nsr = tb // CMP_STRIDE
    wspec = pl.BlockSpec((CMP_STRIDE, LANE, CMP_HW // 2), lambda bi, i: (0, 0, 0))
    return pl.pallas_call(
        functools.partial(_cmp_h_rows_kernel, nsr),
        grid=(b, t // tb),
        in_specs=[pl.BlockSpec((None, tb, LANE), lambda bi, i: (bi, i, lt)),
                  pl.BlockSpec((None, tb, LANE), lambda bi, i: (bi, i, lt + 1)), wspec, wspec],
        out_specs=pl.BlockSpec((None, nsr, CMP_HW), lambda bi, i: (bi, i, 0)),
        out_shape=jax.ShapeDtypeStruct((b, t // CMP_STRIDE, CMP_HW), F32),
        compiler_params=_cparams(("arbitrary", "arbitrary")),
        name="cmp_h_rows",
    )(rows, rows, wk, wv)


PAGES_PER_STEP = 16
CMP_PAGES_PER_STEP = 32


def _cmp_h_pages_kernel(pt_ref, *refs):
    pp = CMP_PAGES_PER_STEP
    pk, pv = refs[:pp], refs[pp:2 * pp]
    wk_ref, wv_ref, o_ref = refs[2 * pp:]
    half = CMP_HW // 2
    acc_k = jnp.zeros((pp * NSEG_PAGE, half), F32)
    acc_v = jnp.zeros((pp * NSEG_PAGE, half), F32)
    for s in range(CMP_STRIDE):
        rows = pl.ds(s, NSEG_PAGE, stride=CMP_STRIDE)
        xk = jnp.concatenate([r[rows, :] for r in pk], axis=0)
        xv = jnp.concatenate([r[rows, :] for r in pv], axis=0)
        acc_k = acc_k + _dot(xk, wk_ref[s])
        acc_v = acc_v + _dot(xv, wv_ref[s])
    o_ref[...] = jnp.concatenate([acc_k, acc_v], axis=1)


def _page_specs(layer, lane_tile, n_pages, pp=PAGES_PER_STEP):
    def spec(p):
        def imap(bi, i, pt):
            return (layer, pt[bi, jnp.minimum(i * pp + p, n_pages - 1)], 0, lane_tile)
        return pl.BlockSpec((None, None, PAGE, LANE), imap)
    return [spec(p) for p in range(pp)]


def cmp_h_pages(cache, page_table, layer, wk, wv):
    b, n_pages = page_table.shape
    pp = CMP_PAGES_PER_STEP
    wspec = pl.BlockSpec((CMP_STRIDE, LANE, CMP_HW // 2), lambda bi, i, pt: (0, 0, 0))
    gs = pltpu.PrefetchScalarGridSpec(
        num_scalar_prefetch=1, grid=(b, n_pages // pp),
        in_specs=_page_specs(layer, 0, n_pages, pp) + _page_specs(layer, 1, n_pages, pp) + [wspec, wspec],
        out_specs=pl.BlockSpec((None, pp * NSEG_PAGE, CMP_HW), lambda bi, i, pt: (bi, i, 0)))
    return pl.pallas_call(
        _cmp_h_pages_kernel, grid_spec=gs,
        out_shape=jax.ShapeDtypeStruct((b, n_pages * NSEG_PAGE, CMP_HW), F32),
        compiler_params=_cparams(("arbitrary", "arbitrary")),
        name="cmp_h_pages",
    )(page_table, *([cache] * (2 * pp)), wk, wv)


NCP = 1024


def _cmp_finish_kernel(hn_row, hp_ref, hn_ref, pek_ref, pev_ref, wk_ref, wv_ref, w2k_ref, w2v_ref,
                       ckh_ref, ckl_ref, cvt_ref):
    qw = CMP_HW // 4
    rk = jnp.zeros((8, 2 * qw), F32)
    rv = jnp.zeros((8, 2 * qw), F32)
    for s in range(CMP_STRIDE):
        rk = rk + _dot(pek_ref[s], wk_ref[s])
        rv = rv + _dot(pev_ref[s], wv_ref[s])
    last = _iota((NCP, qw), 0) == hn_row

    def hidden(off, rb):
        h0 = hp_ref[:, off:off + qw]
        h1 = pltpu.roll(hp_ref[:, off + qw:off + 2 * qw], NCP - 1, 0)
        h1 = jnp.where(last, hn_ref[0:1, off + qw:off + 2 * qw], h1)
        return _silu(h0 + h1 + rb[0:1, :qw] + rb[1:2, qw:])

    ckh_ref[...], ckl_ref[...] = _split16(_dot(hidden(0, rk), w2k_ref[...]))
    cvt_ref[...] = _dot(hidden(2 * qw, rv), w2v_ref[...]).T.astype(BF16)


def cmp_finish(hp, hn, hn_row, pek, pev, wk, wv, w2k, w2v):
    b = hp.shape[0]
    full = lambda shp: pl.BlockSpec(shp, lambda bi: (0,) * len(shp))
    return pl.pallas_call(
        functools.partial(_cmp_finish_kernel, hn_row),
        grid=(b,),
        in_specs=[pl.BlockSpec((None, NCP, CMP_HW), lambda bi: (bi, 0, 0)),
                  pl.BlockSpec((None, 8, CMP_HW), lambda bi: (bi, 0, 0)),
                  full((CMP_STRIDE, 8, LANE)), full((CMP_STRIDE, 8, LANE)),
                  full((CMP_STRIDE, LANE, CMP_HW // 2)), full((CMP_STRIDE, LANE, CMP_HW // 2)),
                  full((CMP_HW // 4, LANE)), full((CMP_HW // 4, LANE))],
        out_specs=[pl.BlockSpec((None, NCP, LANE), lambda bi: (bi, 0, 0)),
                   pl.BlockSpec((None, NCP, LANE), lambda bi: (bi, 0, 0)),
                   pl.BlockSpec((None, LANE, NCP), lambda bi: (bi, 0, 0))],
        out_shape=[jax.ShapeDtypeStruct((b, NCP, LANE), BF16),
                   jax.ShapeDtypeStruct((b, NCP, LANE), BF16),
                   jax.ShapeDtypeStruct((b, LANE, NCP), BF16)],
        compiler_params=_cparams(("arbitrary",)),
        name="cmp_finish",
    )(hp, hn, pek, pev, wk, wv, w2k, w2v)


def _cmp_weights(pe, w1, w2):
    eye_g = jnp.eye(NSA_KVH, dtype=F32)

    def first(e):
        w = w1[e].reshape(2, CMP_STRIDE, NSA_HD, CMP_HIDDEN)
        w = jnp.einsum('isdf,gh->sgdihf', w, eye_g)
        return w.reshape(CMP_STRIDE, LANE, CMP_HW // 2).astype(BF16)

    def second(e):
        return jnp.einsum('fd,gh->gfhd', w2[e], eye_g).reshape(CMP_HW // 4, LANE).astype(BF16)

    def pos(e):
        p = pe[e].reshape(2, CMP_STRIDE, NSA_HD)
        p = jnp.tile(jnp.transpose(p, (1, 0, 2)), (1, 1, NSA_KVH))
        return jnp.pad(p, ((0, 0), (0, 6), (0, 0)))

    return pos(0), pos(1), first(0), first(1), second(0), second(1)


def _kvprep_rows_kernel(nsub, tk, xk_ref, xv_ref, k_ref, vt_ref):
    k_ref[...] = xk_ref[...].astype(BF16)
    for u in range(nsub):
        vt_ref[u] = xv_ref[u * tk:(u + 1) * tk, :].T.astype(BF16)


def kvprep_rows(rows, tk, lt=0):
    b, t, _ = rows.shape
    tt = 512 if t % 512 == 0 else tk
    nsub = tt // tk
    return pl.pallas_call(
        functools.partial(_kvprep_rows_kernel, nsub, tk),
        grid=(b, t // tt),
        in_specs=[pl.BlockSpec((None, tt, LANE), lambda bi, i: (bi, i, lt)),
                  pl.BlockSpec((None, tt, LANE), lambda bi, i: (bi, i, lt + 1))],
        out_specs=[pl.BlockSpec((None, tt, LANE), lambda bi, i: (bi, i, 0)),
                   pl.BlockSpec((None, nsub, LANE, tk), lambda bi, i: (bi, i, 0, 0))],
        out_shape=[jax.ShapeDtypeStruct((b, t, LANE), BF16),
                   jax.ShapeDtypeStruct((b, t // tk, LANE, tk), BF16)],
        compiler_params=_cparams(("arbitrary", "arbitrary")),
        name="kvprep_rows",
    )(rows, rows)


SEL_TK = 512
WIN_TK = 128


def _kvprep_pages_kernel(n_steps, pt_ref, *refs):
    pp = PAGES_PER_STEP
    pk, pv = refs[:pp], refs[pp:2 * pp]
    nk_ref, nv_ref, k_ref, vt_ref = refs[2 * pp:]
    i = pl.program_id(1)
    per = SEL_TK // PAGE

    @pl.when(i < n_steps - 1)
    def _():
        for p in range(pp):
            k_ref[p * PAGE:(p + 1) * PAGE, :] = pk[p][...].astype(BF16)
            vt_ref[p // per, :, (p % per) * PAGE:(p % per + 1) * PAGE] = pv[p][...].T.astype(BF16)

    @pl.when(i == n_steps - 1)
    def _():
        k_ref[...] = jnp.zeros(k_ref.shape, BF16)
        vt_ref[...] = jnp.zeros(vt_ref.shape, BF16)
        k_ref[0:PAGE, :] = nk_ref[...].astype(BF16)
        vt_ref[0, :, 0:PAGE] = nv_ref[...].T.astype(BF16)


def kvprep_pages(cache, page_table, layer, new_rows, lt0=0):
    b, n_pages = page_table.shape
    pp = PAGES_PER_STEP
    n_steps = n_pages // pp + 1
    t = n_steps * pp * PAGE
    new = lambda lt: pl.BlockSpec((None, PAGE, LANE), lambda bi, i, pt: (bi, 0, lt0 + lt))
    gs = pltpu.PrefetchScalarGridSpec(
        num_scalar_prefetch=1, grid=(b, n_steps),
        in_specs=_page_specs(layer, 0, n_pages) + _page_specs(layer, 1, n_pages) + [new(0), new(1)],
        out_specs=[pl.BlockSpec((None, pp * PAGE, LANE), lambda bi, i, pt: (bi, i, 0)),
                   pl.BlockSpec((None, pp * PAGE // SEL_TK, LANE, SEL_TK), lambda bi, i, pt: (bi, i, 0, 0))])
    return pl.pallas_call(
        functools.partial(_kvprep_pages_kernel, n_steps), grid_spec=gs,
        out_shape=[jax.ShapeDtypeStruct((b, t, LANE), BF16),
                   jax.ShapeDtypeStruct((b, t // SEL_TK, LANE, SEL_TK), BF16)],
        compiler_params=_cparams(("arbitrary", "arbitrary")),
        name="kvprep_pages",
    )(page_table, *([cache] * (2 * pp)), new_rows, new_rows)


TQ = 128
NQL = NSA_HEADS * TQ
NSA_CMP_TILE, NSA_SEL_TILE, NSA_WIN_TILE, NSA_GL_TILE = 8, 10, 12, 14
NSA_W = (NSA_GL_TILE + 1) * LANE
WIN_TILES = WINDOW // WIN_TK + 1


def _rowmax(x):
    return jnp.max(x, axis=0, keepdims=True)


def _rowsum(x):
    return jnp.sum(x, axis=0, keepdims=True)


LOG2E = 1.4426950408889634
SEL_PER = SEL_TK // SEL_BLOCK


def _real(m):
    return jnp.where(m > 0.5 * NEG, m, 0.0)


def _nsa_attn_kernel(pos0, nc, ns, wt0, wpos0, n_win,
                     q_ref, gl_ref, ckh_ref, ckl_ref, cvt_ref, ks_ref, vst_ref, *rest):
    kw = rest[:WIN_TILES]
    vw = rest[WIN_TILES:2 * WIN_TILES]
    msel_ref, o_ref, sel_scr, work_scr, sel3_scr, imp_scr, oc_scr, acc_scr, ml_scr, ot_scr = rest[2 * WIN_TILES:]
    i = pl.program_id(1)
    t0 = pos0 + i * TQ
    nsp = sel_scr.shape[0]
    hd = NSA_HD

    qt = (q_ref[...] * (hd ** -0.5 * LOG2E)).T
    zero = jnp.zeros((hd, TQ), F32)
    pieces = []
    for jh in range(NSA_HEADS):
        blk = qt[jh * hd:(jh + 1) * hd, :]
        pieces.append(jnp.concatenate([blk, zero] if jh < NSA_HPG else [zero, blk], axis=0))
    qbd = jnp.concatenate(pieces, axis=1)
    qbd16 = qbd.astype(BF16)
    t_q = t0 + _iota((1, TQ), 1)
    gw = NSA_HPG * TQ

    def heads(x, n=NSA_HPG):
        return jnp.concatenate([x] * n, axis=1)

    c_idx = _iota((NCP, TQ), 0)
    cbias = jnp.where((CMP_STRIDE * c_idx + (CMP_BLOCK - 1) <= t_q) & (c_idx < nc), 0.0, NEG)
    for g in range(NSA_KVH):
        gs = slice(g * gw, (g + 1) * gw)
        qh, ql = _split16(qbd[:, gs])
        ckh = ckh_ref[...]
        s = _dot(ckh, qh) + _dot(ckh, ql) + _dot(ckl_ref[...], qh) + heads(cbias)
        p = jnp.exp2(s - _real(_rowmax(s)))
        l = _rowsum(p)
        inv = 1.0 / jnp.where(l > 0.0, l, 1.0)
        oc_scr[g] = _dot(cvt_ref[g * hd:(g + 1) * hd, :], p) * inv
        pc = p * inv
        imp = pc[:, :TQ]
        for h in range(1, NSA_HPG):
            imp = imp + pc[:, h * TQ:(h + 1) * TQ]
        imp_scr[:, g * TQ:(g + 1) * TQ] = imp
    pslc = _dot_hi(msel_ref[...], imp_scr[...])

    jrow = _iota((nsp, 2 * TQ), 0)
    t_gq = t0 + (_iota((1, 2 * TQ), 1) % TQ)
    cur = lax.shift_right_logical(t_gq, 6)
    forced = (jrow == 0) | (jrow == cur) | (jrow == cur - 1)
    score = jnp.where(jrow * SEL_BLOCK <= t_gq, pslc + jnp.where(forced, FORCE_BONUS, 0.0), NEG)
    work_scr[...] = jnp.where(jrow < ns, score, -3e38)
    sel_scr[...] = jnp.zeros(sel_scr.shape, F32)
    jrow_f = jrow.astype(F32)

    def pick(_, carry):
        w = work_scr[...]
        best = _rowmax(w)
        first = jnp.min(jnp.where(w == best, jrow_f, 1e9), axis=0, keepdims=True)
        hit = jrow_f == first
        sel_scr[...] = jnp.where(hit, 1.0, sel_scr[...])
        work_scr[...] = jnp.where(hit, -jnp.inf, w)
        return carry

    lax.fori_loop(0, N_SEL, pick, 0)
    for u in range(nsp // SEL_PER):
        sel3_scr[u] = jnp.where(sel_scr[u * SEL_PER:(u + 1) * SEL_PER, :] > 0.5, 0.0, NEG)

    last = (t0 + TQ + SEL_TK - 1) // SEL_TK - 1
    ml_scr[0:1, :] = jnp.full((1, NQL), NEG, F32)
    ml_scr[1:2, :] = jnp.zeros((1, NQL), F32)
    acc_scr[...] = jnp.zeros(acc_scr.shape, F32)

    ones16 = jnp.ones((16, SEL_TK), BF16)

    def sel_tile(kt, extra):
        s = _dot(ks_ref[pl.ds(pl.multiple_of(kt * SEL_TK, SEL_TK), SEL_TK), :], qbd16).astype(BF16)
        sb = sel3_scr[kt].astype(BF16)
        s = jnp.concatenate(
            [s[jj * SEL_BLOCK:(jj + 1) * SEL_BLOCK, :]
             + jnp.concatenate([heads(sb[jj:jj + 1, :TQ]), heads(sb[jj:jj + 1, TQ:])], axis=1)
             for jj in range(SEL_PER)], axis=0)
        if extra is not None:
            s = s + extra
        m_old = ml_scr[0:1, :]
        m_new = jnp.maximum(m_old, _rowmax(s).astype(F32))
        alpha = jnp.exp2(m_old - m_new)
        p = jnp.exp2(s - m_new.astype(BF16))
        ml_scr[0:1, :] = m_new
        for g in range(NSA_KVH):
            gs = slice(g * gw, (g + 1) * gw)
            vt1 = jnp.concatenate([vst_ref[kt, g * hd:(g + 1) * hd, :], ones16], axis=0)
            pv = _dot(vt1, p[:, gs])
            acc_scr[g] = acc_scr[g] * alpha[:, gs] + pv[:hd, :]
            ml_scr[1:2, gs] = alpha[:, gs] * ml_scr[1:2, gs] + pv[hd:hd + 1, :]

    def sel_body(kt, carry):
        sel_tile(kt, None)
        return carry

    lax.fori_loop(0, last, sel_body, 0)
    krow = _iota((SEL_TK, TQ), 0)
    sel_tile(last, heads(jnp.where(last * SEL_TK + krow <= t_q, 0.0, NEG).astype(BF16), NSA_HEADS))
    l_sel = ml_scr[1:2, :]
    inv_sel = 1.0 / jnp.where(l_sel > 0.0, l_sel, 1.0)

    wrow = _iota((WIN_TK, TQ), 0)
    m = jnp.full((1, NQL), NEG, F32)
    l = jnp.zeros((1, NQL), F32)
    acc_w = [jnp.zeros((hd, gw), F32) for _ in range(NSA_KVH)]
    for w in range(WIN_TILES):
        tile = i + (wt0 - (WIN_TILES - 1) + w)
        idx = tile * WIN_TK + wrow
        wpos = wpos0 + idx
        dlt = t_q - wpos
        ok = (dlt >= 0) & (dlt < WINDOW) & (wpos >= 0) & (idx < n_win) & (tile >= 0)
        s = _dot(kw[w][...], qbd16) + heads(jnp.where(ok, 0.0, NEG), NSA_HEADS)
        m_new = jnp.maximum(m, _rowmax(s))
        alpha = jnp.where(m > 0.5 * NEG, jnp.exp2(m - _real(m_new)), 0.0)
        p = jnp.exp2(s - _real(m_new))
        l = alpha * l + _rowsum(p)
        p16 = p.astype(BF16)
        for g in range(NSA_KVH):
            gs = slice(g * gw, (g + 1) * gw)
            acc_w[g] = acc_w[g] * alpha[:, gs] + _dot(vw[w][g * hd:(g + 1) * hd, :], p16[:, gs])
        m = m_new
    inv_win = 1.0 / jnp.where(l > 0.0, l, 1.0)

    gt = _sigmoid(gl_ref[...]).T
    for jh in range(NSA_HEADS):
        g, h = divmod(jh, NSA_HPG)
        hl = slice(h * TQ, (h + 1) * TQ)
        ls = slice(jh * TQ, (jh + 1) * TQ)
        ot_scr[jh * hd:(jh + 1) * hd, :] = (
            gt[3 * jh:3 * jh + 1, :] * oc_scr[g, :, hl]
            + gt[3 * jh + 1:3 * jh + 2, :] * (acc_scr[g, :, hl] * inv_sel[:, ls])
            + gt[3 * jh + 2:3 * jh + 3, :] * (acc_w[g][:, hl] * inv_win[:, ls]))
    o_ref[...] = ot_scr[...].T


def nsa_attention(cols, ckh, ckl, cvt, ks, vst, kwin, vwt, pos0, n_new, n_buf, n_win):
    b, lq, _ = cols.shape
    q = gl = cols
    tk_total = ks.shape[1]
    n_sel_tiles = tk_total // SEL_TK
    nsp = tk_total // SEL_BLOCK
    t_total = pos0 + n_new
    nseg = -(-t_total // CMP_STRIDE)
    nc = nseg - CMP_BLOCK // CMP_STRIDE + 1
    ns = -(-t_total // SEL_BLOCK)
    wt0 = n_buf // WIN_TK
    n_wtiles = kwin.shape[1] // WIN_TK
    j = jnp.arange(nsp)[:, None]
    c = jnp.arange(NCP)[None, :]
    msel = ((c >= 4 * j - 1) & (c <= 4 * j + 3)).astype(F32)

    def wspec(w, vt):
        def imap(bi, i):
            tile = jnp.clip(i + (wt0 - (WIN_TILES - 1) + w), 0, n_wtiles - 1)
            return (bi, tile, 0, 0) if vt else (bi, tile, 0)
        return pl.BlockSpec((None, None, LANE, WIN_TK) if vt else (None, WIN_TK, LANE), imap)

    kern = functools.partial(_nsa_attn_kernel, pos0, nc, ns, wt0, pos0 - n_buf, n_win)
    return pl.pallas_call(
        kern,
        grid=(b, lq // TQ),
        in_specs=[pl.BlockSpec((None, TQ, NSA_DIM), lambda bi, i: (bi, i, 0)),
                  pl.BlockSpec((None, TQ, LANE), lambda bi, i: (bi, i, NSA_GL_TILE)),
                  pl.BlockSpec((None, NCP, LANE), lambda bi, i: (bi, 0, 0)),
                  pl.BlockSpec((None, NCP, LANE), lambda bi, i: (bi, 0, 0)),
                  pl.BlockSpec((None, LANE, NCP), lambda bi, i: (bi, 0, 0)),
                  pl.BlockSpec((None, tk_total, LANE), lambda bi, i: (bi, 0, 0)),
                  pl.BlockSpec((None, n_sel_tiles, LANE, SEL_TK), lambda bi, i: (bi, 0, 0, 0))]
                 + [wspec(w, False) for w in range(WIN_TILES)]
                 + [wspec(w, True) for w in range(WIN_TILES)]
                 + [pl.BlockSpec((nsp, NCP), lambda bi, i: (0, 0))],
        out_specs=pl.BlockSpec((None, TQ, NSA_DIM), lambda bi, i: (bi, i, 0)),
        out_shape=jax.ShapeDtypeStruct((b, lq, NSA_DIM), F32),
        scratch_shapes=[pltpu.VMEM((nsp, 2 * TQ), F32), pltpu.VMEM((nsp, 2 * TQ), F32),
                        pltpu.VMEM((nsp // SEL_PER, SEL_PER, 2 * TQ), F32),
                        pltpu.VMEM((NCP, 2 * TQ), F32),
                        pltpu.VMEM((NSA_KVH, NSA_HD, NSA_HPG * TQ), F32),
                        pltpu.VMEM((NSA_KVH, NSA_HD, NSA_HPG * TQ), F32),
                        pltpu.VMEM((8, NQL), F32), pltpu.VMEM((NSA_DIM, TQ), F32)],
        compiler_params=_cparams(("arbitrary", "arbitrary")),
        name="nsa_attention",
    )(q, gl, ckh, ckl, cvt, ks, vst, *([kwin] * WIN_TILES), *([vwt] * WIN_TILES), msel)


TQS = 8


def _nsa_attn_small_kernel(pos0, nc, ns, wpos0, n_win, q_ref, gl_ref, ckh_ref, ckl_ref, cvt_ref, ks_ref,
                           vst_ref, kw_ref, vw_ref, msel_ref, eh_ref, eht_ref, gx_ref, o_ref,
                           sel_scr, work_scr, sel3_scr):
    hd = NSA_HD
    nsp = sel_scr.shape[0]
    lane8 = _iota((TQS, LANE), 1)
    lane = _iota((1, LANE), 1)
    t_lane = pos0 + (lane % TQS)
    grp1 = lane >= NSA_HPG * TQS

    q = q_ref[...] * (hd ** -0.5 * LOG2E)
    pieces = []
    for jh in range(NSA_HEADS):
        tile = q[:, (jh // 2) * LANE:(jh // 2 + 1) * LANE]
        dst_hi = jh >= NSA_HPG
        if (jh % 2 == 1) != dst_hi:
            tile = pltpu.roll(tile, hd, 1)
        pieces.append(jnp.where((lane8 >= hd) if dst_hi else (lane8 < hd), tile, 0.0))
    z = jnp.concatenate(pieces, axis=0).T
    z16 = z.astype(BF16)
    zh, zl = _split16(z)

    def own_group(full):
        return jnp.where(grp1, full[hd:, :], full[:hd, :])

    c_idx = _iota((NCP, LANE), 0)
    ckh = ckh_ref[...]
    s = _dot(ckh, zh) + _dot(ckh, zl) + _dot(ckl_ref[...], zh)
    s = s + jnp.where((CMP_STRIDE * c_idx + (CMP_BLOCK - 1) <= t_lane) & (c_idx < nc), 0.0, NEG)
    p = jnp.exp2(s - _real(_rowmax(s)))
    l = _rowsum(p)
    inv = 1.0 / jnp.where(l > 0.0, l, 1.0)
    o_cmp = own_group(_dot(cvt_ref[...], p) * inv)
    pslc = _dot_hi(msel_ref[...], _dot_hi(p * inv, eht_ref[...]))

    jrow = _iota((nsp, LANE), 0)
    cur = lax.shift_right_logical(t_lane, 6)
    forced = (jrow == 0) | (jrow == cur) | (jrow == cur - 1)
    score = jnp.where(jrow * SEL_BLOCK <= t_lane, pslc + jnp.where(forced, FORCE_BONUS, 0.0), NEG)
    work_scr[...] = jnp.where(jrow < ns, score, -3e38)
    sel_scr[...] = jnp.zeros(sel_scr.shape, F32)
    jrow_f = jrow.astype(F32)

    def pick(_, carry):
        w = work_scr[...]
        best = _rowmax(w)
        first = jnp.min(jnp.where(w == best, jrow_f, 1e9), axis=0, keepdims=True)
        hit = jrow_f == first
        sel_scr[...] = jnp.where(hit, 1.0, sel_scr[...])
        work_scr[...] = jnp.where(hit, -jnp.inf, w)
        return carry

    lax.fori_loop(0, N_SEL, pick, 0)
    on = _dot(sel_scr[...], eh_ref[...])
    for u in range(nsp // SEL_PER):
        sel3_scr[u] = jnp.where(on[u * SEL_PER:(u + 1) * SEL_PER, :] > 0.5, 0.0, NEG)

    last = (pos0 + TQS + SEL_TK - 1) // SEL_TK - 1
    krow = _iota((SEL_TK, LANE), 0)

    def sel_tile(kt, carry, extra):
        m, l, acc = carry
        s = _dot(ks_ref[pl.ds(pl.multiple_of(kt * SEL_TK, SEL_TK), SEL_TK), :], z16)
        sb = sel3_scr[kt]
        s = jnp.concatenate([s[jj * SEL_BLOCK:(jj + 1) * SEL_BLOCK, :] + sb[jj:jj + 1, :]
                             for jj in range(SEL_PER)], axis=0)
        if extra is not None:
            s = s + extra
        m_new = jnp.maximum(m, _rowmax(s))
        alpha = jnp.exp2(m - m_new)
        p = jnp.exp2(s - m_new)
        return m_new, alpha * l + _rowsum(p), acc * alpha + _dot(vst_ref[kt], p)

    init = (jnp.full((1, LANE), NEG, F32), jnp.zeros((1, LANE), F32), jnp.zeros((2 * hd, LANE), F32))

    def two_tiles(u, carry):
        return sel_tile(2 * u, carry[0], None), sel_tile(2 * u + 1, carry[1], None)

    ca, cb = lax.fori_loop(0, last // 2, two_tiles, (init, init))
    if last % 2:
        ca = sel_tile(last - 1, ca, None)
    (ma, la, acca), (mb, lb, accb) = sel_tile(last, ca, jnp.where(last * SEL_TK + krow <= t_lane, 0.0, NEG)), cb
    m = jnp.maximum(ma, mb)
    fa, fb = jnp.exp2(ma - m), jnp.exp2(mb - m)
    l = la * fa + lb * fb
    o_sel = own_group((acca * fa + accb * fb) * (1.0 / jnp.where(l > 0.0, l, 1.0)))

    wrow = _iota((WIN_TK, LANE), 0)
    m = jnp.full((1, LANE), NEG, F32)
    l = jnp.zeros((1, LANE), F32)
    acc = jnp.zeros((2 * hd, LANE), F32)
    for w in range(kw_ref.shape[0] // WIN_TK):
        idx = w * WIN_TK + wrow
        wpos = wpos0 + idx
        dlt = t_lane - wpos
        ok = (dlt >= 0) & (dlt < WINDOW) & (wpos >= 0) & (idx < n_win)
        s = _dot(kw_ref[w * WIN_TK:(w + 1) * WIN_TK, :], z16) + jnp.where(ok, 0.0, NEG)
        m_new = jnp.maximum(m, _rowmax(s))
        alpha = jnp.where(m > 0.5 * NEG, jnp.exp2(m - _real(m_new)), 0.0)
        p = jnp.exp2(s - _real(m_new))
        l = alpha * l + _rowsum(p)
        acc = acc * alpha + _dot(vw_ref[w], p)
        m = m_new
    o_win = own_group(acc * (1.0 / jnp.where(l > 0.0, l, 1.0)))

    def to_rows(o):
        ot = jnp.concatenate([o, o], axis=0).T
        tiles = []
        for k in range(NSA_HEADS // 2):
            a = ot[(2 * k) * TQS:(2 * k + 1) * TQS, :]
            b = ot[(2 * k + 1) * TQS:(2 * k + 2) * TQS, :]
            tiles.append(jnp.where(lane8 < hd, a, b))
        return jnp.concatenate(tiles, axis=1)

    sg = _sigmoid(gl_ref[...])
    o_ref[...] = (_dot_hi(sg, gx_ref[0]) * to_rows(o_cmp) + _dot_hi(sg, gx_ref[1]) * to_rows(o_sel)
                  + _dot_hi(sg, gx_ref[2]) * to_rows(o_win))


def nsa_attention_small(cols, ckh, ckl, cvt, ks, vst, kwin, vwt, pos0, n_buf, n_win):
    b, lq, _ = cols.shape
    assert lq == TQS
    tk_total = ks.shape[1]
    nsp = tk_total // SEL_BLOCK
    t_total = pos0 + lq
    nc = -(-t_total // CMP_STRIDE) - CMP_BLOCK // CMP_STRIDE + 1
    ns = -(-t_total // SEL_BLOCK)
    j = jnp.arange(nsp)[:, None]
    c = jnp.arange(NCP)[None, :]
    msel = ((c >= 4 * j - 1) & (c <= 4 * j + 3)).astype(F32)
    n = jnp.arange(LANE)
    head, qi = n // TQS, n % TQS
    gq = (head // NSA_HPG) * TQS + qi
    eh = (jnp.arange(LANE)[:, None] == gq[None, :]).astype(F32)
    col = jnp.arange(NSA_DIM) // NSA_HD
    gx = jnp.stack([(jnp.arange(LANE)[:, None] == (3 * col + br)[None, :]).astype(F32) for br in range(3)])
    full = lambda a: pl.BlockSpec(a.shape, lambda bi: (0,) * a.ndim)
    per_b = lambda a: pl.BlockSpec((None,) + a.shape[1:], lambda bi: (bi,) + (0,) * (a.ndim - 1))
    kern = functools.partial(_nsa_attn_small_kernel, pos0, nc, ns, pos0 - n_buf, n_win)
    return pl.pallas_call(
        kern,
        grid=(b,),
        in_specs=[pl.BlockSpec((None, TQS, NSA_DIM), lambda bi: (bi, 0, 0)),
                  pl.BlockSpec((None, TQS, LANE), lambda bi: (bi, 0, NSA_GL_TILE)),
                  per_b(ckh), per_b(ckl), per_b(cvt), per_b(ks), per_b(vst), per_b(kwin), per_b(vwt),
                  full(msel), full(eh), full(eh), full(gx)],
        out_specs=pl.BlockSpec((None, TQS, NSA_DIM), lambda bi: (bi, 0, 0)),
        out_shape=jax.ShapeDtypeStruct((b, TQS, NSA_DIM), F32),
        scratch_shapes=[pltpu.VMEM((nsp, LANE), F32), pltpu.VMEM((nsp, LANE), F32),
                        pltpu.VMEM((nsp // SEL_PER, SEL_PER, LANE), F32)],
        compiler_params=_cparams(("arbitrary",)),
        name="nsa_attention_small",
    )(cols, cols, ckh, ckl, cvt, ks, vst, kwin, vwt, msel, eh, eh.T, gx)


def _layer_norm(v, g, b):
    mu = jnp.mean(v, axis=-1, keepdims=True)
    vc = v - mu
    var = jnp.mean(vc * vc, axis=-1, keepdims=True)
    return vc * lax.rsqrt(var + LN_EPS) * g + b


def _merge_kernel(x_ref, ya_ref, yb_ref, yc_ref, gate_ref, g1_ref, wa_ref, wb_ref, wc_ref, wo_ref,
                  lg_ref, lb_ref, o_ref):
    d = D_MODEL
    merged = (_sigmoid(gate_ref[:, :d]) * _dot(ya_ref[...], wa_ref[...])
              + _sigmoid(gate_ref[:, d:2 * d]) * _dot(yb_ref[...], wb_ref[...])
              + _sigmoid(gate_ref[:, 2 * d:]) * _dot(yc_ref[...], wc_ref[...]))
    o = _dot(merged, wo_ref[...])
    o_ref[...] = _layer_norm(ALPHA * x_ref[...] + g1_ref[...] * o, lg_ref[...], lb_ref[...])


def merge_out(x, ya, yb, yc, gate, g1, wa, wb, wc, wo, lg, lb, tm=256):
    m, d = x.shape
    tm = min(tm, m)
    per_row = g1.shape[0] != 1
    row = lambda w: pl.BlockSpec((tm, w), lambda i: (i, 0))
    mspec = row(d) if per_row else pl.BlockSpec((1, d), lambda i: (0, 0))
    wspec = pl.BlockSpec((d, d), lambda i: (0, 0))
    vspec = pl.BlockSpec((1, d), lambda i: (0, 0))
    return pl.pallas_call(
        _merge_kernel,
        grid=(m // tm,),
        in_specs=[row(d), row(d), row(d), row(d), row(3 * d), mspec, wspec, wspec, wspec, wspec, vspec, vspec],
        out_specs=row(d),
        out_shape=jax.ShapeDtypeStruct((m, d), F32),
        compiler_params=_cparams(("arbitrary",)),
        name="merge_out",
    )(x, ya, yb, yc, gate, g1, wa, wb, wc, wo, lg, lb)


def _lane_first(mask, lane_f):
    return jnp.min(jnp.where(mask, lane_f, 1e9), axis=-1, keepdims=True)


def _moe_kernel(x_ref, sc_ref, sh_ref, g2_ref, wr_ref, w1_ref, w3_ref, w2_ref, lg_ref, lb_ref,
                o_ref, u_scr, gate_scr, acc_scr):
    e = pl.program_id(1)
    tm = x_ref.shape[0]
    lane = _iota((tm, LANE), 1)
    lane_f = lane.astype(F32)

    @pl.when(e == 0)
    def _():
        u = x_ref[...] * (1.0 + sc_ref[...]) + sh_ref[...]
        u_scr[...] = u.astype(BF16)
        logits = _dot_hi(u, wr_ref[...])
        lg = jnp.where(lane < N_GROUPS, logits, -jnp.inf)
        gmax = jnp.max(lg, axis=-1, keepdims=True)
        gstar = _lane_first(lg == gmax, lane_f)
        pg = 1.0 / jnp.sum(jnp.exp(lg - gmax), axis=-1, keepdims=True)
        in_grp = (lane >= N_GROUPS) & (lane < N_GROUPS + N_EXPERTS) & (
            lax.shift_right_logical(lane - N_GROUPS, 2).astype(F32) == gstar)
        le = jnp.where(in_grp, logits, -jnp.inf)
        v1 = jnp.max(le, axis=-1, keepdims=True)
        i1 = _lane_first(le == v1, lane_f)
        le2 = jnp.where(lane_f == i1, -jnp.inf, le)
        v2 = jnp.max(le2, axis=-1, keepdims=True)
        i2 = _lane_first(le2 == v2, lane_f)
        e2 = jnp.exp(v2 - v1)
        den = 1.0 / (1.0 + e2)
        gate_scr[...] = jnp.where(lane_f == i1, den * pg, jnp.where(lane_f == i2, e2 * den * pg, 0.0))
        acc_scr[...] = jnp.zeros(acc_scr.shape, F32)

    u = u_scr[...]
    gate = gate_scr[...]
    out = acc_scr[...]
    for k in range(MOE_EPS):
        ge = jnp.sum(jnp.where(lane == e * MOE_EPS + (k + N_GROUPS), gate, 0.0), axis=-1, keepdims=True)
        h = _silu(_dot(u, w1_ref[k])) * _dot(u, w3_ref[k])
        out = out + _dot(h * ge, w2_ref[k])
    acc_scr[...] = out

    @pl.when(e == pl.num_programs(1) - 1)
    def _():
        o_ref[...] = _layer_norm(ALPHA * x_ref[...] + g2_ref[...] * acc_scr[...], lg_ref[...], lb_ref[...])


MOE_EPS = 1


def moe_out(x, sc, sh, g2, wr, w1, w3, w2, lg, lb, tm=512):
    m, d = x.shape
    tm = min(tm, m)
    per_row = sc.shape[0] != 1
    row = pl.BlockSpec((tm, d), lambda i, e: (i, 0))
    mspec = row if per_row else pl.BlockSpec((1, d), lambda i, e: (0, 0))
    vspec = pl.BlockSpec((1, d), lambda i, e: (0, 0))
    return pl.pallas_call(
        _moe_kernel,
        grid=(m // tm, N_EXPERTS // MOE_EPS),
        in_specs=[row, mspec, mspec, mspec, pl.BlockSpec((d, LANE), lambda i, e: (0, 0)),
                  pl.BlockSpec((MOE_EPS, d, EXPERT_HIDDEN), lambda i, e: (e, 0, 0)),
                  pl.BlockSpec((MOE_EPS, d, EXPERT_HIDDEN), lambda i, e: (e, 0, 0)),
                  pl.BlockSpec((MOE_EPS, EXPERT_HIDDEN, d), lambda i, e: (e, 0, 0)), vspec, vspec],
        out_specs=row,
        out_shape=jax.ShapeDtypeStruct((m, d), F32),
        scratch_shapes=[pltpu.VMEM((tm, d), BF16), pltpu.VMEM((tm, LANE), F32), pltpu.VMEM((tm, d), F32)],
        compiler_params=_cparams(("arbitrary", "arbitrary")),
        name="moe_out",
    )(x, sc, sh, g2, wr, w1, w3, w2, lg, lb)


def _pad_rows(a, n):
    return a if a.shape[1] == n else jnp.pad(a, ((0, 0), (0, n - a.shape[1]), (0, 0)))


def _layer_weights(l, p):
    w_in = p['w_in'][l]
    o1, o2, o3 = SSD_COLS, SSD_COLS + RWKV_COLS, SSD_COLS + RWKV_COLS + NSA_COLS
    padc = lambda w, n: jnp.pad(w, ((0, 0), (0, n - w.shape[1])))
    w_ssd = padc(w_in[:, :o1], SSD_W)
    w_nsa = padc(w_in[:, o2:o3], NSA_W)
    z64 = jnp.zeros((64, RWKV_DIM), F32)
    seg = _head_onehot(RWKV_HEADS, RWKV_HD, LANE).T
    rwkv_prm = (p['rwkv_mu'][l][None], p['rwkv_w0'][l][None], p['rwkv_a0'][l][None], p['rwkv_k_k'][l][None],
                p['rwkv_k_a'][l][None], p['rwkv_r_k'][l].reshape(1, RWKV_DIM),
                jnp.concatenate([p['rwkv_w2'][l], z64], 0), jnp.concatenate([z64, p['rwkv_a2'][l]], 0),
                p['rwkv_g2'][l], seg, seg.T)
    wr = jnp.pad(jnp.concatenate([p['w_group'][l], p['w_router'][l]], axis=1),
                 ((0, 0), (0, LANE - N_GROUPS - N_EXPERTS)))
    row = lambda v: v[None]
    return dict(
        w_ssd=w_ssd.astype(BF16), w_rwkv=w_in[:, o1:o2].astype(BF16), w_nsa=w_nsa.astype(BF16),
        w_gate=w_in[:, o3:].astype(BF16),
        ssd=(jnp.pad(p['ssd_conv_w'][l], ((0, 4), (0, 0))), row(p['ssd_conv_b'][l]),
             row(jnp.pad(p['ssd_dt_bias'][l], (0, LANE - SSD_HEADS))), row(jnp.pad(p['ssd_a_log'][l], (0, LANE - SSD_HEADS))),
             row(jnp.repeat(p['ssd_d'][l], SSD_HD)), row(p['ssd_norm_w'][l])),
        rwkv=rwkv_prm, lnx=(row(p['rwkv_lnx_w'][l]), row(p['rwkv_lnx_b'][l])),
        cmp=_cmp_weights(p['cmp_pe'][l], p['cmp_w1'][l], p['cmp_w2'][l]),
        wo=tuple(p[k][l].astype(BF16) for k in ('w_o_ssd', 'w_o_rwkv', 'w_o_nsa', 'w_out')),
        ln1=(row(p['ln1_g'][l]), row(p['ln1_b'][l])), ln2=(row(p['ln2_g'][l]), row(p['ln2_b'][l])),
        wr=wr, w1=p['moe_w1'][l].astype(BF16), w3=p['moe_w3'][l].astype(BF16), w2=p['moe_w2'][l].astype(BF16))


def _trunk_layer(x, mod, l, w, pos0, conv_st, ssm_st, shift_st, rwkv_st, cache_win, cache_cmp, cache_sel,
                 page_table):
    b, L, d = x.shape
    m = b * L
    x2 = x.reshape(m, d)
    sh1, sc1, g1, sh2, sc2, g2 = [mod[:, k * d:(k + 1) * d] for k in range(6)]
    if b > 1:
        sh1, sc1, g1, sh2, sc2, g2 = [jnp.repeat(t, L, axis=0) for t in (sh1, sc1, g1, sh2, sc2, g2)]
    proj = lambda wt: mod_proj(x2, sc1, sh1, wt).reshape(b, L, -1)
    c_ssd, c_rwkv, c_nsa, c_gate = proj(w['w_ssd']), proj(w['w_rwkv']), proj(w['w_nsa']), proj(w['w_gate'])

    lp = -(-L // SSD_Q) * SSD_Q
    cst8 = jnp.pad(conv_st, ((0, 0), (8 - (SSD_CONV - 1), 0), (0, 0)))
    h0t = jnp.transpose(ssm_st.reshape(b, SSD_INNER, SSD_STATE), (0, 2, 1))
    y_a, ht = ssd_mixer(_pad_rows(c_ssd, lp), cst8, h0t, L, *w['ssd'])
    y_a = y_a[:, :L]
    ssm_new = jnp.transpose(ht, (0, 2, 1)).reshape(b, SSD_HEADS, SSD_HD, SSD_STATE)
    conv_new = c_ssd[:, L - (SSD_CONV - 1):, SSD_INNER:SSD_INNER + SSD_CONV_DIM]

    lp = -(-L // RWKV_C) * RWKV_C
    sh8 = jnp.pad(shift_st[:, None, :], ((0, 0), (7, 0), (0, 0)))
    pre = rwkv_prep(_pad_rows(c_rwkv, lp), sh8, L, w['rwkv'])
    y_b, sp = rwkv_scan(*pre, _pair_blockdiag(rwkv_st), *w['lnx'])
    y_b = y_b[:, :L]
    rwkv_new = _pair_unblock(sp)
    shift_new = c_rwkv[:, -1]

    pek, pev, wk, wv, w2k, w2v = w['cmp']
    kvrow = (2, NSA_KVH, NSA_HD)
    rows = lambda tile: c_nsa[:, :, tile * LANE:(tile + 2) * LANE]
    if cache_cmp is None:
        hp = cmp_h_rows(c_nsa, wk, wv, NSA_CMP_TILE)
        hp = _pad_rows(hp, NCP)
        hn = jnp.zeros((b, 8, CMP_HW), F32)
        ks, vst = kvprep_rows(c_nsa, SEL_TK, NSA_SEL_TILE)
        kwin, vwt = kvprep_rows(c_nsa, WIN_TK, NSA_WIN_TILE)
        cq, n_buf, n_win = c_nsa, 0, L
        win_new = rows(NSA_WIN_TILE)[:, max(L - WINDOW, 0):]
    else:
        cq = _pad_rows(c_nsa, TQ)
        hp = _pad_rows(cmp_h_pages(cache_cmp, page_table, l, wk, wv), NCP)
        hn = cmp_h_rows(cq, wk, wv, NSA_CMP_TILE)
        ks, vst = kvprep_pages(cache_sel, page_table, l, cq, NSA_SEL_TILE)
        n_buf = cache_win.shape[1]
        win_all = jnp.concatenate([cache_win.reshape(b, n_buf, 2 * LANE), rows(NSA_WIN_TILE)], axis=1)
        n_win = n_buf + L
        kwin, vwt = kvprep_rows(_pad_rows(win_all, -(-(n_buf + TQ) // WIN_TK) * WIN_TK), WIN_TK)
        win_new = win_all[:, n_win - min(WINDOW, n_win):]
    hn_row = NCP - 1 if cache_cmp is None else pos0 // CMP_STRIDE - 1
    ckh, ckl, cvt = cmp_finish(hp, hn, hn_row, pek, pev, wk, wv, w2k, w2v)
    if L == TQS:
        y_c = nsa_attention_small(c_nsa, ckh, ckl, cvt, ks, vst, kwin, vwt, pos0, n_buf, n_win)
    else:
        y_c = nsa_attention(cq, ckh, ckl, cvt, ks, vst, kwin, vwt, pos0, L, n_buf, n_win)[:, :L]
    cmp_rows = rows(NSA_CMP_TILE).reshape((b, L) + kvrow)
    sel_rows = rows(NSA_SEL_TILE).reshape((b, L) + kvrow)
    win_new = win_new.reshape(win_new.shape[:2] + kvrow)

    flat = lambda t: t.reshape(m, -1)
    x1 = merge_out(x2, flat(y_a), flat(y_b), flat(y_c), flat(c_gate), g1, *w['wo'], *w['ln1'])
    x_out = moe_out(x1, sc2, sh2, g2, w['wr'], w['w1'], w['w3'], w['w2'], *w['ln2'])
    return x_out.reshape(b, L, d), (cmp_rows, sel_rows, win_new, ssm_new, conv_new, rwkv_new, shift_new)


def kernel(x_prompt, x_sample, c_prompt, c_sample, cache_cmp, cache_sel, cache_win, state_ssm, state_ssm_conv,
           state_rwkv, state_rwkv_shift, page_table, w_ada, b_ada, w_in, ssd_conv_w, ssd_conv_b, ssd_dt_bias,
           ssd_a_log, ssd_d, ssd_norm_w, rwkv_mu, rwkv_w0, rwkv_w2, rwkv_a0, rwkv_a2, rwkv_g2, rwkv_k_k, rwkv_k_a,
           rwkv_r_k, rwkv_lnx_w, rwkv_lnx_b, cmp_pe, cmp_w1, cmp_w2, w_o_ssd, w_o_rwkv, w_o_nsa, w_out, ln1_g,
           ln1_b, ln2_g, ln2_b, w_group, w_router, moe_w1, moe_w3, moe_w2):
    p = dict(w_in=w_in, ssd_conv_w=ssd_conv_w, ssd_conv_b=ssd_conv_b, ssd_dt_bias=ssd_dt_bias, ssd_a_log=ssd_a_log,
             ssd_d=ssd_d, ssd_norm_w=ssd_norm_w, rwkv_mu=rwkv_mu, rwkv_w0=rwkv_w0, rwkv_w2=rwkv_w2, rwkv_a0=rwkv_a0,
             rwkv_a2=rwkv_a2, rwkv_g2=rwkv_g2, rwkv_k_k=rwkv_k_k, rwkv_k_a=rwkv_k_a, rwkv_r_k=rwkv_r_k,
             rwkv_lnx_w=rwkv_lnx_w, rwkv_lnx_b=rwkv_lnx_b, cmp_pe=cmp_pe, cmp_w1=cmp_w1, cmp_w2=cmp_w2,
             w_o_ssd=w_o_ssd, w_o_rwkv=w_o_rwkv, w_o_nsa=w_o_nsa, w_out=w_out, ln1_g=ln1_g, ln1_b=ln1_b,
             ln2_g=ln2_g, ln2_b=ln2_b, w_group=w_group, w_router=w_router, moe_w1=moe_w1, moe_w3=moe_w3,
             moe_w2=moe_w2)
    bp, bs = x_prompt.shape[0], x_sample.shape[0]
    past_len = page_table.shape[1] * PAGE
    nb = -(-(bp + bs) // SUBLANE) * SUBLANE
    c_all = jnp.pad(jnp.concatenate([c_prompt, c_sample], axis=0), ((0, nb - bp - bs), (0, 0)))
    mod = ada_mod(c_all, w_ada, b_ada)
    n_phys = cache_cmp.shape[1]
    cmp_pages = cache_cmp.reshape(DEPTH, n_phys, PAGE, 2 * LANE)
    sel_pages = cache_sel.reshape(DEPTH, n_phys, PAGE, 2 * LANE)
    zeros = lambda *s: jnp.zeros(s, F32)
    xp, xs = x_prompt, x_sample
    st_p, st_s = [], []
    for l in range(DEPTH):
        w = _layer_weights(l, p)
        xp, sp_l = _trunk_layer(xp, mod[l, :bp], l, w, 0, zeros(bp, SSD_CONV - 1, SSD_CONV_DIM),
                                zeros(bp, SSD_HEADS, SSD_HD, SSD_STATE), zeros(bp, RWKV_COLS),
                                zeros(bp, RWKV_HEADS, RWKV_HD, RWKV_HD), None, None, None, None)
        xs, ss_l = _trunk_layer(xs, mod[l, bp:bp + bs], l, w, past_len, state_ssm_conv[l], state_ssm[l],
                                state_rwkv_shift[l], state_rwkv[l], cache_win[l], cmp_pages, sel_pages, page_table)
        st_p.append(sp_l)
        st_s.append(ss_l)
    sp = [jnp.stack(z) for z in zip(*st_p)]
    ss = [jnp.stack(z) for z in zip(*st_s)]
    return (xp, xs, sp[0], sp[1], sp[2], sp[3], sp[4], sp[5], sp[6], ss[0], ss[1], ss[2], ss[3], ss[4], ss[5], ss[6])
```

```python
import functools
import math

import jax
import jax.numpy as jnp
from jax import lax
from jax.experimental import pallas as pl
from jax.experimental.pallas import tpu as pltpu

F32 = jnp.float32
BF16 = jnp.bfloat16
HIGHEST = lax.Precision.HIGHEST

D_MODEL = 1024
DEPTH = 2
PAGE = 128
SSD_HEADS, SSD_HD, SSD_GROUPS, SSD_STATE, SSD_CONV = 16, 64, 2, 128, 4
SSD_INNER = SSD_HEADS * SSD_HD
SSD_CONV_DIM = SSD_INNER + 2 * SSD_GROUPS * SSD_STATE
SSD_COLS = SSD_INNER + SSD_CONV_DIM + SSD_HEADS
RWKV_HEADS, RWKV_HD = 16, 64
RWKV_DIM = RWKV_HEADS * RWKV_HD
RWKV_COLS = 3 * RWKV_DIM + 64 + 64 + 128
RWKV_LNX_EPS = 64e-5
NSA_HEADS, NSA_KVH, NSA_HPG, NSA_HD = 16, 2, 8, 64
NSA_DIM = NSA_HEADS * NSA_HD
NSA_COLS = NSA_DIM + 3 * 2 * NSA_KVH * NSA_HD + 3 * NSA_HEADS
CMP_BLOCK, CMP_STRIDE, CMP_HIDDEN = 32, 16, 128
SEL_BLOCK, N_SEL, WINDOW = 64, 16, 512
FORCE_BONUS = 1e4
N_GROUPS, EPG, N_EXPERTS, EXPERT_HIDDEN = 4, 4, 16, 256
ALPHA = (2 * DEPTH) ** 0.25
LN_EPS = 1e-5
RMS_EPS = 1e-5
NEG = -1e30

LANE = 128
SUBLANE = 8
VMEM_LIMIT = 56 * 1024 * 1024


def _cparams(sem):
    return pltpu.CompilerParams(dimension_semantics=sem, vmem_limit_bytes=VMEM_LIMIT)


def _dot(a, b):
    return jnp.dot(a.astype(BF16), b.astype(BF16), preferred_element_type=F32)


def _dot_hi(a, b):
    return jnp.dot(a, b, precision=HIGHEST, preferred_element_type=F32)


def _dot_nt(a, b):
    return lax.dot_general(a.astype(BF16), b.astype(BF16), (((1,), (1,)), ((), ())),
                           preferred_element_type=F32)


def _dot_nt_hi(a, b):
    return lax.dot_general(a, b, (((1,), (1,)), ((), ())), precision=HIGHEST,
                           preferred_element_type=F32)


def _split16(a):
    hi = a.astype(BF16)
    return hi, (a - hi.astype(F32)).astype(BF16)


def _split24(a):
    h1 = a.astype(BF16)
    r1 = a - h1.astype(F32)
    h2 = r1.astype(BF16)
    return h1, h2, (r1 - h2.astype(F32)).astype(BF16)


def _dot_w01(x, w16):
    return sum(_dot(piece, w16) for piece in _split24(x))


def _dot3(a, b):
    (ah, al), (bh, bl) = a, b
    return _dot(jnp.concatenate([ah, ah, al], axis=1), jnp.concatenate([bh, bl, bh], axis=0))


def _dot3_nt(a, b):
    (ah, al), (bh, bl) = a, b
    return _dot_nt(ah, bh) + _dot_nt(ah, bl) + _dot_nt(al, bh)


def _sigmoid(x):
    return 1.0 / (1.0 + jnp.exp(-x))


def _silu(x):
    return x * _sigmoid(x)


def _softplus(x):
    return jnp.maximum(x, 0.0) + jnp.log(1.0 + jnp.exp(-jnp.abs(x)))


def _iota(shape, dim):
    return lax.broadcasted_iota(jnp.int32, shape, dim)


def _head_onehot(n_heads, hd, pad_rows):
    r = jnp.arange(pad_rows)[:, None]
    c = jnp.arange(n_heads * hd)[None, :] // hd
    return (r == c).astype(F32)


def _ada_kernel(c_ref, w_ref, b_ref, o_ref):
    o_ref[...] = _dot_hi(_silu(c_ref[...]), w_ref[...]) + b_ref[...]


def ada_mod(c_all, w_ada, b_ada):
    nb = c_all.shape[0]
    return pl.pallas_call(
        _ada_kernel,
        grid=(DEPTH, 6),
        in_specs=[pl.BlockSpec((nb, D_MODEL), lambda l, j: (0, 0)),
                  pl.BlockSpec((None, D_MODEL, D_MODEL), lambda l, j: (l, 0, j)),
                  pl.BlockSpec((None, 1, D_MODEL), lambda l, j: (l, 0, j))],
        out_specs=pl.BlockSpec((None, nb, D_MODEL), lambda l, j: (l, 0, j)),
        out_shape=jax.ShapeDtypeStruct((DEPTH, nb, 6 * D_MODEL), F32),
        compiler_params=_cparams(("arbitrary", "arbitrary")),
        name="ada_mod",
    )(c_all, w_ada, b_ada.reshape(DEPTH, 1, 6 * D_MODEL))


def _inproj_kernel(x_ref, sc_ref, sh_ref, w_ref, o_ref):
    u = x_ref[...] * (1.0 + sc_ref[...]) + sh_ref[...]
    o_ref[...] = _dot(u, w_ref[...])


def mod_proj(x, sc, sh, w, tm=256):
    m, k = x.shape
    n = w.shape[1]
    tm = min(tm, m)
    per_row = sc.shape[0] != 1
    mspec = (pl.BlockSpec((tm, k), lambda i: (i, 0)) if per_row
             else pl.BlockSpec((1, k), lambda i: (0, 0)))
    return pl.pallas_call(
        _inproj_kernel,
        grid=(m // tm,),
        in_specs=[pl.BlockSpec((tm, k), lambda i: (i, 0)), mspec, mspec,
                  pl.BlockSpec((k, n), lambda i: (0, 0))],
        out_specs=pl.BlockSpec((tm, n), lambda i: (i, 0)),
        out_shape=jax.ShapeDtypeStruct((m, n), F32),
        compiler_params=_cparams(("arbitrary",)),
        name="mod_proj",
    )(x, sc, sh, w)


SSD_Q = 128
SSD_W = SSD_INNER + SSD_CONV_DIM + LANE


def _ssd_kernel(n_valid, zxd_ref, cst_ref, h0_ref, cw_ref, cb_ref, dtb_ref, alog_ref, dexp_ref,
                nw_ref, e16_ref, tri_ref, y_ref, hout_ref, ext, hT):
    j = pl.program_id(1)
    q = SSD_Q

    @pl.when(j == 0)
    def _():
        ext[0:8, :] = cst_ref[...]
        hT[...] = h0_ref[...]

    ext[8:8 + q, :] = zxd_ref[:, SSD_INNER:SSD_INNER + SSD_CONV_DIM]
    conv = (cb_ref[...] + ext[5:5 + q, :] * cw_ref[0:1, :] + ext[6:6 + q, :] * cw_ref[1:2, :]
            + ext[7:7 + q, :] * cw_ref[2:3, :] + ext[8:8 + q, :] * cw_ref[3:4, :])
    ext[0:8, :] = ext[q:q + 8, :]
    xbc = _silu(conv)
    xs = xbc[:, :SSD_INNER]
    bm = xbc[:, SSD_INNER:SSD_INNER + 2 * SSD_STATE]
    cm = xbc[:, SSD_INNER + 2 * SSD_STATE:]

    row = _iota((q, LANE), 0)
    lane = _iota((q, LANE), 1)
    dt = _softplus(zxd_ref[:, SSD_INNER + SSD_CONV_DIM:] + dtb_ref[...])
    dt = jnp.where((lane < SSD_HEADS) & (row + j * q < n_valid), dt, 0.0)
    a = -jnp.exp(alog_ref[...])
    acum = _dot_hi(tri_ref[...], dt * a)
    a_last = acum[q - 1:q, :]
    e16 = e16_ref[...]
    dt_e = _dot_hi(dt, e16)
    ea_e = _dot_hi(jnp.exp(acum), e16)
    dte_e = _dot_hi(jnp.exp(a_last - acum), e16)
    elast_e = ea_e[q - 1:q, :]
    xdt = xs * dt_e
    acum_t = acum.T
    tril = row >= lane
    lo_half = lane < SSD_HD

    y_tiles = []
    for g in range(SSD_GROUPS):
        cc = cm[:, g * SSD_STATE:(g + 1) * SSD_STATE]
        bc = bm[:, g * SSD_STATE:(g + 1) * SSD_STATE]
        cb = _dot_nt(cc, bc)
        gs = slice(g * 512, (g + 1) * 512)
        h_in = hT[:, gs]
        y_off = _dot(cc, h_in) * ea_e[:, gs]
        hT[:, gs] = h_in * elast_e[:, gs] + _dot(bc.T, xdt[:, gs] * dte_e[:, gs])
        for tl in range(4):
            t = 4 * g + tl
            xt = xdt[:, t * LANE:(t + 1) * LANE]
            yd = jnp.zeros((q, LANE), F32)
            for sub in range(2):
                h = 2 * t + sub
                diff = acum[:, h:h + 1] - acum_t[h:h + 1, :]
                m = cb * jnp.exp(jnp.where(tril, diff, NEG))
                yd = yd + _dot(m, jnp.where(lo_half if sub == 0 else ~lo_half, xt, 0.0))
            y_tiles.append(yd + y_off[:, tl * LANE:(tl + 1) * LANE])
    y = jnp.concatenate(y_tiles, axis=1) + dexp_ref[...] * xs
    y = y * _silu(zxd_ref[:, :SSD_INNER])
    outs = []
    for g in range(SSD_GROUPS):
        yg = y[:, g * 512:(g + 1) * 512]
        ms = jnp.sum(yg * yg, axis=-1, keepdims=True) * (1.0 / 512.0)
        outs.append(yg * lax.rsqrt(ms + RMS_EPS))
    y_ref[...] = jnp.concatenate(outs, axis=1) * nw_ref[...]

    @pl.when(j == pl.num_programs(1) - 1)
    def _():
        hout_ref[...] = hT[...]


def ssd_mixer(zxd, conv_st8, h0t, n_valid, cw8, cb, dtb, alog, dexp, nw):
    b, lp, _ = zxd.shape
    nj = lp // SSD_Q
    e16 = _head_onehot(SSD_HEADS, SSD_HD, LANE)
    tri = (jnp.arange(SSD_Q)[:, None] >= jnp.arange(SSD_Q)[None, :]).astype(F32)
    full = lambda shp: pl.BlockSpec(shp, lambda bi, j: (0,) * len(shp))
    return pl.pallas_call(
        functools.partial(_ssd_kernel, n_valid),
        grid=(b, nj),
        in_specs=[pl.BlockSpec((None, SSD_Q, SSD_W), lambda bi, j: (bi, j, 0)),
                  pl.BlockSpec((None, 8, SSD_CONV_DIM), lambda bi, j: (bi, 0, 0)),
                  pl.BlockSpec((None, SSD_STATE, SSD_INNER), lambda bi, j: (bi, 0, 0)),
                  full((8, SSD_CONV_DIM)), full((1, SSD_CONV_DIM)), full((1, LANE)), full((1, LANE)),
                  full((1, SSD_INNER)), full((1, SSD_INNER)), full((LANE, SSD_INNER)),
                  full((SSD_Q, SSD_Q))],
        out_specs=[pl.BlockSpec((None, SSD_Q, SSD_INNER), lambda bi, j: (bi, j, 0)),
                   pl.BlockSpec((None, SSD_STATE, SSD_INNER), lambda bi, j: (bi, 0, 0))],
        out_shape=[jax.ShapeDtypeStruct((b, lp, SSD_INNER), F32),
                   jax.ShapeDtypeStruct((b, SSD_STATE, SSD_INNER), F32)],
        scratch_shapes=[pltpu.VMEM((SSD_Q + 8, SSD_CONV_DIM), F32),
                        pltpu.VMEM((SSD_STATE, SSD_INNER), F32)],
        compiler_params=_cparams(("arbitrary", "arbitrary")),
        name="ssd_mixer",
    )(zxd, conv_st8, h0t, cw8, cb, dtb, alog, dexp, nw, e16, tri)


def _rwkv_prep_kernel(n_valid, tr, x_ref, sh8_ref, mu_ref, w0_ref, a0_ref, kk_ref, ka_ref, rk_ref,
                      w2_ref, a2_ref, g2_ref, seg_ref, e16_ref,
                      r_o, ld_o, k_o, v_o, kk_o, bb_o, g_o, bv_o, ext):
    j = pl.program_id(1)

    @pl.when(j == 0)
    def _():
        ext[0:8, :] = sh8_ref[...]

    x = x_ref[...]
    ext[8:8 + tr, :] = x
    prev = ext[7:7 + tr, :]
    ext[0:8, :] = ext[tr:tr + 8, :]
    xm = x + (prev - x) * mu_ref[...]
    d = RWKV_DIM
    r, k, v = xm[:, :d], xm[:, d:2 * d], xm[:, 2 * d:3 * d]
    lo = xm[:, 3 * d:3 * d + LANE]
    glo = xm[:, 3 * d + LANE:]
    w = w0_ref[...] + _dot3(_split16(jnp.tanh(lo)), _split16(w2_ref[...]))
    ld = -jnp.exp(-_softplus(-w) - 0.5)
    a = _sigmoid(a0_ref[...] + _dot3(_split16(lo), _split16(a2_ref[...])))
    g = _dot(_sigmoid(glo), g2_ref[...])
    seg, e16 = seg_ref[...].astype(BF16), e16_ref[...].astype(BF16)
    kk = k * kk_ref[...]
    ss = _dot_w01(kk * kk, seg)
    kk = kk * _dot_w01(lax.rsqrt(jnp.maximum(ss, 1e-24)), e16)
    k2 = k * (1.0 + (a - 1.0) * ka_ref[...])
    bonus = _dot_w01(_dot_w01(r * k2 * rk_ref[...], seg), e16)
    valid = (_iota((tr, 1), 0) + j * tr) < n_valid
    zero = lambda t: jnp.where(valid, t, 0.0)
    r_o[...] = r
    ld_o[...] = zero(ld)
    k_o[...] = zero(k2)
    v_o[...] = zero(v)
    kk_o[...] = zero(kk)
    bb_o[...] = zero(kk * a)
    g_o[...] = g
    bv_o[...] = bonus * v


def rwkv_prep(cols, shift8, n_valid, prm):
    b, lp, _ = cols.shape
    tr = min(128, lp)
    d = RWKV_DIM
    full = lambda shp: pl.BlockSpec(shp, lambda bi, j: (0,) * len(shp))
    row = lambda: pl.BlockSpec((None, tr, d), lambda bi, j: (bi, j, 0))
    return pl.pallas_call(
        functools.partial(_rwkv_prep_kernel, n_valid, tr),
        grid=(b, lp // tr),
        in_specs=[pl.BlockSpec((None, tr, RWKV_COLS), lambda bi, j: (bi, j, 0)),
                  pl.BlockSpec((None, 8, RWKV_COLS), lambda bi, j: (bi, 0, 0)),
                  full((1, RWKV_COLS))] + [full((1, d))] * 5 + [full((LANE, d))] * 3
                 + [full((d, LANE)), full((LANE, d))],
        out_specs=[row() for _ in range(8)],
        out_shape=[jax.ShapeDtypeStruct((b, lp, d), F32)] * 8,
        scratch_shapes=[pltpu.VMEM((tr + 8, RWKV_COLS), F32)],
        compiler_params=_cparams(("arbitrary", "arbitrary")),
        name="rwkv_prep",
    )(cols, shift8, *prm)


RWKV_C = 64


def _rwkv_scan_kernel(nch, r_ref, ld_ref, k_ref, v_ref, kk_ref, bb_ref, g_ref, bv_ref, s0_ref,
                      lnw_ref, lnb_ref, y_ref, sout_ref, st):
    i = pl.program_id(2)
    c = RWKV_C
    npp = st.shape[0]

    @pl.when(i == 0)
    def _():
        st[...] = s0_ref[...]

    n2 = 2 * c
    rr = _iota((n2, n2), 0)
    cc = _iota((n2, n2), 1)
    eye = (rr == cc).astype(F32)
    upper = rr < cc
    upper_eq = rr <= cc
    tri = (_iota((c, c), 0) >= _iota((c, c), 1)).astype(BF16)
    lane = _iota((c, LANE), 1)
    m0 = lane < RWKV_HD
    hmean = jnp.where((rr < RWKV_HD) == (cc < RWKV_HD), 1.0 / RWKV_HD, 0.0).astype(BF16)

    def stack(t):
        return jnp.concatenate([jnp.where(m0, t, 0.0), jnp.where(m0, 0.0, t)], axis=0)

    def head_mean(t):
        return sum(_dot(piece, hmean) for piece in _split24(t))

    units = [(ci, pi) for ci in range(nch) for pi in range(npp)]
    pre = []
    for ci, pi in units:
        sl = pl.ds(ci * c, c)
        ls = slice(pi * LANE, (pi + 1) * LANE)
        r, ld, k, v, kk, bb = (ref[sl, ls] for ref in (r_ref, ld_ref, k_ref, v_ref, kk_ref, bb_ref))
        cum = sum(_dot(tri, piece) for piece in _split24(ld))
        p_in = jnp.exp(cum)
        p_inv = jnp.exp(-cum)
        p_c = p_in[c - 1:c, :]
        kks = stack(kk * jnp.exp(cum - ld))
        rs = stack(r * p_in)
        bs = stack(bb * p_inv)
        ks = stack(k * p_inv)
        kks16, rs16, bs16, ks16 = (t.astype(BF16) for t in (kks, rs, bs, ks))
        pre.append(dict(
            p_c=p_c, vs_t=_split16(stack(v).T), kks_t=kks.T.astype(BF16), rs_t=rs.T.astype(BF16),
            upd_r=_split16(jnp.concatenate([bs * p_c, ks * p_c], axis=0)),
            nt=jnp.where(upper, _dot_nt(bs16, kks16), 0.0),
            avk=jnp.where(upper, _dot_nt(ks16, kks16), 0.0).astype(BF16),
            arb=jnp.where(upper_eq, _dot_nt(bs16, rs16), 0.0).astype(BF16),
            ark=jnp.where(upper_eq, _dot_nt(ks16, rs16), 0.0).astype(BF16)))
    tinv = [eye - u['nt'] for u in pre]
    pw = [u['nt'].astype(BF16) for u in pre]
    pw = [_dot(w, w).astype(BF16) for w in pw]
    for _ in range(4):
        both = [_dot(jnp.concatenate([w, t.astype(BF16)], axis=0), w) for w, t in zip(pw, tinv)]
        pw = [b[:n2, :].astype(BF16) for b in both]
        tinv = [t + b[n2:, :] for t, b in zip(tinv, both)]
    tinv = [_split16(t + _dot(t, w)) for t, w in zip(tinv, pw)]

    for ci in range(nch):
        sl = pl.ds(ci * c, c)
        for pi in range(npp):
            ls = slice(pi * LANE, (pi + 1) * LANE)
            u = pre[ci * npp + pi]
            s = st[pi]
            s16 = s.astype(BF16)
            vs_t16 = u['vs_t'][0]
            ut = _dot3(_split16(_dot(s16, u['kks_t']) + _dot(vs_t16, u['avk'])), tinv[ci * npp + pi])
            ut_p = _split16(ut)
            yt = _dot(s16, u['rs_t']) - _dot(ut_p[0], u['arb']) + _dot(vs_t16, u['ark'])
            upd_l = tuple(jnp.concatenate([-a, b], axis=1) for a, b in zip(ut_p, u['vs_t']))
            st[pi] = s * u['p_c'] + _dot3(upd_l, u['upd_r'])
            ys = yt.T
            y = ys[:c, :] + ys[c:, :]
            yc = y - head_mean(y)
            var = head_mean(yc * yc)
            y = yc * lax.rsqrt(var + RWKV_LNX_EPS) * lnw_ref[:, ls] + lnb_ref[:, ls] + bv_ref[sl, ls]
            y_ref[sl, ls] = y * g_ref[sl, ls]

    @pl.when(i == pl.num_programs(2) - 1)
    def _():
        sout_ref[...] = st[...]


def rwkv_scan(r, ld, k, v, kk, bb, g, bv, s0p, lnw, lnb):
    b, lp, d = r.shape
    npair = d // LANE
    nch = min(8, lp // RWKV_C)
    npp = min(npair, 16 // nch)
    rb = nch * RWKV_C
    row = lambda: pl.BlockSpec((None, rb, npp * LANE), lambda bi, p, i: (bi, i, p))
    st = lambda: pl.BlockSpec((None, npp, LANE, LANE), lambda bi, p, i: (bi, p, 0, 0))
    vec = lambda: pl.BlockSpec((1, npp * LANE), lambda bi, p, i: (0, p))
    return pl.pallas_call(
        functools.partial(_rwkv_scan_kernel, nch),
        grid=(b, npair // npp, lp // rb),
        in_specs=[row() for _ in range(8)] + [st(), vec(), vec()],
        out_specs=[row(), st()],
        out_shape=[jax.ShapeDtypeStruct((b, lp, d), F32),
                   jax.ShapeDtypeStruct((b, npair, LANE, LANE), F32)],
        scratch_shapes=[pltpu.VMEM((npp, LANE, LANE), F32)],
        compiler_params=_cparams(("arbitrary", "arbitrary", "arbitrary")),
        name="rwkv_scan",
    )(r, ld, k, v, kk, bb, g, bv, s0p, lnw, lnb)


def _pair_blockdiag(s):
    b = s.shape[0]
    s = s.reshape(b, 8, 2, RWKV_HD, RWKV_HD)
    z = jnp.zeros_like(s[:, :, 0])
    top = jnp.concatenate([s[:, :, 0], z], axis=-1)
    bot = jnp.concatenate([z, s[:, :, 1]], axis=-1)
    return jnp.concatenate([top, bot], axis=-2)


def _pair_unblock(sp):
    b = sp.shape[0]
    return jnp.stack([sp[:, :, :RWKV_HD, :RWKV_HD], sp[:, :, RWKV_HD:, RWKV_HD:]], axis=2).reshape(
        b, RWKV_HEADS, RWKV_HD, RWKV_HD)


CMP_HW = 4 * NSA_KVH * CMP_HIDDEN
NSEG_PAGE = PAGE // CMP_STRIDE


def _cmp_h_rows_kernel(nsr, xk_ref, xv_ref, wk_ref, wv_ref, o_ref):
    half = CMP_HW // 2
    acc_k = jnp.zeros((nsr, half), F32)
    acc_v = jnp.zeros((nsr, half), F32)
    for s in range(CMP_STRIDE):
        rows = pl.ds(s, nsr, stride=CMP_STRIDE)
        acc_k = acc_k + _dot(xk_ref[rows, :], wk_ref[s])
        acc_v = acc_v + _dot(xv_ref[rows, :], wv_ref[s])
    o_ref[...] = jnp.concatenate([acc_k, acc_v], axis=1)


def cmp_h_rows(rows, wk, wv, lt=0):
    b, t, _ = rows.shape
    tb = 2048 if t % 2048 == 0 else t
    nsr = tb // CMP_STRIDE
    wspec = pl.BlockSpec((CMP_STRIDE, LANE, CMP_HW // 2), lambda bi, i: (0, 0, 0))
    return pl.pallas_call(
        functools.partial(_cmp_h_rows_kernel, nsr),
        grid=(b, t // tb),
        in_specs=[pl.BlockSpec((None, tb, LANE), lambda bi, i: (bi, i, lt)),
                  pl.BlockSpec((None, tb, LANE), lambda bi, i: (bi, i, lt + 1)), wspec, wspec],
        out_specs=pl.BlockSpec((None, nsr, CMP_HW), lambda bi, i: (bi, i, 0)),
        out_shape=jax.ShapeDtypeStruct((b, t // CMP_STRIDE, CMP_HW), F32),
        compiler_params=_cparams(("arbitrary", "arbitrary")),
        name="cmp_h_rows",
    )(rows, rows, wk, wv)


PAGES_PER_STEP = 16
CMP_PAGES_PER_STEP = 32


def _cmp_h_pages_kernel(pt_ref, *refs):
    pp = CMP_PAGES_PER_STEP
    pk, pv = refs[:pp], refs[pp:2 * pp]
    wk_ref, wv_ref, o_ref = refs[2 * pp:]
    half = CMP_HW // 2
    acc_k = jnp.zeros((pp * NSEG_PAGE, half), F32)
    acc_v = jnp.zeros((pp * NSEG_PAGE, half), F32)
    for s in range(CMP_STRIDE):
        rows = pl.ds(s, NSEG_PAGE, stride=CMP_STRIDE)
        xk = jnp.concatenate([r[rows, :] for r in pk], axis=0)
        xv = jnp.concatenate([r[rows, :] for r in pv], axis=0)
        acc_k = acc_k + _dot(xk, wk_ref[s])
        acc_v = acc_v + _dot(xv, wv_ref[s])
    o_ref[...] = jnp.concatenate([acc_k, acc_v], axis=1)


def _page_specs(layer, lane_tile, n_pages, pp=PAGES_PER_STEP):
    def spec(p):
        def imap(bi, i, pt):
            return (layer, pt[bi, jnp.minimum(i * pp + p, n_pages - 1)], 0, lane_tile)
        return pl.BlockSpec((None, None, PAGE, LANE), imap)
    return [spec(p) for p in range(pp)]


def cmp_h_pages(cache, page_table, layer, wk, wv):
    b, n_pages = page_table.shape
    pp = CMP_PAGES_PER_STEP
    wspec = pl.BlockSpec((CMP_STRIDE, LANE, CMP_HW // 2), lambda bi, i, pt: (0, 0, 0))
    gs = pltpu.PrefetchScalarGridSpec(
        num_scalar_prefetch=1, grid=(b, n_pages // pp),
        in_specs=_page_specs(layer, 0, n_pages, pp) + _page_specs(layer, 1, n_pages, pp) + [wspec, wspec],
        out_specs=pl.BlockSpec((None, pp * NSEG_PAGE, CMP_HW), lambda bi, i, pt: (bi, i, 0)))
    return pl.pallas_call(
        _cmp_h_pages_kernel, grid_spec=gs,
        out_shape=jax.ShapeDtypeStruct((b, n_pages * NSEG_PAGE, CMP_HW), F32),
        compiler_params=_cparams(("arbitrary", "arbitrary")),
        name="cmp_h_pages",
    )(page_table, *([cache] * (2 * pp)), wk, wv)


NCP = 1024


def _cmp_finish_kernel(hn_row, hp_ref, hn_ref, pek_ref, pev_ref, wk_ref, wv_ref, w2k_ref, w2v_ref,
                       ckh_ref, ckl_ref, cvt_ref):
    qw = CMP_HW // 4
    rk = jnp.zeros((8, 2 * qw), F32)
    rv = jnp.zeros((8, 2 * qw), F32)
    for s in range(CMP_STRIDE):
        rk = rk + _dot(pek_ref[s], wk_ref[s])
        rv = rv + _dot(pev_ref[s], wv_ref[s])
    last = _iota((NCP, qw), 0) == hn_row

    def hidden(off, rb):
        h0 = hp_ref[:, off:off + qw]
        h1 = pltpu.roll(hp_ref[:, off + qw:off + 2 * qw], NCP - 1, 0)
        h1 = jnp.where(last, hn_ref[0:1, off + qw:off + 2 * qw], h1)
        return _silu(h0 + h1 + rb[0:1, :qw] + rb[1:2, qw:])

    ckh_ref[...], ckl_ref[...] = _split16(_dot(hidden(0, rk), w2k_ref[...]))
    cvt_ref[...] = _dot(hidden(2 * qw, rv), w2v_ref[...]).T.astype(BF16)


def cmp_finish(hp, hn, hn_row, pek, pev, wk, wv, w2k, w2v):
    b = hp.shape[0]
    full = lambda shp: pl.BlockSpec(shp, lambda bi: (0,) * len(shp))
    return pl.pallas_call(
        functools.partial(_cmp_finish_kernel, hn_row),
        grid=(b,),
        in_specs=[pl.BlockSpec((None, NCP, CMP_HW), lambda bi: (bi, 0, 0)),
                  pl.BlockSpec((None, 8, CMP_HW), lambda bi: (bi, 0, 0)),
                  full((CMP_STRIDE, 8, LANE)), full((CMP_STRIDE, 8, LANE)),
                  full((CMP_STRIDE, LANE, CMP_HW // 2)), full((CMP_STRIDE, LANE, CMP_HW // 2)),
                  full((CMP_HW // 4, LANE)), full((CMP_HW // 4, LANE))],
        out_specs=[pl.BlockSpec((None, NCP, LANE), lambda bi: (bi, 0, 0)),
                   pl.BlockSpec((None, NCP, LANE), lambda bi: (bi, 0, 0)),
                   pl.BlockSpec((None, LANE, NCP), lambda bi: (bi, 0, 0))],
        out_shape=[jax.ShapeDtypeStruct((b, NCP, LANE), BF16),
                   jax.ShapeDtypeStruct((b, NCP, LANE), BF16),
                   jax.ShapeDtypeStruct((b, LANE, NCP), BF16)],
        compiler_params=_cparams(("arbitrary",)),
        name="cmp_finish",
    )(hp, hn, pek, pev, wk, wv, w2k, w2v)


def _cmp_weights(pe, w1, w2):
    eye_g = jnp.eye(NSA_KVH, dtype=F32)

    def first(e):
        w = w1[e].reshape(2, CMP_STRIDE, NSA_HD, CMP_HIDDEN)
        w = jnp.einsum('isdf,gh->sgdihf', w, eye_g)
        return w.reshape(CMP_STRIDE, LANE, CMP_HW // 2).astype(BF16)

    def second(e):
        return jnp.einsum('fd,gh->gfhd', w2[e], eye_g).reshape(CMP_HW // 4, LANE).astype(BF16)

    def pos(e):
        p = pe[e].reshape(2, CMP_STRIDE, NSA_HD)
        p = jnp.tile(jnp.transpose(p, (1, 0, 2)), (1, 1, NSA_KVH))
        return jnp.pad(p, ((0, 0), (0, 6), (0, 0)))

    return pos(0), pos(1), first(0), first(1), second(0), second(1)


def _kvprep_rows_kernel(nsub, tk, xk_ref, xv_ref, k_ref, vt_ref):
    k_ref[...] = xk_ref[...].astype(BF16)
    for u in range(nsub):
        vt_ref[u] = xv_ref[u * tk:(u + 1) * tk, :].T.astype(BF16)


def kvprep_rows(rows, tk, lt=0):
    b, t, _ = rows.shape
    tt = 512 if t % 512 == 0 else tk
    nsub = tt // tk
    return pl.pallas_call(
        functools.partial(_kvprep_rows_kernel, nsub, tk),
        grid=(b, t // tt),
        in_specs=[pl.BlockSpec((None, tt, LANE), lambda bi, i: (bi, i, lt)),
                  pl.BlockSpec((None, tt, LANE), lambda bi, i: (bi, i, lt + 1))],
        out_specs=[pl.BlockSpec((None, tt, LANE), lambda bi, i: (bi, i, 0)),
                   pl.BlockSpec((None, nsub, LANE, tk), lambda bi, i: (bi, i, 0, 0))],
        out_shape=[jax.ShapeDtypeStruct((b, t, LANE), BF16),
                   jax.ShapeDtypeStruct((b, t // tk, LANE, tk), BF16)],
        compiler_params=_cparams(("arbitrary", "arbitrary")),
        name="kvprep_rows",
    )(rows, rows)


SEL_TK = 512
WIN_TK = 128


def _kvprep_pages_kernel(n_steps, pt_ref, *refs):
    pp = PAGES_PER_STEP
    pk, pv = refs[:pp], refs[pp:2 * pp]
    nk_ref, nv_ref, k_ref, vt_ref = refs[2 * pp:]
    i = pl.program_id(1)
    per = SEL_TK // PAGE

    @pl.when(i < n_steps - 1)
    def _():
        for p in range(pp):
            k_ref[p * PAGE:(p + 1) * PAGE, :] = pk[p][...].astype(BF16)
            vt_ref[p // per, :, (p % per) * PAGE:(p % per + 1) * PAGE] = pv[p][...].T.astype(BF16)

    @pl.when(i == n_steps - 1)
    def _():
        k_ref[...] = jnp.zeros(k_ref.shape, BF16)
        vt_ref[...] = jnp.zeros(vt_ref.shape, BF16)
        k_ref[0:PAGE, :] = nk_ref[...].astype(BF16)
        vt_ref[0, :, 0:PAGE] = nv_ref[...].T.astype(BF16)


def kvprep_pages(cache, page_table, layer, new_rows, lt0=0):
    b, n_pages = page_table.shape
    pp = PAGES_PER_STEP
    n_steps = n_pages // pp + 1
    t = n_steps * pp * PAGE
    new = lambda lt: pl.BlockSpec((None, PAGE, LANE), lambda bi, i, pt: (bi, 0, lt0 + lt))
    gs = pltpu.PrefetchScalarGridSpec(
        num_scalar_prefetch=1, grid=(b, n_steps),
        in_specs=_page_specs(layer, 0, n_pages) + _page_specs(layer, 1, n_pages) + [new(0), new(1)],
        out_specs=[pl.BlockSpec((None, pp * PAGE, LANE), lambda bi, i, pt: (bi, i, 0)),
                   pl.BlockSpec((None, pp * PAGE // SEL_TK, LANE, SEL_TK), lambda bi, i, pt: (bi, i, 0, 0))])
    return pl.pallas_call(
        functools.partial(_kvprep_pages_kernel, n_steps), grid_spec=gs,
        out_shape=[jax.ShapeDtypeStruct((b, t, LANE), BF16),
                   jax.ShapeDtypeStruct((b, t // SEL_TK, LANE, SEL_TK), BF16)],
        compiler_params=_cparams(("arbitrary", "arbitrary")),
        name="kvprep_pages",
    )(page_table, *([cache] * (2 * pp)), new_rows, new_rows)


TQ = 128
NQL = NSA_HEADS * TQ
NSA_CMP_TILE, NSA_SEL_TILE, NSA_WIN_TILE, NSA_GL_TILE = 8, 10, 12, 14
NSA_W = (NSA_GL_TILE + 1) * LANE
WIN_TILES = WINDOW // WIN_TK + 1


def _rowmax(x):
    return jnp.max(x, axis=0, keepdims=True)


def _rowsum(x):
    return jnp.sum(x, axis=0, keepdims=True)


LOG2E = 1.4426950408889634
SEL_PER = SEL_TK // SEL_BLOCK


def _real(m):
    return jnp.where(m > 0.5 * NEG, m, 0.0)


def _nsa_attn_kernel(pos0, nc, ns, wt0, wpos0, n_win,
                     q_ref, gl_ref, ckh_ref, ckl_ref, cvt_ref, ks_ref, vst_ref, *rest):
    kw = rest[:WIN_TILES]
    vw = rest[WIN_TILES:2 * WIN_TILES]
    msel_ref, o_ref, sel_scr, work_scr, sel3_scr, imp_scr, oc_scr, acc_scr, ml_scr, ot_scr = rest[2 * WIN_TILES:]
    i = pl.program_id(1)
    t0 = pos0 + i * TQ
    nsp = sel_scr.shape[0]
    hd = NSA_HD

    qt = (q_ref[...] * (hd ** -0.5 * LOG2E)).T
    zero = jnp.zeros((hd, TQ), F32)
    pieces = []
    for jh in range(NSA_HEADS):
        blk = qt[jh * hd:(jh + 1) * hd, :]
        pieces.append(jnp.concatenate([blk, zero] if jh < NSA_HPG else [zero, blk], axis=0))
    qbd = jnp.concatenate(pieces, axis=1)
    qbd16 = qbd.astype(BF16)
    t_q = t0 + _iota((1, TQ), 1)
    gw = NSA_HPG * TQ

    def heads(x, n=NSA_HPG):
        return jnp.concatenate([x] * n, axis=1)

    c_idx = _iota((NCP, TQ), 0)
    cbias = jnp.where((CMP_STRIDE * c_idx + (CMP_BLOCK - 1) <= t_q) & (c_idx < nc), 0.0, NEG)
    for g in range(NSA_KVH):
        gs = slice(g * gw, (g + 1) * gw)
        qh, ql = _split16(qbd[:, gs])
        ckh = ckh_ref[...]
        s = _dot(ckh, qh) + _dot(ckh, ql) + _dot(ckl_ref[...], qh) + heads(cbias)
        p = jnp.exp2(s - _real(_rowmax(s)))
        l = _rowsum(p)
        inv = 1.0 / jnp.where(l > 0.0, l, 1.0)
        oc_scr[g] = _dot(cvt_ref[g * hd:(g + 1) * hd, :], p) * inv
        pc = p * inv
        imp = pc[:, :TQ]
        for h in range(1, NSA_HPG):
            imp = imp + pc[:, h * TQ:(h + 1) * TQ]
        imp_scr[:, g * TQ:(g + 1) * TQ] = imp
    msel16 = msel_ref[...].astype(BF16)
    pslc = sum(_dot(msel16, piece) for piece in _split24(imp_scr[...]))

    jrow = _iota((nsp, 2 * TQ), 0)
    t_gq = t0 + (_iota((1, 2 * TQ), 1) % TQ)
    cur = lax.shift_right_logical(t_gq, 6)
    forced = (jrow == 0) | (jrow == cur) | (jrow == cur - 1)
    score = jnp.where(jrow * SEL_BLOCK <= t_gq, pslc + jnp.where(forced, FORCE_BONUS, 0.0), NEG)
    work_scr[...] = jnp.where(jrow < ns, score, -3e38)
    sel_scr[...] = jnp.zeros(sel_scr.shape, F32)
    jrow_f = jrow.astype(F32)

    def pick(_, carry):
        w = work_scr[...]
        best = _rowmax(w)
        first = jnp.min(jnp.where(w == best, jrow_f, 1e9), axis=0, keepdims=True)
        hit = jrow_f == first
        sel_scr[...] = jnp.where(hit, 1.0, sel_scr[...])
        work_scr[...] = jnp.where(hit, -jnp.inf, w)
        return carry

    lax.fori_loop(0, N_SEL, pick, 0)
    for u in range(nsp // SEL_PER):
        sel3_scr[u] = jnp.where(sel_scr[u * SEL_PER:(u + 1) * SEL_PER, :] > 0.5, 0.0, NEG)

    last = (t0 + TQ + SEL_TK - 1) // SEL_TK - 1
    ml_scr[0:1, :] = jnp.full((1, NQL), NEG, F32)
    ml_scr[1:2, :] = jnp.zeros((1, NQL), F32)
    acc_scr[...] = jnp.zeros(acc_scr.shape, F32)

    ones16 = jnp.ones((16, SEL_TK), BF16)

    def sel_tile(kt, extra):
        s = _dot(ks_ref[pl.ds(pl.multiple_of(kt * SEL_TK, SEL_TK), SEL_TK), :], qbd16).astype(BF16)
        sb = sel3_scr[kt].astype(BF16)
        s = jnp.concatenate(
            [s[jj * SEL_BLOCK:(jj + 1) * SEL_BLOCK, :]
             + jnp.concatenate([heads(sb[jj:jj + 1, :TQ]), heads(sb[jj:jj + 1, TQ:])], axis=1)
             for jj in range(SEL_PER)], axis=0)
        if extra is not None:
            s = s + extra
        m_old = ml_scr[0:1, :]
        m_new = jnp.maximum(m_old, _rowmax(s).astype(F32))
        alpha = jnp.exp2(m_old - m_new)
        p = jnp.exp2(s - m_new.astype(BF16))
        ml_scr[0:1, :] = m_new
        for g in range(NSA_KVH):
            gs = slice(g * gw, (g + 1) * gw)
            vt1 = jnp.concatenate([vst_ref[kt, g * hd:(g + 1) * hd, :], ones16], axis=0)
            pv = _dot(vt1, p[:, gs])
            acc_scr[g] = acc_scr[g] * alpha[:, gs] + pv[:hd, :]
            ml_scr[1:2, gs] = alpha[:, gs] * ml_scr[1:2, gs] + pv[hd:hd + 1, :]

    def sel_body(kt, carry):
        sel_tile(kt, None)
        return carry

    lax.fori_loop(0, last, sel_body, 0)
    krow = _iota((SEL_TK, TQ), 0)
    sel_tile(last, heads(jnp.where(last * SEL_TK + krow <= t_q, 0.0, NEG).astype(BF16), NSA_HEADS))
    l_sel = ml_scr[1:2, :]
    inv_sel = 1.0 / jnp.where(l_sel > 0.0, l_sel, 1.0)

    wrow = _iota((WIN_TK, TQ), 0)
    m = jnp.full((1, NQL), NEG, F32)
    l = jnp.zeros((1, NQL), F32)
    acc_w = [jnp.zeros((hd, gw), F32) for _ in range(NSA_KVH)]
    for w in range(WIN_TILES):
        tile = i + (wt0 - (WIN_TILES - 1) + w)
        idx = tile * WIN_TK + wrow
        wpos = wpos0 + idx
        dlt = t_q - wpos
        ok = (dlt >= 0) & (dlt < WINDOW) & (wpos >= 0) & (idx < n_win) & (tile >= 0)
        s = _dot(kw[w][...], qbd16).astype(BF16) + heads(jnp.where(ok, 0.0, NEG).astype(BF16), NSA_HEADS)
        m_new = jnp.maximum(m, _rowmax(s).astype(F32))
        m_use = _real(m_new)
        alpha = jnp.where(m > 0.5 * NEG, jnp.exp2(m - m_use), 0.0)
        p = jnp.exp2(s - m_use.astype(BF16))
        sums = []
        for g in range(NSA_KVH):
            gs = slice(g * gw, (g + 1) * gw)
            pv = _dot(jnp.concatenate([vw[w][g * hd:(g + 1) * hd, :], ones16[:, :WIN_TK]], axis=0), p[:, gs])
            acc_w[g] = acc_w[g] * alpha[:, gs] + pv[:hd, :]
            sums.append(pv[hd:hd + 1, :])
        l = alpha * l + jnp.concatenate(sums, axis=1)
        m = m_new
    inv_win = 1.0 / jnp.where(l > 0.0, l, 1.0)

    gt = _sigmoid(gl_ref[...]).T
    for jh in range(NSA_HEADS):
        g, h = divmod(jh, NSA_HPG)
        hl = slice(h * TQ, (h + 1) * TQ)
        ls = slice(jh * TQ, (jh + 1) * TQ)
        ot_scr[jh * hd:(jh + 1) * hd, :] = (
            gt[3 * jh:3 * jh + 1, :] * oc_scr[g, :, hl]
            + gt[3 * jh + 1:3 * jh + 2, :] * (acc_scr[g, :, hl] * inv_sel[:, ls])
            + gt[3 * jh + 2:3 * jh + 3, :] * (acc_w[g][:, hl] * inv_win[:, ls]))
    o_ref[...] = ot_scr[...].T


def nsa_attention(cols, ckh, ckl, cvt, ks, vst, kwin, vwt, pos0, n_new, n_buf, n_win):
    b, lq, _ = cols.shape
    q = gl = cols
    tk_total = ks.shape[1]
    n_sel_tiles = tk_total // SEL_TK
    nsp = tk_total // SEL_BLOCK
    t_total = pos0 + n_new
    nseg = -(-t_total // CMP_STRIDE)
    nc = nseg - CMP_BLOCK // CMP_STRIDE + 1
    ns = -(-t_total // SEL_BLOCK)
    wt0 = n_buf // WIN_TK
    n_wtiles = kwin.shape[1] // WIN_TK
    j = jnp.arange(nsp)[:, None]
    c = jnp.arange(NCP)[None, :]
    msel = ((c >= 4 * j - 1) & (c <= 4 * j + 3)).astype(F32)

    def wspec(w, vt):
        def imap(bi, i):
            tile = jnp.clip(i + (wt0 - (WIN_TILES - 1) + w), 0, n_wtiles - 1)
            return (bi, tile, 0, 0) if vt else (bi, tile, 0)
        return pl.BlockSpec((None, None, LANE, WIN_TK) if vt else (None, WIN_TK, LANE), imap)

    kern = functools.partial(_nsa_attn_kernel, pos0, nc, ns, wt0, pos0 - n_buf, n_win)
    return pl.pallas_call(
        kern,
        grid=(b, lq // TQ),
        in_specs=[pl.BlockSpec((None, TQ, NSA_DIM), lambda bi, i: (bi, i, 0)),
                  pl.BlockSpec((None, TQ, LANE), lambda bi, i: (bi, i, NSA_GL_TILE)),
                  pl.BlockSpec((None, NCP, LANE), lambda bi, i: (bi, 0, 0)),
                  pl.BlockSpec((None, NCP, LANE), lambda bi, i: (bi, 0, 0)),
                  pl.BlockSpec((None, LANE, NCP), lambda bi, i: (bi, 0, 0)),
                  pl.BlockSpec((None, tk_total, LANE), lambda bi, i: (bi, 0, 0)),
                  pl.BlockSpec((None, n_sel_tiles, LANE, SEL_TK), lambda bi, i: (bi, 0, 0, 0))]
                 + [wspec(w, False) for w in range(WIN_TILES)]
                 + [wspec(w, True) for w in range(WIN_TILES)]
                 + [pl.BlockSpec((nsp, NCP), lambda bi, i: (0, 0))],
        out_specs=pl.BlockSpec((None, TQ, NSA_DIM), lambda bi, i: (bi, i, 0)),
        out_shape=jax.ShapeDtypeStruct((b, lq, NSA_DIM), F32),
        scratch_shapes=[pltpu.VMEM((nsp, 2 * TQ), F32), pltpu.VMEM((nsp, 2 * TQ), F32),
                        pltpu.VMEM((nsp // SEL_PER, SEL_PER, 2 * TQ), F32),
                        pltpu.VMEM((NCP, 2 * TQ), F32),
                        pltpu.VMEM((NSA_KVH, NSA_HD, NSA_HPG * TQ), F32),
                        pltpu.VMEM((NSA_KVH, NSA_HD, NSA_HPG * TQ), F32),
                        pltpu.VMEM((8, NQL), F32), pltpu.VMEM((NSA_DIM, TQ), F32)],
        compiler_params=_cparams(("arbitrary", "arbitrary")),
        name="nsa_attention",
    )(q, gl, ckh, ckl, cvt, ks, vst, *([kwin] * WIN_TILES), *([vwt] * WIN_TILES), msel)


TQS = 8


def _nsa_attn_small_kernel(pos0, nc, ns, wpos0, n_win, q_ref, gl_ref, ckh_ref, ckl_ref, cvt_ref, ks_ref,
                           vst_ref, kw_ref, vw_ref, msel_ref, eh_ref, eht_ref, gx_ref, o_ref,
                           sel_scr, work_scr, sel3_scr):
    hd = NSA_HD
    nsp = sel_scr.shape[0]
    lane8 = _iota((TQS, LANE), 1)
    lane = _iota((1, LANE), 1)
    t_lane = pos0 + (lane % TQS)
    grp1 = lane >= NSA_HPG * TQS

    q = q_ref[...] * (hd ** -0.5 * LOG2E)
    pieces = []
    for jh in range(NSA_HEADS):
        tile = q[:, (jh // 2) * LANE:(jh // 2 + 1) * LANE]
        dst_hi = jh >= NSA_HPG
        if (jh % 2 == 1) != dst_hi:
            tile = pltpu.roll(tile, hd, 1)
        pieces.append(jnp.where((lane8 >= hd) if dst_hi else (lane8 < hd), tile, 0.0))
    z = jnp.concatenate(pieces, axis=0).T
    z16 = z.astype(BF16)
    zh, zl = _split16(z)

    def own_group(full):
        return jnp.where(grp1, full[hd:, :], full[:hd, :])

    c_idx = _iota((NCP, LANE), 0)
    ckh = ckh_ref[...]
    s = _dot(ckh, zh) + _dot(ckh, zl) + _dot(ckl_ref[...], zh)
    s = s + jnp.where((CMP_STRIDE * c_idx + (CMP_BLOCK - 1) <= t_lane) & (c_idx < nc), 0.0, NEG)
    p = jnp.exp2(s - _real(_rowmax(s)))
    l = _rowsum(p)
    inv = 1.0 / jnp.where(l > 0.0, l, 1.0)
    o_cmp = own_group(_dot(cvt_ref[...], p) * inv)
    pslc = _dot_hi(msel_ref[...], _dot_hi(p * inv, eht_ref[...]))

    jrow = _iota((nsp, LANE), 0)
    cur = lax.shift_right_logical(t_lane, 6)
    forced = (jrow == 0) | (jrow == cur) | (jrow == cur - 1)
    score = jnp.where(jrow * SEL_BLOCK <= t_lane, pslc + jnp.where(forced, FORCE_BONUS, 0.0), NEG)
    work_scr[...] = jnp.where(jrow < ns, score, -3e38)
    sel_scr[...] = jnp.zeros(sel_scr.shape, F32)
    jrow_f = jrow.astype(F32)

    def pick(_, carry):
        w = work_scr[...]
        best = _rowmax(w)
        first = jnp.min(jnp.where(w == best, jrow_f, 1e9), axis=0, keepdims=True)
        hit = jrow_f == first
        sel_scr[...] = jnp.where(hit, 1.0, sel_scr[...])
        work_scr[...] = jnp.where(hit, -jnp.inf, w)
        return carry

    lax.fori_loop(0, N_SEL, pick, 0)
    on = _dot(sel_scr[...], eh_ref[...])
    for u in range(nsp // SEL_PER):
        sel3_scr[u] = jnp.where(on[u * SEL_PER:(u + 1) * SEL_PER, :] > 0.5, 0.0, NEG)

    last = (pos0 + TQS + SEL_TK - 1) // SEL_TK - 1
    krow = _iota((SEL_TK, LANE), 0)

    def sel_tile(kt, carry, extra):
        m, l, acc = carry
        s = _dot(ks_ref[pl.ds(pl.multiple_of(kt * SEL_TK, SEL_TK), SEL_TK), :], z16)
        sb = sel3_scr[kt]
        s = jnp.concatenate([s[jj * SEL_BLOCK:(jj + 1) * SEL_BLOCK, :] + sb[jj:jj + 1, :]
                             for jj in range(SEL_PER)], axis=0)
        if extra is not None:
            s = s + extra
        m_new = jnp.maximum(m, _rowmax(s))
        alpha = jnp.exp2(m - m_new)
        p = jnp.exp2(s - m_new)
        return m_new, alpha * l + _rowsum(p), acc * alpha + _dot(vst_ref[kt], p)

    init = (jnp.full((1, LANE), NEG, F32), jnp.zeros((1, LANE), F32), jnp.zeros((2 * hd, LANE), F32))

    def two_tiles(u, carry):
        return sel_tile(2 * u, carry[0], None), sel_tile(2 * u + 1, carry[1], None)

    ca, cb = lax.fori_loop(0, last // 2, two_tiles, (init, init))
    if last % 2:
        ca = sel_tile(last - 1, ca, None)
    (ma, la, acca), (mb, lb, accb) = sel_tile(last, ca, jnp.where(last * SEL_TK + krow <= t_lane, 0.0, NEG)), cb
    m = jnp.maximum(ma, mb)
    fa, fb = jnp.exp2(ma - m), jnp.exp2(mb - m)
    l = la * fa + lb * fb
    o_sel = own_group((acca * fa + accb * fb) * (1.0 / jnp.where(l > 0.0, l, 1.0)))

    wrow = _iota((WIN_TK, LANE), 0)
    m = jnp.full((1, LANE), NEG, F32)
    l = jnp.zeros((1, LANE), F32)
    acc = jnp.zeros((2 * hd, LANE), F32)
    for w in range(kw_ref.shape[0] // WIN_TK):
        idx = w * WIN_TK + wrow
        wpos = wpos0 + idx
        dlt = t_lane - wpos
        ok = (dlt >= 0) & (dlt < WINDOW) & (wpos >= 0) & (idx < n_win)
        s = _dot(kw_ref[w * WIN_TK:(w + 1) * WIN_TK, :], z16) + jnp.where(ok, 0.0, NEG)
        m_new = jnp.maximum(m, _rowmax(s))
        alpha = jnp.where(m > 0.5 * NEG, jnp.exp2(m - _real(m_new)), 0.0)
        p = jnp.exp2(s - _real(m_new))
        l = alpha * l + _rowsum(p)
        acc = acc * alpha + _dot(vw_ref[w], p)
        m = m_new
    o_win = own_group(acc * (1.0 / jnp.where(l > 0.0, l, 1.0)))

    def to_rows(o):
        ot = jnp.concatenate([o, o], axis=0).T
        tiles = []
        for k in range(NSA_HEADS // 2):
            a = ot[(2 * k) * TQS:(2 * k + 1) * TQS, :]
            b = ot[(2 * k + 1) * TQS:(2 * k + 2) * TQS, :]
            tiles.append(jnp.where(lane8 < hd, a, b))
        return jnp.concatenate(tiles, axis=1)

    sg = _sigmoid(gl_ref[...])
    o_ref[...] = (_dot_hi(sg, gx_ref[0]) * to_rows(o_cmp) + _dot_hi(sg, gx_ref[1]) * to_rows(o_sel)
                  + _dot_hi(sg, gx_ref[2]) * to_rows(o_win))


def nsa_attention_small(cols, ckh, ckl, cvt, ks, vst, kwin, vwt, pos0, n_buf, n_win):
    b, lq, _ = cols.shape
    assert lq == TQS
    tk_total = ks.shape[1]
    nsp = tk_total // SEL_BLOCK
    t_total = pos0 + lq
    nc = -(-t_total // CMP_STRIDE) - CMP_BLOCK // CMP_STRIDE + 1
    ns = -(-t_total // SEL_BLOCK)
    j = jnp.arange(nsp)[:, None]
    c = jnp.arange(NCP)[None, :]
    msel = ((c >= 4 * j - 1) & (c <= 4 * j + 3)).astype(F32)
    n = jnp.arange(LANE)
    head, qi = n // TQS, n % TQS
    gq = (head // NSA_HPG) * TQS + qi
    eh = (jnp.arange(LANE)[:, None] == gq[None, :]).astype(F32)
    col = jnp.arange(NSA_DIM) // NSA_HD
    gx = jnp.stack([(jnp.arange(LANE)[:, None] == (3 * col + br)[None, :]).astype(F32) for br in range(3)])
    full = lambda a: pl.BlockSpec(a.shape, lambda bi: (0,) * a.ndim)
    per_b = lambda a: pl.BlockSpec((None,) + a.shape[1:], lambda bi: (bi,) + (0,) * (a.ndim - 1))
    kern = functools.partial(_nsa_attn_small_kernel, pos0, nc, ns, pos0 - n_buf, n_win)
    return pl.pallas_call(
        kern,
        grid=(b,),
        in_specs=[pl.BlockSpec((None, TQS, NSA_DIM), lambda bi: (bi, 0, 0)),
                  pl.BlockSpec((None, TQS, LANE), lambda bi: (bi, 0, NSA_GL_TILE)),
                  per_b(ckh), per_b(ckl), per_b(cvt), per_b(ks), per_b(vst), per_b(kwin), per_b(vwt),
                  full(msel), full(eh), full(eh), full(gx)],
        out_specs=pl.BlockSpec((None, TQS, NSA_DIM), lambda bi: (bi, 0, 0)),
        out_shape=jax.ShapeDtypeStruct((b, TQS, NSA_DIM), F32),
        scratch_shapes=[pltpu.VMEM((nsp, LANE), F32), pltpu.VMEM((nsp, LANE), F32),
                        pltpu.VMEM((nsp // SEL_PER, SEL_PER, LANE), F32)],
        compiler_params=_cparams(("arbitrary",)),
        name="nsa_attention_small",
    )(cols, cols, ckh, ckl, cvt, ks, vst, kwin, vwt, msel, eh, eh.T, gx)


def _layer_norm(v, g, b):
    mu = jnp.mean(v, axis=-1, keepdims=True)
    vc = v - mu
    var = jnp.mean(vc * vc, axis=-1, keepdims=True)
    return vc * lax.rsqrt(var + LN_EPS) * g + b


def _merge_kernel(x_ref, ya_ref, yb_ref, yc_ref, gate_ref, g1_ref, wa_ref, wb_ref, wc_ref, wo_ref,
                  lg_ref, lb_ref, o_ref):
    d = D_MODEL
    merged = (_sigmoid(gate_ref[:, :d]) * _dot(ya_ref[...], wa_ref[...])
              + _sigmoid(gate_ref[:, d:2 * d]) * _dot(yb_ref[...], wb_ref[...])
              + _sigmoid(gate_ref[:, 2 * d:]) * _dot(yc_ref[...], wc_ref[...]))
    o = _dot(merged, wo_ref[...])
    o_ref[...] = _layer_norm(ALPHA * x_ref[...] + g1_ref[...] * o, lg_ref[...], lb_ref[...])


def merge_out(x, ya, yb, yc, gate, g1, wa, wb, wc, wo, lg, lb, tm=256):
    m, d = x.shape
    tm = min(tm, m)
    per_row = g1.shape[0] != 1
    row = lambda w: pl.BlockSpec((tm, w), lambda i: (i, 0))
    mspec = row(d) if per_row else pl.BlockSpec((1, d), lambda i: (0, 0))
    wspec = pl.BlockSpec((d, d), lambda i: (0, 0))
    vspec = pl.BlockSpec((1, d), lambda i: (0, 0))
    return pl.pallas_call(
        _merge_kernel,
        grid=(m // tm,),
        in_specs=[row(d), row(d), row(d), row(d), row(3 * d), mspec, wspec, wspec, wspec, wspec, vspec, vspec],
        out_specs=row(d),
        out_shape=jax.ShapeDtypeStruct((m, d), F32),
        compiler_params=_cparams(("arbitrary",)),
        name="merge_out",
    )(x, ya, yb, yc, gate, g1, wa, wb, wc, wo, lg, lb)


def _lane_first(mask, lane_f):
    return jnp.min(jnp.where(mask, lane_f, 1e9), axis=-1, keepdims=True)


def _moe_kernel(x_ref, sc_ref, sh_ref, g2_ref, wr_ref, w1_ref, w3_ref, w2_ref, lg_ref, lb_ref,
                o_ref, u_scr, gate_scr, acc_scr):
    e = pl.program_id(1)
    tm = x_ref.shape[0]
    lane = _iota((tm, LANE), 1)
    lane_f = lane.astype(F32)

    @pl.when(e == 0)
    def _():
        u = x_ref[...] * (1.0 + sc_ref[...]) + sh_ref[...]
        u_scr[...] = u.astype(BF16)
        logits = _dot_hi(u, wr_ref[...])
        lg = jnp.where(lane < N_GROUPS, logits, -jnp.inf)
        gmax = jnp.max(lg, axis=-1, keepdims=True)
        gstar = _lane_first(lg == gmax, lane_f)
        pg = 1.0 / jnp.sum(jnp.exp(lg - gmax), axis=-1, keepdims=True)
        in_grp = (lane >= N_GROUPS) & (lane < N_GROUPS + N_EXPERTS) & (
            lax.shift_right_logical(lane - N_GROUPS, 2).astype(F32) == gstar)
        le = jnp.where(in_grp, logits, -jnp.inf)
        v1 = jnp.max(le, axis=-1, keepdims=True)
        i1 = _lane_first(le == v1, lane_f)
        le2 = jnp.where(lane_f == i1, -jnp.inf, le)
        v2 = jnp.max(le2, axis=-1, keepdims=True)
        i2 = _lane_first(le2 == v2, lane_f)
        e2 = jnp.exp(v2 - v1)
        den = 1.0 / (1.0 + e2)
        gate_scr[...] = jnp.where(lane_f == i1, den * pg, jnp.where(lane_f == i2, e2 * den * pg, 0.0))
        acc_scr[...] = jnp.zeros(acc_scr.shape, F32)

    u = u_scr[...]
    gate = gate_scr[...]
    out = acc_scr[...]
    for k in range(MOE_EPS):
        ge = jnp.sum(jnp.where(lane == e * MOE_EPS + (k + N_GROUPS), gate, 0.0), axis=-1, keepdims=True)
        h = _silu(_dot(u, w1_ref[k])) * _dot(u, w3_ref[k])
        out = out + _dot(h * ge, w2_ref[k])
    acc_scr[...] = out

    @pl.when(e == pl.num_programs(1) - 1)
    def _():
        o_ref[...] = _layer_norm(ALPHA * x_ref[...] + g2_ref[...] * acc_scr[...], lg_ref[...], lb_ref[...])


MOE_EPS = 1


def moe_out(x, sc, sh, g2, wr, w1, w3, w2, lg, lb, tm=512):
    m, d = x.shape
    tm = min(tm, m)
    per_row = sc.shape[0] != 1
    row = pl.BlockSpec((tm, d), lambda i, e: (i, 0))
    mspec = row if per_row else pl.BlockSpec((1, d), lambda i, e: (0, 0))
    vspec = pl.BlockSpec((1, d), lambda i, e: (0, 0))
    return pl.pallas_call(
        _moe_kernel,
        grid=(m // tm, N_EXPERTS // MOE_EPS),
        in_specs=[row, mspec, mspec, mspec, pl.BlockSpec((d, LANE), lambda i, e: (0, 0)),
                  pl.BlockSpec((MOE_EPS, d, EXPERT_HIDDEN), lambda i, e: (e, 0, 0)),
                  pl.BlockSpec((MOE_EPS, d, EXPERT_HIDDEN), lambda i, e: (e, 0, 0)),
                  pl.BlockSpec((MOE_EPS, EXPERT_HIDDEN, d), lambda i, e: (e, 0, 0)), vspec, vspec],
        out_specs=row,
        out_shape=jax.ShapeDtypeStruct((m, d), F32),
        scratch_shapes=[pltpu.VMEM((tm, d), BF16), pltpu.VMEM((tm, LANE), F32), pltpu.VMEM((tm, d), F32)],
        compiler_params=_cparams(("arbitrary", "arbitrary")),
        name="moe_out",
    )(x, sc, sh, g2, wr, w1, w3, w2, lg, lb)


def _pad_rows(a, n):
    return a if a.shape[1] == n else jnp.pad(a, ((0, 0), (0, n - a.shape[1]), (0, 0)))


def _layer_weights(l, p):
    w_in = p['w_in'][l]
    o1, o2, o3 = SSD_COLS, SSD_COLS + RWKV_COLS, SSD_COLS + RWKV_COLS + NSA_COLS
    padc = lambda w, n: jnp.pad(w, ((0, 0), (0, n - w.shape[1])))
    w_ssd = padc(w_in[:, :o1], SSD_W)
    w_nsa = padc(w_in[:, o2:o3], NSA_W)
    z64 = jnp.zeros((64, RWKV_DIM), F32)
    seg = _head_onehot(RWKV_HEADS, RWKV_HD, LANE).T
    rwkv_prm = (p['rwkv_mu'][l][None], p['rwkv_w0'][l][None], p['rwkv_a0'][l][None], p['rwkv_k_k'][l][None],
                p['rwkv_k_a'][l][None], p['rwkv_r_k'][l].reshape(1, RWKV_DIM),
                jnp.concatenate([p['rwkv_w2'][l], z64], 0), jnp.concatenate([z64, p['rwkv_a2'][l]], 0),
                p['rwkv_g2'][l], seg, seg.T)
    wr = jnp.pad(jnp.concatenate([p['w_group'][l], p['w_router'][l]], axis=1),
                 ((0, 0), (0, LANE - N_GROUPS - N_EXPERTS)))
    row = lambda v: v[None]
    return dict(
        w_ssd=w_ssd.astype(BF16), w_rwkv=w_in[:, o1:o2].astype(BF16), w_nsa=w_nsa.astype(BF16),
        w_gate=w_in[:, o3:].astype(BF16),
        ssd=(jnp.pad(p['ssd_conv_w'][l], ((0, 4), (0, 0))), row(p['ssd_conv_b'][l]),
             row(jnp.pad(p['ssd_dt_bias'][l], (0, LANE - SSD_HEADS))), row(jnp.pad(p['ssd_a_log'][l], (0, LANE - SSD_HEADS))),
             row(jnp.repeat(p['ssd_d'][l], SSD_HD)), row(p['ssd_norm_w'][l])),
        rwkv=rwkv_prm, lnx=(row(p['rwkv_lnx_w'][l]), row(p['rwkv_lnx_b'][l])),
        cmp=_cmp_weights(p['cmp_pe'][l], p['cmp_w1'][l], p['cmp_w2'][l]),
        wo=tuple(p[k][l].astype(BF16) for k in ('w_o_ssd', 'w_o_rwkv', 'w_o_nsa', 'w_out')),
        ln1=(row(p['ln1_g'][l]), row(p['ln1_b'][l])), ln2=(row(p['ln2_g'][l]), row(p['ln2_b'][l])),
        wr=wr, w1=p['moe_w1'][l].astype(BF16), w3=p['moe_w3'][l].astype(BF16), w2=p['moe_w2'][l].astype(BF16))


def _trunk_layer(x, mod, l, w, pos0, conv_st, ssm_st, shift_st, rwkv_st, cache_win, cache_cmp, cache_sel,
                 page_table):
    b, L, d = x.shape
    m = b * L
    x2 = x.reshape(m, d)
    sh1, sc1, g1, sh2, sc2, g2 = [mod[:, k * d:(k + 1) * d] for k in range(6)]
    if b > 1:
        sh1, sc1, g1, sh2, sc2, g2 = [jnp.repeat(t, L, axis=0) for t in (sh1, sc1, g1, sh2, sc2, g2)]
    proj = lambda wt: mod_proj(x2, sc1, sh1, wt).reshape(b, L, -1)
    c_ssd, c_rwkv, c_nsa, c_gate = proj(w['w_ssd']), proj(w['w_rwkv']), proj(w['w_nsa']), proj(w['w_gate'])

    lp = -(-L // SSD_Q) * SSD_Q
    cst8 = jnp.pad(conv_st, ((0, 0), (8 - (SSD_CONV - 1), 0), (0, 0)))
    h0t = jnp.transpose(ssm_st.reshape(b, SSD_INNER, SSD_STATE), (0, 2, 1))
    y_a, ht = ssd_mixer(_pad_rows(c_ssd, lp), cst8, h0t, L, *w['ssd'])
    y_a = y_a[:, :L]
    ssm_new = jnp.transpose(ht, (0, 2, 1)).reshape(b, SSD_HEADS, SSD_HD, SSD_STATE)
    conv_new = c_ssd[:, L - (SSD_CONV - 1):, SSD_INNER:SSD_INNER + SSD_CONV_DIM]

    lp = -(-L // RWKV_C) * RWKV_C
    sh8 = jnp.pad(shift_st[:, None, :], ((0, 0), (7, 0), (0, 0)))
    pre = rwkv_prep(_pad_rows(c_rwkv, lp), sh8, L, w['rwkv'])
    y_b, sp = rwkv_scan(*pre, _pair_blockdiag(rwkv_st), *w['lnx'])
    y_b = y_b[:, :L]
    rwkv_new = _pair_unblock(sp)
    shift_new = c_rwkv[:, -1]

    pek, pev, wk, wv, w2k, w2v = w['cmp']
    kvrow = (2, NSA_KVH, NSA_HD)
    rows = lambda tile: c_nsa[:, :, tile * LANE:(tile + 2) * LANE]
    if cache_cmp is None:
        hp = cmp_h_rows(c_nsa, wk, wv, NSA_CMP_TILE)
        hp = _pad_rows(hp, NCP)
        hn = jnp.zeros((b, 8, CMP_HW), F32)
        ks, vst = kvprep_rows(c_nsa, SEL_TK, NSA_SEL_TILE)
        kwin, vwt = kvprep_rows(c_nsa, WIN_TK, NSA_WIN_TILE)
        cq, n_buf, n_win = c_nsa, 0, L
        win_new = rows(NSA_WIN_TILE)[:, max(L - WINDOW, 0):]
    else:
        cq = _pad_rows(c_nsa, TQ)
        hp = _pad_rows(cmp_h_pages(cache_cmp, page_table, l, wk, wv), NCP)
        hn = cmp_h_rows(cq, wk, wv, NSA_CMP_TILE)
        ks, vst = kvprep_pages(cache_sel, page_table, l, cq, NSA_SEL_TILE)
        n_buf = cache_win.shape[1]
        win_all = jnp.concatenate([cache_win.reshape(b, n_buf, 2 * LANE), rows(NSA_WIN_TILE)], axis=1)
        n_win = n_buf + L
        kwin, vwt = kvprep_rows(_pad_rows(win_all, -(-(n_buf + TQ) // WIN_TK) * WIN_TK), WIN_TK)
        win_new = win_all[:, n_win - min(WINDOW, n_win):]
    hn_row = NCP - 1 if cache_cmp is None else pos0 // CMP_STRIDE - 1
    ckh, ckl, cvt = cmp_finish(hp, hn, hn_row, pek, pev, wk, wv, w2k, w2v)
    if L == TQS:
        y_c = nsa_attention_small(c_nsa, ckh, ckl, cvt, ks, vst, kwin, vwt, pos0, n_buf, n_win)
    else:
        y_c = nsa_attention(cq, ckh, ckl, cvt, ks, vst, kwin, vwt, pos0, L, n_buf, n_win)[:, :L]
    cmp_rows = rows(NSA_CMP_TILE).reshape((b, L) + kvrow)
    sel_rows = rows(NSA_SEL_TILE).reshape((b, L) + kvrow)
    win_new = win_new.reshape(win_new.shape[:2] + kvrow)

    flat = lambda t: t.reshape(m, -1)
    x1 = merge_out(x2, flat(y_a), flat(y_b), flat(y_c), flat(c_gate), g1, *w['wo'], *w['ln1'])
    x_out = moe_out(x1, sc2, sh2, g2, w['wr'], w['w1'], w['w3'], w['w2'], *w['ln2'])
    return x_out.reshape(b, L, d), (cmp_rows, sel_rows, win_new, ssm_new, conv_new, rwkv_new, shift_new)


def kernel(x_prompt, x_sample, c_prompt, c_sample, cache_cmp, cache_sel, cache_win, state_ssm, state_ssm_conv,
           state_rwkv, state_rwkv_shift, page_table, w_ada, b_ada, w_in, ssd_conv_w, ssd_conv_b, ssd_dt_bias,
           ssd_a_log, ssd_d, ssd_norm_w, rwkv_mu, rwkv_w0, rwkv_w2, rwkv_a0, rwkv_a2, rwkv_g2, rwkv_k_k, rwkv_k_a,
           rwkv_r_k, rwkv_lnx_w, rwkv_lnx_b, cmp_pe, cmp_w1, cmp_w2, w_o_ssd, w_o_rwkv, w_o_nsa, w_out, ln1_g,
           ln1_b, ln2_g, ln2_b, w_group, w_router, moe_w1, moe_w3, moe_w2):
    p = dict(w_in=w_in, ssd_conv_w=ssd_conv_w, ssd_conv_b=ssd_conv_b, ssd_dt_bias=ssd_dt_bias, ssd_a_log=ssd_a_log,
             ssd_d=ssd_d, ssd_norm_w=ssd_norm_w, rwkv_mu=rwkv_mu, rwkv_w0=rwkv_w0, rwkv_w2=rwkv_w2, rwkv_a0=rwkv_a0,
             rwkv_a2=rwkv_a2, rwkv_g2=rwkv_g2, rwkv_k_k=rwkv_k_k, rwkv_k_a=rwkv_k_a, rwkv_r_k=rwkv_r_k,
             rwkv_lnx_w=rwkv_lnx_w, rwkv_lnx_b=rwkv_lnx_b, cmp_pe=cmp_pe, cmp_w1=cmp_w1, cmp_w2=cmp_w2,
             w_o_ssd=w_o_ssd, w_o_rwkv=w_o_rwkv, w_o_nsa=w_o_nsa, w_out=w_out, ln1_g=ln1_g, ln1_b=ln1_b,
             ln2_g=ln2_g, ln2_b=ln2_b, w_group=w_group, w_router=w_router, moe_w1=moe_w1, moe_w3=moe_w3,
             moe_w2=moe_w2)
    bp, bs = x_prompt.shape[0], x_sample.shape[0]
    past_len = page_table.shape[1] * PAGE
    nb = -(-(bp + bs) // SUBLANE) * SUBLANE
    c_all = jnp.pad(jnp.concatenate([c_prompt, c_sample], axis=0), ((0, nb - bp - bs), (0, 0)))
    mod = ada_mod(c_all, w_ada, b_ada)
    n_phys = cache_cmp.shape[1]
    cmp_pages = cache_cmp.reshape(DEPTH, n_phys, PAGE, 2 * LANE)
    sel_pages = cache_sel.reshape(DEPTH, n_phys, PAGE, 2 * LANE)
    zeros = lambda *s: jnp.zeros(s, F32)
    xp, xs = x_prompt, x_sample
    st_p, st_s = [], []
    for l in range(DEPTH):
        w = _layer_weights(l, p)
        xp, sp_l = _trunk_layer(xp, mod[l, :bp], l, w, 0, zeros(bp, SSD_CONV - 1, SSD_CONV_DIM),
                                zeros(bp, SSD_HEADS, SSD_HD, SSD_STATE), zeros(bp, RWKV_COLS),
                                zeros(bp, RWKV_HEADS, RWKV_HD, RWKV_HD), None, None, None, None)
        xs, ss_l = _trunk_layer(xs, mod[l, bp:bp + bs], l, w, past_len, state_ssm_conv[l], state_ssm[l],
                                state_rwkv_shift[l], state_rwkv[l], cache_win[l], cmp_pages, sel_pages, page_table)
        st_p.append(sp_l)
        st_s.append(ss_l)
    sp = [jnp.stack(z) for z in zip(*st_p)]
    ss = [jnp.stack(z) for z in zip(*st_s)]
    return (xp, xs, sp[0], sp[1], sp[2], sp[3], sp[4], sp[5], sp[6], ss[0], ss[1], ss[2], ss[3], ss[4], ss[5], ss[6])
```

```python
import functools
import math

import jax
import jax.numpy as jnp
from jax import lax
from jax.experimental import pallas as pl
from jax.experimental.pallas import tpu as pltpu

F32 = jnp.float32
BF16 = jnp.bfloat16
HIGHEST = lax.Precision.HIGHEST

D_MODEL = 1024
DEPTH = 2
PAGE = 128
SSD_HEADS, SSD_HD, SSD_GROUPS, SSD_STATE, SSD_CONV = 16, 64, 2, 128, 4
SSD_INNER = SSD_HEADS * SSD_HD
SSD_CONV_DIM = SSD_INNER + 2 * SSD_GROUPS * SSD_STATE
SSD_COLS = SSD_INNER + SSD_CONV_DIM + SSD_HEADS
RWKV_HEADS, RWKV_HD = 16, 64
RWKV_DIM = RWKV_HEADS * RWKV_HD
RWKV_COLS = 3 * RWKV_DIM + 64 + 64 + 128
RWKV_LNX_EPS = 64e-5
NSA_HEADS, NSA_KVH, NSA_HPG, NSA_HD = 16, 2, 8, 64
NSA_DIM = NSA_HEADS * NSA_HD
NSA_COLS = NSA_DIM + 3 * 2 * NSA_KVH * NSA_HD + 3 * NSA_HEADS
CMP_BLOCK, CMP_STRIDE, CMP_HIDDEN = 32, 16, 128
SEL_BLOCK, N_SEL, WINDOW = 64, 16, 512
FORCE_BONUS = 1e4
N_GROUPS, EPG, N_EXPERTS, EXPERT_HIDDEN = 4, 4, 16, 256
ALPHA = (2 * DEPTH) ** 0.25
LN_EPS = 1e-5
RMS_EPS = 1e-5
NEG = -1e30

LANE = 128
SUBLANE = 8
VMEM_LIMIT = 56 * 1024 * 1024


def _cparams(sem):
    return pltpu.CompilerParams(dimension_semantics=sem, vmem_limit_bytes=VMEM_LIMIT)


def _dot(a, b):
    return jnp.dot(a.astype(BF16), b.astype(BF16), preferred_element_type=F32)


def _dot_hi(a, b):
    return jnp.dot(a, b, precision=HIGHEST, preferred_element_type=F32)


def _dot_nt(a, b):
    return lax.dot_general(a.astype(BF16), b.astype(BF16), (((1,), (1,)), ((), ())),
                           preferred_element_type=F32)


def _dot_nt_hi(a, b):
    return lax.dot_general(a, b, (((1,), (1,)), ((), ())), precision=HIGHEST,
                           preferred_element_type=F32)


def _split16(a):
    hi = a.astype(BF16)
    return hi, (a - hi.astype(F32)).astype(BF16)


def _split24(a):
    h1 = a.astype(BF16)
    r1 = a - h1.astype(F32)
    h2 = r1.astype(BF16)
    return h1, h2, (r1 - h2.astype(F32)).astype(BF16)


def _dot_w01(x, w16):
    return sum(_dot(piece, w16) for piece in _split24(x))


def _dot3(a, b):
    (ah, al), (bh, bl) = a, b
    return _dot(jnp.concatenate([ah, ah, al], axis=1), jnp.concatenate([bh, bl, bh], axis=0))


def _dot3_nt(a, b):
    (ah, al), (bh, bl) = a, b
    return _dot_nt(ah, bh) + _dot_nt(ah, bl) + _dot_nt(al, bh)


def _sigmoid(x):
    return 1.0 / (1.0 + jnp.exp(-x))


def _silu(x):
    return x * _sigmoid(x)


def _softplus(x):
    return jnp.maximum(x, 0.0) + jnp.log(1.0 + jnp.exp(-jnp.abs(x)))


def _iota(shape, dim):
    return lax.broadcasted_iota(jnp.int32, shape, dim)


def _head_onehot(n_heads, hd, pad_rows):
    r = jnp.arange(pad_rows)[:, None]
    c = jnp.arange(n_heads * hd)[None, :] // hd
    return (r == c).astype(F32)


def _ada_kernel(c_ref, w_ref, b_ref, o_ref):
    o_ref[...] = _dot_hi(_silu(c_ref[...]), w_ref[...]) + b_ref[...]


def ada_mod(c_all, w_ada, b_ada):
    nb = c_all.shape[0]
    return pl.pallas_call(
        _ada_kernel,
        grid=(DEPTH, 6),
        in_specs=[pl.BlockSpec((nb, D_MODEL), lambda l, j: (0, 0)),
                  pl.BlockSpec((None, D_MODEL, D_MODEL), lambda l, j: (l, 0, j)),
                  pl.BlockSpec((None, 1, D_MODEL), lambda l, j: (l, 0, j))],
        out_specs=pl.BlockSpec((None, nb, D_MODEL), lambda l, j: (l, 0, j)),
        out_shape=jax.ShapeDtypeStruct((DEPTH, nb, 6 * D_MODEL), F32),
        compiler_params=_cparams(("arbitrary", "arbitrary")),
        name="ada_mod",
    )(c_all, w_ada, b_ada.reshape(DEPTH, 1, 6 * D_MODEL))


def _inproj_kernel(nw, x_ref, sc_ref, sh_ref, *refs):
    u = (x_ref[...] * (1.0 + sc_ref[...]) + sh_ref[...]).astype(BF16)
    for w_ref, o_ref in zip(refs[:nw], refs[nw:]):
        o_ref[...] = _dot(u, w_ref[...])


def mod_proj(x, sc, sh, ws, tm=256):
    m, k = x.shape
    tm = min(tm, m)
    per_row = sc.shape[0] != 1
    mspec = (pl.BlockSpec((tm, k), lambda i: (i, 0)) if per_row
             else pl.BlockSpec((1, k), lambda i: (0, 0)))
    return pl.pallas_call(
        functools.partial(_inproj_kernel, len(ws)),
        grid=(m // tm,),
        in_specs=[pl.BlockSpec((tm, k), lambda i: (i, 0)), mspec, mspec]
                 + [pl.BlockSpec((k, w.shape[1]), lambda i: (0, 0), pipeline_mode=pl.Buffered(1)) for w in ws],
        out_specs=[pl.BlockSpec((tm, w.shape[1]), lambda i: (i, 0)) for w in ws],
        out_shape=[jax.ShapeDtypeStruct((m, w.shape[1]), F32) for w in ws],
        compiler_params=_cparams(("arbitrary",)),
        name="mod_proj",
    )(x, sc, sh, *ws)


SSD_Q = 128
SSD_W = SSD_INNER + SSD_CONV_DIM + LANE


def _ssd_kernel(n_valid, zxd_ref, cst_ref, h0_ref, cw_ref, cb_ref, dtb_ref, alog_ref, dexp_ref,
                nw_ref, e16_ref, tri_ref, y_ref, hout_ref, ext, hT):
    j = pl.program_id(1)
    q = SSD_Q

    @pl.when(j == 0)
    def _():
        ext[0:8, :] = cst_ref[...]
        hT[...] = h0_ref[...]

    ext[8:8 + q, :] = zxd_ref[:, SSD_INNER:SSD_INNER + SSD_CONV_DIM]
    conv = (cb_ref[...] + ext[5:5 + q, :] * cw_ref[0:1, :] + ext[6:6 + q, :] * cw_ref[1:2, :]
            + ext[7:7 + q, :] * cw_ref[2:3, :] + ext[8:8 + q, :] * cw_ref[3:4, :])
    ext[0:8, :] = ext[q:q + 8, :]
    xbc = _silu(conv)
    xs = xbc[:, :SSD_INNER]
    bm = xbc[:, SSD_INNER:SSD_INNER + 2 * SSD_STATE]
    cm = xbc[:, SSD_INNER + 2 * SSD_STATE:]

    row = _iota((q, LANE), 0)
    lane = _iota((q, LANE), 1)
    dt = _softplus(zxd_ref[:, SSD_INNER + SSD_CONV_DIM:] + dtb_ref[...])
    dt = jnp.where((lane < SSD_HEADS) & (row + j * q < n_valid), dt, 0.0)
    a = -jnp.exp(alog_ref[...])
    acum = _dot_hi(tri_ref[...], dt * a)
    a_last = acum[q - 1:q, :]
    e16 = e16_ref[...]
    dt_e = _dot_hi(dt, e16)
    ea_e = _dot_hi(jnp.exp(acum), e16)
    dte_e = _dot_hi(jnp.exp(a_last - acum), e16)
    elast_e = ea_e[q - 1:q, :]
    xdt = xs * dt_e
    acum_t = acum.T
    tril = row >= lane
    lo_half = lane < SSD_HD

    y_tiles = []
    for g in range(SSD_GROUPS):
        cc = cm[:, g * SSD_STATE:(g + 1) * SSD_STATE]
        bc = bm[:, g * SSD_STATE:(g + 1) * SSD_STATE]
        cb = _dot_nt(cc, bc)
        gs = slice(g * 512, (g + 1) * 512)
        h_in = hT[:, gs]
        y_off = _dot(cc, h_in) * ea_e[:, gs]
        hT[:, gs] = h_in * elast_e[:, gs] + _dot(bc.T, xdt[:, gs] * dte_e[:, gs])
        for tl in range(4):
            t = 4 * g + tl
            xt = xdt[:, t * LANE:(t + 1) * LANE]
            yd = jnp.zeros((q, LANE), F32)
            for sub in range(2):
                h = 2 * t + sub
                diff = acum[:, h:h + 1] - acum_t[h:h + 1, :]
                m = cb * jnp.exp(jnp.where(tril, diff, NEG))
                yd = yd + _dot(m, jnp.where(lo_half if sub == 0 else ~lo_half, xt, 0.0))
            y_tiles.append(yd + y_off[:, tl * LANE:(tl + 1) * LANE])
    y = jnp.concatenate(y_tiles, axis=1) + dexp_ref[...] * xs
    y = y * _silu(zxd_ref[:, :SSD_INNER])
    outs = []
    for g in range(SSD_GROUPS):
        yg = y[:, g * 512:(g + 1) * 512]
        ms = jnp.sum(yg * yg, axis=-1, keepdims=True) * (1.0 / 512.0)
        outs.append(yg * lax.rsqrt(ms + RMS_EPS))
    y_ref[...] = jnp.concatenate(outs, axis=1) * nw_ref[...]

    @pl.when(j == pl.num_programs(1) - 1)
    def _():
        hout_ref[...] = hT[...]


def ssd_mixer(zxd, conv_st8, h0t, n_valid, cw8, cb, dtb, alog, dexp, nw):
    b, lp, _ = zxd.shape
    nj = lp // SSD_Q
    e16 = _head_onehot(SSD_HEADS, SSD_HD, LANE)
    tri = (jnp.arange(SSD_Q)[:, None] >= jnp.arange(SSD_Q)[None, :]).astype(F32)
    full = lambda shp: pl.BlockSpec(shp, lambda bi, j: (0,) * len(shp))
    return pl.pallas_call(
        functools.partial(_ssd_kernel, n_valid),
        grid=(b, nj),
        in_specs=[pl.BlockSpec((None, SSD_Q, SSD_W), lambda bi, j: (bi, j, 0)),
                  pl.BlockSpec((None, 8, SSD_CONV_DIM), lambda bi, j: (bi, 0, 0)),
                  pl.BlockSpec((None, SSD_STATE, SSD_INNER), lambda bi, j: (bi, 0, 0)),
                  full((8, SSD_CONV_DIM)), full((1, SSD_CONV_DIM)), full((1, LANE)), full((1, LANE)),
                  full((1, SSD_INNER)), full((1, SSD_INNER)), full((LANE, SSD_INNER)),
                  full((SSD_Q, SSD_Q))],
        out_specs=[pl.BlockSpec((None, SSD_Q, SSD_INNER), lambda bi, j: (bi, j, 0)),
                   pl.BlockSpec((None, SSD_STATE, SSD_INNER), lambda bi, j: (bi, 0, 0))],
        out_shape=[jax.ShapeDtypeStruct((b, lp, SSD_INNER), F32),
                   jax.ShapeDtypeStruct((b, SSD_STATE, SSD_INNER), F32)],
        scratch_shapes=[pltpu.VMEM((SSD_Q + 8, SSD_CONV_DIM), F32),
                        pltpu.VMEM((SSD_STATE, SSD_INNER), F32)],
        compiler_params=_cparams(("arbitrary", "arbitrary")),
        name="ssd_mixer",
    )(zxd, conv_st8, h0t, cw8, cb, dtb, alog, dexp, nw, e16, tri)


def _rwkv_prep_kernel(n_valid, tr, x_ref, sh8_ref, mu_ref, w0_ref, a0_ref, kk_ref, ka_ref, rk_ref,
                      w2_ref, a2_ref, g2_ref, seg_ref, e16_ref,
                      r_o, ld_o, k_o, v_o, kk_o, bb_o, g_o, bv_o, ext):
    j = pl.program_id(1)

    @pl.when(j == 0)
    def _():
        ext[0:8, :] = sh8_ref[...]

    x = x_ref[...]
    ext[8:8 + tr, :] = x
    prev = ext[7:7 + tr, :]
    ext[0:8, :] = ext[tr:tr + 8, :]
    xm = x + (prev - x) * mu_ref[...]
    d = RWKV_DIM
    r, k, v = xm[:, :d], xm[:, d:2 * d], xm[:, 2 * d:3 * d]
    lo = xm[:, 3 * d:3 * d + LANE]
    glo = xm[:, 3 * d + LANE:]
    w = w0_ref[...] + _dot3(_split16(jnp.tanh(lo)), _split16(w2_ref[...]))
    ld = -jnp.exp(-_softplus(-w) - 0.5)
    a = _sigmoid(a0_ref[...] + _dot3(_split16(lo), _split16(a2_ref[...])))
    g = _dot(_sigmoid(glo), g2_ref[...])
    seg, e16 = seg_ref[...].astype(BF16), e16_ref[...].astype(BF16)
    kk = k * kk_ref[...]
    ss = _dot_w01(kk * kk, seg)
    kk = kk * _dot_w01(lax.rsqrt(jnp.maximum(ss, 1e-24)), e16)
    k2 = k * (1.0 + (a - 1.0) * ka_ref[...])
    bonus = _dot_w01(_dot_w01(r * k2 * rk_ref[...], seg), e16)
    valid = (_iota((tr, 1), 0) + j * tr) < n_valid
    zero = lambda t: jnp.where(valid, t, 0.0)
    r_o[...] = r
    ld_o[...] = zero(ld)
    k_o[...] = zero(k2)
    v_o[...] = zero(v)
    kk_o[...] = zero(kk)
    bb_o[...] = zero(kk * a)
    g_o[...] = g
    bv_o[...] = bonus * v


def rwkv_prep(cols, shift8, n_valid, prm):
    b, lp, _ = cols.shape
    tr = min(128, lp)
    d = RWKV_DIM
    full = lambda shp: pl.BlockSpec(shp, lambda bi, j: (0,) * len(shp))
    row = lambda: pl.BlockSpec((None, tr, d), lambda bi, j: (bi, j, 0))
    return pl.pallas_call(
        functools.partial(_rwkv_prep_kernel, n_valid, tr),
        grid=(b, lp // tr),
        in_specs=[pl.BlockSpec((None, tr, RWKV_COLS), lambda bi, j: (bi, j, 0)),
                  pl.BlockSpec((None, 8, RWKV_COLS), lambda bi, j: (bi, 0, 0)),
                  full((1, RWKV_COLS))] + [full((1, d))] * 5 + [full((LANE, d))] * 3
                 + [full((d, LANE)), full((LANE, d))],
        out_specs=[row() for _ in range(8)],
        out_shape=[jax.ShapeDtypeStruct((b, lp, d), F32)] * 8,
        scratch_shapes=[pltpu.VMEM((tr + 8, RWKV_COLS), F32)],
        compiler_params=_cparams(("arbitrary", "arbitrary")),
        name="rwkv_prep",
    )(cols, shift8, *prm)


RWKV_C = 64


def _rwkv_scan_kernel(nch, r_ref, ld_ref, k_ref, v_ref, kk_ref, bb_ref, g_ref, bv_ref, s0_ref,
                      lnw_ref, lnb_ref, y_ref, sout_ref, st):
    i = pl.program_id(2)
    c = RWKV_C
    npp = st.shape[0]

    @pl.when(i == 0)
    def _():
        st[...] = s0_ref[...]

    n2 = 2 * c
    rr = _iota((n2, n2), 0)
    cc = _iota((n2, n2), 1)
    eye = (rr == cc).astype(F32)
    upper = rr < cc
    upper_eq = rr <= cc
    tri = (_iota((c, c), 0) >= _iota((c, c), 1)).astype(BF16)
    lane = _iota((c, LANE), 1)
    m0 = lane < RWKV_HD
    hmean = jnp.where((rr < RWKV_HD) == (cc < RWKV_HD), 1.0 / RWKV_HD, 0.0).astype(BF16)

    def stack(t):
        return jnp.concatenate([jnp.where(m0, t, 0.0), jnp.where(m0, 0.0, t)], axis=0)

    def head_mean(t):
        return sum(_dot(piece, hmean) for piece in _split24(t))

    units = [(ci, pi) for ci in range(nch) for pi in range(npp)]
    pre = []
    for ci, pi in units:
        sl = pl.ds(ci * c, c)
        ls = slice(pi * LANE, (pi + 1) * LANE)
        r, ld, k, v, kk, bb = (ref[sl, ls] for ref in (r_ref, ld_ref, k_ref, v_ref, kk_ref, bb_ref))
        cum = sum(_dot(tri, piece) for piece in _split24(ld))
        p_in = jnp.exp(cum)
        p_inv = jnp.exp(-cum)
        p_c = p_in[c - 1:c, :]
        kks = stack(kk * jnp.exp(cum - ld))
        rs = stack(r * p_in)
        bs = stack(bb * p_inv)
        ks = stack(k * p_inv)
        kks16, rs16, bs16, ks16 = (t.astype(BF16) for t in (kks, rs, bs, ks))
        pre.append(dict(
            p_c=p_c, vs_t=_split16(stack(v).T), kks_t=kks.T.astype(BF16), rs_t=rs.T.astype(BF16),
            upd_r=_split16(jnp.concatenate([bs * p_c, ks * p_c], axis=0)),
            nt=jnp.where(upper, _dot_nt(bs16, kks16), 0.0),
            avk=jnp.where(upper, _dot_nt(ks16, kks16), 0.0).astype(BF16),
            arb=jnp.where(upper_eq, _dot_nt(bs16, rs16), 0.0).astype(BF16),
            ark=jnp.where(upper_eq, _dot_nt(ks16, rs16), 0.0).astype(BF16)))
    tinv = [eye - u['nt'] for u in pre]
    pw = [u['nt'].astype(BF16) for u in pre]
    pw = [_dot(w, w).astype(BF16) for w in pw]
    for _ in range(4):
        both = [_dot(jnp.concatenate([w, t.astype(BF16)], axis=0), w) for w, t in zip(pw, tinv)]
        pw = [b[:n2, :].astype(BF16) for b in both]
        tinv = [t + b[n2:, :] for t, b in zip(tinv, both)]
    tinv = [_split16(t + _dot(t, w)) for t, w in zip(tinv, pw)]

    for ci in range(nch):
        sl = pl.ds(ci * c, c)
        for pi in range(npp):
            ls = slice(pi * LANE, (pi + 1) * LANE)
            u = pre[ci * npp + pi]
            s = st[pi]
            s16 = s.astype(BF16)
            vs_t16 = u['vs_t'][0]
            ut = _dot3(_split16(_dot(s16, u['kks_t']) + _dot(vs_t16, u['avk'])), tinv[ci * npp + pi])
            ut_p = _split16(ut)
            yt = _dot(s16, u['rs_t']) - _dot(ut_p[0], u['arb']) + _dot(vs_t16, u['ark'])
            upd_l = tuple(jnp.concatenate([-a, b], axis=1) for a, b in zip(ut_p, u['vs_t']))
            st[pi] = s * u['p_c'] + _dot3(upd_l, u['upd_r'])
            ys = yt.T
            y = ys[:c, :] + ys[c:, :]
            yc = y - head_mean(y)
            var = head_mean(yc * yc)
            y = yc * lax.rsqrt(var + RWKV_LNX_EPS) * lnw_ref[:, ls] + lnb_ref[:, ls] + bv_ref[sl, ls]
            y_ref[sl, ls] = y * g_ref[sl, ls]

    @pl.when(i == pl.num_programs(2) - 1)
    def _():
        sout_ref[...] = st[...]


def rwkv_scan(r, ld, k, v, kk, bb, g, bv, s0p, lnw, lnb):
    b, lp, d = r.shape
    npair = d // LANE
    nch = min(8, lp // RWKV_C)
    npp = min(npair, 16 // nch)
    rb = nch * RWKV_C
    row = lambda: pl.BlockSpec((None, rb, npp * LANE), lambda bi, p, i: (bi, i, p))
    st = lambda: pl.BlockSpec((None, npp, LANE, LANE), lambda bi, p, i: (bi, p, 0, 0))
    vec = lambda: pl.BlockSpec((1, npp * LANE), lambda bi, p, i: (0, p))
    return pl.pallas_call(
        functools.partial(_rwkv_scan_kernel, nch),
        grid=(b, npair // npp, lp // rb),
        in_specs=[row() for _ in range(8)] + [st(), vec(), vec()],
        out_specs=[row(), st()],
        out_shape=[jax.ShapeDtypeStruct((b, lp, d), F32),
                   jax.ShapeDtypeStruct((b, npair, LANE, LANE), F32)],
        scratch_shapes=[pltpu.VMEM((npp, LANE, LANE), F32)],
        compiler_params=_cparams(("arbitrary", "arbitrary", "arbitrary")),
        name="rwkv_scan",
    )(r, ld, k, v, kk, bb, g, bv, s0p, lnw, lnb)


def _pair_blockdiag(s):
    b = s.shape[0]
    s = s.reshape(b, 8, 2, RWKV_HD, RWKV_HD)
    z = jnp.zeros_like(s[:, :, 0])
    top = jnp.concatenate([s[:, :, 0], z], axis=-1)
    bot = jnp.concatenate([z, s[:, :, 1]], axis=-1)
    return jnp.concatenate([top, bot], axis=-2)


def _pair_unblock(sp):
    b = sp.shape[0]
    return jnp.stack([sp[:, :, :RWKV_HD, :RWKV_HD], sp[:, :, RWKV_HD:, RWKV_HD:]], axis=2).reshape(
        b, RWKV_HEADS, RWKV_HD, RWKV_HD)


CMP_HW = 4 * NSA_KVH * CMP_HIDDEN
NSEG_PAGE = PAGE // CMP_STRIDE


def _cmp_h_rows_kernel(nsr, xk_ref, xv_ref, wk_ref, wv_ref, o_ref):
    half = CMP_HW // 2
    acc_k = jnp.zeros((nsr, half), F32)
    acc_v = jnp.zeros((nsr, half), F32)
    for s in range(CMP_STRIDE):
        rows = pl.ds(s, nsr, stride=CMP_STRIDE)
        acc_k = acc_k + _dot(xk_ref[rows, :], wk_ref[s])
        acc_v = acc_v + _dot(xv_ref[rows, :], wv_ref[s])
    o_ref[...] = jnp.concatenate([acc_k, acc_v], axis=1)


def cmp_h_rows(rows, wk, wv, lt=0):
    b, t, _ = rows.shape
    tb = 2048 if t % 2048 == 0 else t
    nsr = tb // CMP_STRIDE
    wspec = pl.BlockSpec((CMP_STRIDE, LANE, CMP_HW // 2), lambda bi, i: (0, 0, 0))
    return pl.pallas_call(
        functools.partial(_cmp_h_rows_kernel, nsr),
        grid=(b, t // tb),
        in_specs=[pl.BlockSpec((None, tb, LANE), lambda bi, i: (bi, i, lt)),
                  pl.BlockSpec((None, tb, LANE), lambda bi, i: (bi, i, lt + 1)), wspec, wspec],
        out_specs=pl.BlockSpec((None, nsr, CMP_HW), lambda bi, i: (bi, i, 0)),
        out_shape=jax.ShapeDtypeStruct((b, t // CMP_STRIDE, CMP_HW), F32),
        compiler_params=_cparams(("arbitrary", "arbitrary")),
        name="cmp_h_rows",
    )(rows, rows, wk, wv)


PAGES_PER_STEP = 16
CMP_PAGES_PER_STEP = 32


def _cmp_h_pages_kernel(pt_ref, *refs):
    pp = CMP_PAGES_PER_STEP
    pk, pv = refs[:pp], refs[pp:2 * pp]
    wk_ref, wv_ref, o_ref = refs[2 * pp:]
    half = CMP_HW // 2
    acc_k = jnp.zeros((pp * NSEG_PAGE, half), F32)
    acc_v = jnp.zeros((pp * NSEG_PAGE, half), F32)
    for s in range(CMP_STRIDE):
        rows = pl.ds(s, NSEG_PAGE, stride=CMP_STRIDE)
        xk = jnp.concatenate([r[rows, :] for r in pk], axis=0)
        xv = jnp.concatenate([r[rows, :] for r in pv], axis=0)
        acc_k = acc_k + _dot(xk, wk_ref[s])
        acc_v = acc_v + _dot(xv, wv_ref[s])
    o_ref[...] = jnp.concatenate([acc_k, acc_v], axis=1)


def _page_specs(layer, lane_tile, n_pages, pp=PAGES_PER_STEP):
    def spec(p):
        def imap(bi, i, pt):
            return (layer, pt[bi, jnp.minimum(i * pp + p, n_pages - 1)], 0, lane_tile)
        return pl.BlockSpec((None, None, PAGE, LANE), imap)
    return [spec(p) for p in range(pp)]


def cmp_h_pages(cache, page_table, layer, wk, wv):
    b, n_pages = page_table.shape
    pp = CMP_PAGES_PER_STEP
    wspec = pl.BlockSpec((CMP_STRIDE, LANE, CMP_HW // 2), lambda bi, i, pt: (0, 0, 0))
    gs = pltpu.PrefetchScalarGridSpec(
        num_scalar_prefetch=1, grid=(b, n_pages // pp),
        in_specs=_page_specs(layer, 0, n_pages, pp) + _page_specs(layer, 1, n_pages, pp) + [wspec, wspec],
        out_specs=pl.BlockSpec((None, pp * NSEG_PAGE, CMP_HW), lambda bi, i, pt: (bi, i, 0)))
    return pl.pallas_call(
        _cmp_h_pages_kernel, grid_spec=gs,
        out_shape=jax.ShapeDtypeStruct((b, n_pages * NSEG_PAGE, CMP_HW), F32),
        compiler_params=_cparams(("arbitrary", "arbitrary")),
        name="cmp_h_pages",
    )(page_table, *([cache] * (2 * pp)), wk, wv)


NCP = 1024


def _cmp_finish_kernel(hn_row, hp_ref, hn_ref, pek_ref, pev_ref, wk_ref, wv_ref, w2k_ref, w2v_ref,
                       ckh_ref, ckl_ref, cvt_ref):
    qw = CMP_HW // 4
    rk = jnp.zeros((8, 2 * qw), F32)
    rv = jnp.zeros((8, 2 * qw), F32)
    for s in range(CMP_STRIDE):
        rk = rk + _dot(pek_ref[s], wk_ref[s])
        rv = rv + _dot(pev_ref[s], wv_ref[s])
    last = _iota((NCP, qw), 0) == hn_row

    def hidden(off, rb):
        h0 = hp_ref[:, off:off + qw]
        h1 = pltpu.roll(hp_ref[:, off + qw:off + 2 * qw], NCP - 1, 0)
        h1 = jnp.where(last, hn_ref[0:1, off + qw:off + 2 * qw], h1)
        return _silu(h0 + h1 + rb[0:1, :qw] + rb[1:2, qw:])

    ckh_ref[...], ckl_ref[...] = _split16(_dot(hidden(0, rk), w2k_ref[...]))
    cvt_ref[...] = _dot(hidden(2 * qw, rv), w2v_ref[...]).T.astype(BF16)


def cmp_finish(hp, hn, hn_row, pek, pev, wk, wv, w2k, w2v):
    b = hp.shape[0]
    full = lambda shp: pl.BlockSpec(shp, lambda bi: (0,) * len(shp))
    return pl.pallas_call(
        functools.partial(_cmp_finish_kernel, hn_row),
        grid=(b,),
        in_specs=[pl.BlockSpec((None, NCP, CMP_HW), lambda bi: (bi, 0, 0)),
                  pl.BlockSpec((None, 8, CMP_HW), lambda bi: (bi, 0, 0)),
                  full((CMP_STRIDE, 8, LANE)), full((CMP_STRIDE, 8, LANE)),
                  full((CMP_STRIDE, LANE, CMP_HW // 2)), full((CMP_STRIDE, LANE, CMP_HW // 2)),
                  full((CMP_HW // 4, LANE)), full((CMP_HW // 4, LANE))],
        out_specs=[pl.BlockSpec((None, NCP, LANE), lambda bi: (bi, 0, 0)),
                   pl.BlockSpec((None, NCP, LANE), lambda bi: (bi, 0, 0)),
                   pl.BlockSpec((None, LANE, NCP), lambda bi: (bi, 0, 0))],
        out_shape=[jax.ShapeDtypeStruct((b, NCP, LANE), BF16),
                   jax.ShapeDtypeStruct((b, NCP, LANE), BF16),
                   jax.ShapeDtypeStruct((b, LANE, NCP), BF16)],
        compiler_params=_cparams(("arbitrary",)),
        name="cmp_finish",
    )(hp, hn, pek, pev, wk, wv, w2k, w2v)


def _cmp_weights(pe, w1, w2):
    eye_g = jnp.eye(NSA_KVH, dtype=F32)

    def first(e):
        w = w1[e].reshape(2, CMP_STRIDE, NSA_HD, CMP_HIDDEN)
        w = jnp.einsum('isdf,gh->sgdihf', w, eye_g)
        return w.reshape(CMP_STRIDE, LANE, CMP_HW // 2).astype(BF16)

    def second(e):
        return jnp.einsum('fd,gh->gfhd', w2[e], eye_g).reshape(CMP_HW // 4, LANE).astype(BF16)

    def pos(e):
        p = pe[e].reshape(2, CMP_STRIDE, NSA_HD)
        p = jnp.tile(jnp.transpose(p, (1, 0, 2)), (1, 1, NSA_KVH))
        return jnp.pad(p, ((0, 0), (0, 6), (0, 0)))

    return pos(0), pos(1), first(0), first(1), second(0), second(1)


def _kvprep_rows_kernel(nsub, tk, xk_ref, xv_ref, k_ref, vt_ref):
    k_ref[...] = xk_ref[...].astype(BF16)
    for u in range(nsub):
        vt_ref[u] = xv_ref[u * tk:(u + 1) * tk, :].T.astype(BF16)


def kvprep_rows(rows, tk, lt=0):
    b, t, _ = rows.shape
    tt = 512 if t % 512 == 0 else tk
    nsub = tt // tk
    return pl.pallas_call(
        functools.partial(_kvprep_rows_kernel, nsub, tk),
        grid=(b, t // tt),
        in_specs=[pl.BlockSpec((None, tt, LANE), lambda bi, i: (bi, i, lt)),
                  pl.BlockSpec((None, tt, LANE), lambda bi, i: (bi, i, lt + 1))],
        out_specs=[pl.BlockSpec((None, tt, LANE), lambda bi, i: (bi, i, 0)),
                   pl.BlockSpec((None, nsub, LANE, tk), lambda bi, i: (bi, i, 0, 0))],
        out_shape=[jax.ShapeDtypeStruct((b, t, LANE), BF16),
                   jax.ShapeDtypeStruct((b, t // tk, LANE, tk), BF16)],
        compiler_params=_cparams(("arbitrary", "arbitrary")),
        name="kvprep_rows",
    )(rows, rows)


SEL_TK = 512
WIN_TK = 128


def _kvprep_pages_kernel(n_steps, pt_ref, *refs):
    pp = PAGES_PER_STEP
    pk, pv = refs[:pp], refs[pp:2 * pp]
    nk_ref, nv_ref, k_ref, vt_ref = refs[2 * pp:]
    i = pl.program_id(1)
    per = SEL_TK // PAGE

    @pl.when(i < n_steps - 1)
    def _():
        for p in range(pp):
            k_ref[p * PAGE:(p + 1) * PAGE, :] = pk[p][...].astype(BF16)
            vt_ref[p // per, :, (p % per) * PAGE:(p % per + 1) * PAGE] = pv[p][...].T.astype(BF16)

    @pl.when(i == n_steps - 1)
    def _():
        k_ref[...] = jnp.zeros(k_ref.shape, BF16)
        vt_ref[...] = jnp.zeros(vt_ref.shape, BF16)
        k_ref[0:PAGE, :] = nk_ref[...].astype(BF16)
        vt_ref[0, :, 0:PAGE] = nv_ref[...].T.astype(BF16)


def kvprep_pages(cache, page_table, layer, new_rows, lt0=0):
    b, n_pages = page_table.shape
    pp = PAGES_PER_STEP
    n_steps = n_pages // pp + 1
    t = n_steps * pp * PAGE
    new = lambda lt: pl.BlockSpec((None, PAGE, LANE), lambda bi, i, pt: (bi, 0, lt0 + lt))
    gs = pltpu.PrefetchScalarGridSpec(
        num_scalar_prefetch=1, grid=(b, n_steps),
        in_specs=_page_specs(layer, 0, n_pages) + _page_specs(layer, 1, n_pages) + [new(0), new(1)],
        out_specs=[pl.BlockSpec((None, pp * PAGE, LANE), lambda bi, i, pt: (bi, i, 0)),
                   pl.BlockSpec((None, pp * PAGE // SEL_TK, LANE, SEL_TK), lambda bi, i, pt: (bi, i, 0, 0))])
    return pl.pallas_call(
        functools.partial(_kvprep_pages_kernel, n_steps), grid_spec=gs,
        out_shape=[jax.ShapeDtypeStruct((b, t, LANE), BF16),
                   jax.ShapeDtypeStruct((b, t // SEL_TK, LANE, SEL_TK), BF16)],
        compiler_params=_cparams(("arbitrary", "arbitrary")),
        name="kvprep_pages",
    )(page_table, *([cache] * (2 * pp)), new_rows, new_rows)


TQ = 128
NQL = NSA_HEADS * TQ
NSA_CMP_TILE, NSA_SEL_TILE, NSA_WIN_TILE, NSA_GL_TILE = 8, 10, 12, 14
NSA_W = (NSA_GL_TILE + 1) * LANE
WIN_TILES = WINDOW // WIN_TK + 1


def _rowmax(x):
    return jnp.max(x, axis=0, keepdims=True)


def _rowsum(x):
    return jnp.sum(x, axis=0, keepdims=True)


LOG2E = 1.4426950408889634
SEL_PER = SEL_TK // SEL_BLOCK


def _real(m):
    return jnp.where(m > 0.5 * NEG, m, 0.0)


def _nsa_attn_kernel(pos0, nc, ns, wt0, wpos0, n_win,
                     q_ref, gl_ref, ckh_ref, ckl_ref, cvt_ref, ks_ref, vst_ref, *rest):
    kw = rest[:WIN_TILES]
    vw = rest[WIN_TILES:2 * WIN_TILES]
    msel_ref, o_ref, sel_scr, work_scr, sel3_scr, imp_scr, oc_scr, acc_scr, ml_scr, ot_scr = rest[2 * WIN_TILES:]
    i = pl.program_id(1)
    t0 = pos0 + i * TQ
    nsp = sel_scr.shape[0]
    hd = NSA_HD

    qt = (q_ref[...] * (hd ** -0.5 * LOG2E)).T
    zero = jnp.zeros((hd, TQ), F32)
    pieces = []
    for jh in range(NSA_HEADS):
        blk = qt[jh * hd:(jh + 1) * hd, :]
        pieces.append(jnp.concatenate([blk, zero] if jh < NSA_HPG else [zero, blk], axis=0))
    qbd = jnp.concatenate(pieces, axis=1)
    qbd16 = qbd.astype(BF16)
    t_q = t0 + _iota((1, TQ), 1)
    gw = NSA_HPG * TQ

    def heads(x, n=NSA_HPG):
        return jnp.concatenate([x] * n, axis=1)

    c_idx = _iota((NCP, TQ), 0)
    cbias = jnp.where((CMP_STRIDE * c_idx + (CMP_BLOCK - 1) <= t_q) & (c_idx < nc), 0.0, NEG)
    for g in range(NSA_KVH):
        gs = slice(g * gw, (g + 1) * gw)
        qh, ql = _split16(qbd[:, gs])
        ckh = ckh_ref[...]
        s = _dot(ckh, qh) + _dot(ckh, ql) + _dot(ckl_ref[...], qh) + heads(cbias)
        p = jnp.exp2(s - _real(_rowmax(s)))
        l = _rowsum(p)
        inv = 1.0 / jnp.where(l > 0.0, l, 1.0)
        oc_scr[g] = _dot(cvt_ref[g * hd:(g + 1) * hd, :], p) * inv
        pc = p * inv
        imp = pc[:, :TQ]
        for h in range(1, NSA_HPG):
            imp = imp + pc[:, h * TQ:(h + 1) * TQ]
        imp_scr[:, g * TQ:(g + 1) * TQ] = imp
    msel16 = msel_ref[...].astype(BF16)
    pslc = sum(_dot(msel16, piece) for piece in _split24(imp_scr[...]))

    jrow = _iota((nsp, 2 * TQ), 0)
    t_gq = t0 + (_iota((1, 2 * TQ), 1) % TQ)
    cur = lax.shift_right_logical(t_gq, 6)
    forced = (jrow == 0) | (jrow == cur) | (jrow == cur - 1)
    score = jnp.where(jrow * SEL_BLOCK <= t_gq, pslc + jnp.where(forced, FORCE_BONUS, 0.0), NEG)
    work_scr[...] = jnp.where(jrow < ns, score, -3e38)
    sel_scr[...] = jnp.zeros(sel_scr.shape, F32)
    jrow_f = jrow.astype(F32)

    def pick(_, carry):
        w = work_scr[...]
        best = _rowmax(w)
        first = jnp.min(jnp.where(w == best, jrow_f, 1e9), axis=0, keepdims=True)
        hit = jrow_f == first
        sel_scr[...] = jnp.where(hit, 1.0, sel_scr[...])
        work_scr[...] = jnp.where(hit, -jnp.inf, w)
        return carry

    lax.fori_loop(0, N_SEL, pick, 0)
    for u in range(nsp // SEL_PER):
        sel3_scr[u] = jnp.where(sel_scr[u * SEL_PER:(u + 1) * SEL_PER, :] > 0.5, 0.0, NEG)

    last = (t0 + TQ + SEL_TK - 1) // SEL_TK - 1
    ml_scr[0:1, :] = jnp.full((1, NQL), NEG, F32)
    ml_scr[1:2, :] = jnp.zeros((1, NQL), F32)
    acc_scr[...] = jnp.zeros(acc_scr.shape, F32)

    ones16 = jnp.ones((16, SEL_TK), BF16)

    def sel_tile(kt, extra):
        s = _dot(ks_ref[pl.ds(pl.multiple_of(kt * SEL_TK, SEL_TK), SEL_TK), :], qbd16).astype(BF16)
        sb = sel3_scr[kt].astype(BF16)
        s = jnp.concatenate(
            [s[jj * SEL_BLOCK:(jj + 1) * SEL_BLOCK, :]
             + jnp.concatenate([heads(sb[jj:jj + 1, :TQ]), heads(sb[jj:jj + 1, TQ:])], axis=1)
             for jj in range(SEL_PER)], axis=0)
        if extra is not None:
            s = s + extra
        m_old = ml_scr[0:1, :]
        m_new = jnp.maximum(m_old, _rowmax(s).astype(F32))
        alpha = jnp.exp2(m_old - m_new)
        p = jnp.exp2(s - m_new.astype(BF16))
        ml_scr[0:1, :] = m_new
        for g in range(NSA_KVH):
            gs = slice(g * gw, (g + 1) * gw)
            vt1 = jnp.concatenate([vst_ref[kt, g * hd:(g + 1) * hd, :], ones16], axis=0)
            pv = _dot(vt1, p[:, gs])
            acc_scr[g] = acc_scr[g] * alpha[:, gs] + pv[:hd, :]
            ml_scr[1:2, gs] = alpha[:, gs] * ml_scr[1:2, gs] + pv[hd:hd + 1, :]

    def sel_body(kt, carry):
        sel_tile(kt, None)
        return carry

    lax.fori_loop(0, last, sel_body, 0)
    krow = _iota((SEL_TK, TQ), 0)
    sel_tile(last, heads(jnp.where(last * SEL_TK + krow <= t_q, 0.0, NEG).astype(BF16), NSA_HEADS))
    l_sel = ml_scr[1:2, :]
    inv_sel = 1.0 / jnp.where(l_sel > 0.0, l_sel, 1.0)

    wrow = _iota((WIN_TK, TQ), 0)
    m = jnp.full((1, NQL), NEG, F32)
    l = jnp.zeros((1, NQL), F32)
    acc_w = [jnp.zeros((hd, gw), F32) for _ in range(NSA_KVH)]
    for w in range(WIN_TILES):
        tile = i + (wt0 - (WIN_TILES - 1) + w)
        idx = tile * WIN_TK + wrow
        wpos = wpos0 + idx
        dlt = t_q - wpos
        ok = (dlt >= 0) & (dlt < WINDOW) & (wpos >= 0) & (idx < n_win) & (tile >= 0)
        s = _dot(kw[w][...], qbd16).astype(BF16) + heads(jnp.where(ok, 0.0, NEG).astype(BF16), NSA_HEADS)
        m_new = jnp.maximum(m, _rowmax(s).astype(F32))
        m_use = _real(m_new)
        alpha = jnp.where(m > 0.5 * NEG, jnp.exp2(m - m_use), 0.0)
        p = jnp.exp2(s - m_use.astype(BF16))
        sums = []
        for g in range(NSA_KVH):
            gs = slice(g * gw, (g + 1) * gw)
            pv = _dot(jnp.concatenate([vw[w][g * hd:(g + 1) * hd, :], ones16[:, :WIN_TK]], axis=0), p[:, gs])
            acc_w[g] = acc_w[g] * alpha[:, gs] + pv[:hd, :]
            sums.append(pv[hd:hd + 1, :])
        l = alpha * l + jnp.concatenate(sums, axis=1)
        m = m_new
    inv_win = 1.0 / jnp.where(l > 0.0, l, 1.0)

    gt = _sigmoid(gl_ref[...]).T
    for jh in range(NSA_HEADS):
        g, h = divmod(jh, NSA_HPG)
        hl = slice(h * TQ, (h + 1) * TQ)
        ls = slice(jh * TQ, (jh + 1) * TQ)
        ot_scr[jh * hd:(jh + 1) * hd, :] = (
            gt[3 * jh:3 * jh + 1, :] * oc_scr[g, :, hl]
            + gt[3 * jh + 1:3 * jh + 2, :] * (acc_scr[g, :, hl] * inv_sel[:, ls])
            + gt[3 * jh + 2:3 * jh + 3, :] * (acc_w[g][:, hl] * inv_win[:, ls]))
    o_ref[...] = ot_scr[...].T


def nsa_attention(cols, ckh, ckl, cvt, ks, vst, kwin, vwt, pos0, n_new, n_buf, n_win):
    b, lq, _ = cols.shape
    q = gl = cols
    tk_total = ks.shape[1]
    n_sel_tiles = tk_total // SEL_TK
    nsp = tk_total // SEL_BLOCK
    t_total = pos0 + n_new
    nseg = -(-t_total // CMP_STRIDE)
    nc = nseg - CMP_BLOCK // CMP_STRIDE + 1
    ns = -(-t_total // SEL_BLOCK)
    wt0 = n_buf // WIN_TK
    n_wtiles = kwin.shape[1] // WIN_TK
    j = jnp.arange(nsp)[:, None]
    c = jnp.arange(NCP)[None, :]
    msel = ((c >= 4 * j - 1) & (c <= 4 * j + 3)).astype(F32)

    def wspec(w, vt):
        def imap(bi, i):
            tile = jnp.clip(i + (wt0 - (WIN_TILES - 1) + w), 0, n_wtiles - 1)
            return (bi, tile, 0, 0) if vt else (bi, tile, 0)
        return pl.BlockSpec((None, None, LANE, WIN_TK) if vt else (None, WIN_TK, LANE), imap)

    kern = functools.partial(_nsa_attn_kernel, pos0, nc, ns, wt0, pos0 - n_buf, n_win)
    return pl.pallas_call(
        kern,
        grid=(b, lq // TQ),
        in_specs=[pl.BlockSpec((None, TQ, NSA_DIM), lambda bi, i: (bi, i, 0)),
                  pl.BlockSpec((None, TQ, LANE), lambda bi, i: (bi, i, NSA_GL_TILE)),
                  pl.BlockSpec((None, NCP, LANE), lambda bi, i: (bi, 0, 0)),
                  pl.BlockSpec((None, NCP, LANE), lambda bi, i: (bi, 0, 0)),
                  pl.BlockSpec((None, LANE, NCP), lambda bi, i: (bi, 0, 0)),
                  pl.BlockSpec((None, tk_total, LANE), lambda bi, i: (bi, 0, 0)),
                  pl.BlockSpec((None, n_sel_tiles, LANE, SEL_TK), lambda bi, i: (bi, 0, 0, 0))]
                 + [wspec(w, False) for w in range(WIN_TILES)]
                 + [wspec(w, True) for w in range(WIN_TILES)]
                 + [pl.BlockSpec((nsp, NCP), lambda bi, i: (0, 0))],
        out_specs=pl.BlockSpec((None, TQ, NSA_DIM), lambda bi, i: (bi, i, 0)),
        out_shape=jax.ShapeDtypeStruct((b, lq, NSA_DIM), F32),
        scratch_shapes=[pltpu.VMEM((nsp, 2 * TQ), F32), pltpu.VMEM((nsp, 2 * TQ), F32),
                        pltpu.VMEM((nsp // SEL_PER, SEL_PER, 2 * TQ), F32),
                        pltpu.VMEM((NCP, 2 * TQ), F32),
                        pltpu.VMEM((NSA_KVH, NSA_HD, NSA_HPG * TQ), F32),
                        pltpu.VMEM((NSA_KVH, NSA_HD, NSA_HPG * TQ), F32),
                        pltpu.VMEM((8, NQL), F32), pltpu.VMEM((NSA_DIM, TQ), F32)],
        compiler_params=_cparams(("arbitrary", "arbitrary")),
        name="nsa_attention",
    )(q, gl, ckh, ckl, cvt, ks, vst, *([kwin] * WIN_TILES), *([vwt] * WIN_TILES), msel)


TQS = 8


def _nsa_attn_small_kernel(pos0, nc, ns, wpos0, n_win, q_ref, gl_ref, ckh_ref, ckl_ref, cvt_ref, ks_ref,
                           vst_ref, kw_ref, vw_ref, msel_ref, eh_ref, eht_ref, gx_ref, o_ref,
                           sel_scr, work_scr, sel3_scr):
    hd = NSA_HD
    nsp = sel_scr.shape[0]
    lane8 = _iota((TQS, LANE), 1)
    lane = _iota((1, LANE), 1)
    t_lane = pos0 + (lane % TQS)
    grp1 = lane >= NSA_HPG * TQS

    q = q_ref[...] * (hd ** -0.5 * LOG2E)
    pieces = []
    for jh in range(NSA_HEADS):
        tile = q[:, (jh // 2) * LANE:(jh // 2 + 1) * LANE]
        dst_hi = jh >= NSA_HPG
        if (jh % 2 == 1) != dst_hi:
            tile = pltpu.roll(tile, hd, 1)
        pieces.append(jnp.where((lane8 >= hd) if dst_hi else (lane8 < hd), tile, 0.0))
    z = jnp.concatenate(pieces, axis=0).T
    z16 = z.astype(BF16)
    zh, zl = _split16(z)

    def own_group(full):
        return jnp.where(grp1, full[hd:, :], full[:hd, :])

    c_idx = _iota((NCP, LANE), 0)
    ckh = ckh_ref[...]
    s = _dot(ckh, zh) + _dot(ckh, zl) + _dot(ckl_ref[...], zh)
    s = s + jnp.where((CMP_STRIDE * c_idx + (CMP_BLOCK - 1) <= t_lane) & (c_idx < nc), 0.0, NEG)
    p = jnp.exp2(s - _real(_rowmax(s)))
    l = _rowsum(p)
    inv = 1.0 / jnp.where(l > 0.0, l, 1.0)
    o_cmp = own_group(_dot(cvt_ref[...], p) * inv)
    pslc = _dot_hi(msel_ref[...], _dot_hi(p * inv, eht_ref[...]))

    jrow = _iota((nsp, LANE), 0)
    cur = lax.shift_right_logical(t_lane, 6)
    forced = (jrow == 0) | (jrow == cur) | (jrow == cur - 1)
    score = jnp.where(jrow * SEL_BLOCK <= t_lane, pslc + jnp.where(forced, FORCE_BONUS, 0.0), NEG)
    work_scr[...] = jnp.where(jrow < ns, score, -3e38)
    sel_scr[...] = jnp.zeros(sel_scr.shape, F32)
    jrow_f = jrow.astype(F32)

    def pick(_, carry):
        w = work_scr[...]
        best = _rowmax(w)
        first = jnp.min(jnp.where(w == best, jrow_f, 1e9), axis=0, keepdims=True)
        hit = jrow_f == first
        sel_scr[...] = jnp.where(hit, 1.0, sel_scr[...])
        work_scr[...] = jnp.where(hit, -jnp.inf, w)
        return carry

    lax.fori_loop(0, N_SEL, pick, 0)
    on = _dot(sel_scr[...], eh_ref[...])
    for u in range(nsp // SEL_PER):
        sel3_scr[u] = jnp.where(on[u * SEL_PER:(u + 1) * SEL_PER, :] > 0.5, 0.0, NEG)

    last = (pos0 + TQS + SEL_TK - 1) // SEL_TK - 1
    krow = _iota((SEL_TK, LANE), 0)

    def sel_tile(kt, carry, extra):
        m, l, acc = carry
        s = _dot(ks_ref[pl.ds(pl.multiple_of(kt * SEL_TK, SEL_TK), SEL_TK), :], z16)
        sb = sel3_scr[kt]
        s = jnp.concatenate([s[jj * SEL_BLOCK:(jj + 1) * SEL_BLOCK, :] + sb[jj:jj + 1, :]
                             for jj in range(SEL_PER)], axis=0)
        if extra is not None:
            s = s + extra
        m_new = jnp.maximum(m, _rowmax(s))
        alpha = jnp.exp2(m - m_new)
        p = jnp.exp2(s - m_new)
        return m_new, alpha * l + _rowsum(p), acc * alpha + _dot(vst_ref[kt], p)

    init = (jnp.full((1, LANE), NEG, F32), jnp.zeros((1, LANE), F32), jnp.zeros((2 * hd, LANE), F32))

    def two_tiles(u, carry):
        return sel_tile(2 * u, carry[0], None), sel_tile(2 * u + 1, carry[1], None)

    ca, cb = lax.fori_loop(0, last // 2, two_tiles, (init, init))
    if last % 2:
        ca = sel_tile(last - 1, ca, None)
    (ma, la, acca), (mb, lb, accb) = sel_tile(last, ca, jnp.where(last * SEL_TK + krow <= t_lane, 0.0, NEG)), cb
    m = jnp.maximum(ma, mb)
    fa, fb = jnp.exp2(ma - m), jnp.exp2(mb - m)
    l = la * fa + lb * fb
    o_sel = own_group((acca * fa + accb * fb) * (1.0 / jnp.where(l > 0.0, l, 1.0)))

    wrow = _iota((WIN_TK, LANE), 0)
    m = jnp.full((1, LANE), NEG, F32)
    l = jnp.zeros((1, LANE), F32)
    acc = jnp.zeros((2 * hd, LANE), F32)
    for w in range(kw_ref.shape[0] // WIN_TK):
        idx = w * WIN_TK + wrow
        wpos = wpos0 + idx
        dlt = t_lane - wpos
        ok = (dlt >= 0) & (dlt < WINDOW) & (wpos >= 0) & (idx < n_win)
        s = _dot(kw_ref[w * WIN_TK:(w + 1) * WIN_TK, :], z16) + jnp.where(ok, 0.0, NEG)
        m_new = jnp.maximum(m, _rowmax(s))
        alpha = jnp.where(m > 0.5 * NEG, jnp.exp2(m - _real(m_new)), 0.0)
        p = jnp.exp2(s - _real(m_new))
        l = alpha * l + _rowsum(p)
        acc = acc * alpha + _dot(vw_ref[w], p)
        m = m_new
    o_win = own_group(acc * (1.0 / jnp.where(l > 0.0, l, 1.0)))

    def to_rows(o):
        ot = jnp.concatenate([o, o], axis=0).T
        tiles = []
        for k in range(NSA_HEADS // 2):
            a = ot[(2 * k) * TQS:(2 * k + 1) * TQS, :]
            b = ot[(2 * k + 1) * TQS:(2 * k + 2) * TQS, :]
            tiles.append(jnp.where(lane8 < hd, a, b))
        return jnp.concatenate(tiles, axis=1)

    sg = _sigmoid(gl_ref[...])
    o_ref[...] = (_dot_hi(sg, gx_ref[0]) * to_rows(o_cmp) + _dot_hi(sg, gx_ref[1]) * to_rows(o_sel)
                  + _dot_hi(sg, gx_ref[2]) * to_rows(o_win))


def nsa_attention_small(cols, ckh, ckl, cvt, ks, vst, kwin, vwt, pos0, n_buf, n_win):
    b, lq, _ = cols.shape
    assert lq == TQS
    tk_total = ks.shape[1]
    nsp = tk_total // SEL_BLOCK
    t_total = pos0 + lq
    nc = -(-t_total // CMP_STRIDE) - CMP_BLOCK // CMP_STRIDE + 1
    ns = -(-t_total // SEL_BLOCK)
    j = jnp.arange(nsp)[:, None]
    c = jnp.arange(NCP)[None, :]
    msel = ((c >= 4 * j - 1) & (c <= 4 * j + 3)).astype(F32)
    n = jnp.arange(LANE)
    head, qi = n // TQS, n % TQS
    gq = (head // NSA_HPG) * TQS + qi
    eh = (jnp.arange(LANE)[:, None] == gq[None, :]).astype(F32)
    col = jnp.arange(NSA_DIM) // NSA_HD
    gx = jnp.stack([(jnp.arange(LANE)[:, None] == (3 * col + br)[None, :]).astype(F32) for br in range(3)])
    full = lambda a: pl.BlockSpec(a.shape, lambda bi: (0,) * a.ndim)
    per_b = lambda a: pl.BlockSpec((None,) + a.shape[1:], lambda bi: (bi,) + (0,) * (a.ndim - 1))
    kern = functools.partial(_nsa_attn_small_kernel, pos0, nc, ns, pos0 - n_buf, n_win)
    return pl.pallas_call(
        kern,
        grid=(b,),
        in_specs=[pl.BlockSpec((None, TQS, NSA_DIM), lambda bi: (bi, 0, 0)),
                  pl.BlockSpec((None, TQS, LANE), lambda bi: (bi, 0, NSA_GL_TILE)),
                  per_b(ckh), per_b(ckl), per_b(cvt), per_b(ks), per_b(vst), per_b(kwin), per_b(vwt),
                  full(msel), full(eh), full(eh), full(gx)],
        out_specs=pl.BlockSpec((None, TQS, NSA_DIM), lambda bi: (bi, 0, 0)),
        out_shape=jax.ShapeDtypeStruct((b, TQS, NSA_DIM), F32),
        scratch_shapes=[pltpu.VMEM((nsp, LANE), F32), pltpu.VMEM((nsp, LANE), F32),
                        pltpu.VMEM((nsp // SEL_PER, SEL_PER, LANE), F32)],
        compiler_params=_cparams(("arbitrary",)),
        name="nsa_attention_small",
    )(cols, cols, ckh, ckl, cvt, ks, vst, kwin, vwt, msel, eh, eh.T, gx)


def _layer_norm(v, g, b):
    mu = jnp.mean(v, axis=-1, keepdims=True)
    vc = v - mu
    var = jnp.mean(vc * vc, axis=-1, keepdims=True)
    return vc * lax.rsqrt(var + LN_EPS) * g + b


def _merge_kernel(x_ref, ya_ref, yb_ref, yc_ref, gate_ref, g1_ref, wa_ref, wb_ref, wc_ref, wo_ref,
                  lg_ref, lb_ref, o_ref):
    d = D_MODEL
    merged = (_sigmoid(gate_ref[:, :d]) * _dot(ya_ref[...], wa_ref[...])
              + _sigmoid(gate_ref[:, d:2 * d]) * _dot(yb_ref[...], wb_ref[...])
              + _sigmoid(gate_ref[:, 2 * d:]) * _dot(yc_ref[...], wc_ref[...]))
    o = _dot(merged, wo_ref[...])
    o_ref[...] = _layer_norm(ALPHA * x_ref[...] + g1_ref[...] * o, lg_ref[...], lb_ref[...])


def merge_out(x, ya, yb, yc, gate, g1, wa, wb, wc, wo, lg, lb, tm=256):
    m, d = x.shape
    tm = min(tm, m)
    per_row = g1.shape[0] != 1
    row = lambda w: pl.BlockSpec((tm, w), lambda i: (i, 0))
    mspec = row(d) if per_row else pl.BlockSpec((1, d), lambda i: (0, 0))
    wspec = pl.BlockSpec((d, d), lambda i: (0, 0))
    vspec = pl.BlockSpec((1, d), lambda i: (0, 0))
    return pl.pallas_call(
        _merge_kernel,
        grid=(m // tm,),
        in_specs=[row(d), row(d), row(d), row(d), row(3 * d), mspec, wspec, wspec, wspec, wspec, vspec, vspec],
        out_specs=row(d),
        out_shape=jax.ShapeDtypeStruct((m, d), F32),
        compiler_params=_cparams(("arbitrary",)),
        name="merge_out",
    )(x, ya, yb, yc, gate, g1, wa, wb, wc, wo, lg, lb)


def _lane_first(mask, lane_f):
    return jnp.min(jnp.where(mask, lane_f, 1e9), axis=-1, keepdims=True)


def _moe_kernel(x_ref, sc_ref, sh_ref, g2_ref, wr_ref, w1_ref, w3_ref, w2_ref, lg_ref, lb_ref,
                o_ref, u_scr, gate_scr, acc_scr):
    e = pl.program_id(1)
    tm = x_ref.shape[0]
    lane = _iota((tm, LANE), 1)
    lane_f = lane.astype(F32)

    @pl.when(e == 0)
    def _():
        u = x_ref[...] * (1.0 + sc_ref[...]) + sh_ref[...]
        u_scr[...] = u.astype(BF16)
        logits = _dot_hi(u, wr_ref[...])
        lg = jnp.where(lane < N_GROUPS, logits, -jnp.inf)
        gmax = jnp.max(lg, axis=-1, keepdims=True)
        gstar = _lane_first(lg == gmax, lane_f)
        pg = 1.0 / jnp.sum(jnp.exp(lg - gmax), axis=-1, keepdims=True)
        in_grp = (lane >= N_GROUPS) & (lane < N_GROUPS + N_EXPERTS) & (
            lax.shift_right_logical(lane - N_GROUPS, 2).astype(F32) == gstar)
        le = jnp.where(in_grp, logits, -jnp.inf)
        v1 = jnp.max(le, axis=-1, keepdims=True)
        i1 = _lane_first(le == v1, lane_f)
        le2 = jnp.where(lane_f == i1, -jnp.inf, le)
        v2 = jnp.max(le2, axis=-1, keepdims=True)
        i2 = _lane_first(le2 == v2, lane_f)
        e2 = jnp.exp(v2 - v1)
        den = 1.0 / (1.0 + e2)
        gate_scr[...] = jnp.where(lane_f == i1, den * pg, jnp.where(lane_f == i2, e2 * den * pg, 0.0))
        acc_scr[...] = jnp.zeros(acc_scr.shape, F32)

    u = u_scr[...]
    gate = gate_scr[...]
    out = acc_scr[...]
    for k in range(MOE_EPS):
        ge = jnp.sum(jnp.where(lane == e * MOE_EPS + (k + N_GROUPS), gate, 0.0), axis=-1, keepdims=True)
        h = _silu(_dot(u, w1_ref[k])) * _dot(u, w3_ref[k])
        out = out + _dot(h * ge, w2_ref[k])
    acc_scr[...] = out

    @pl.when(e == pl.num_programs(1) - 1)
    def _():
        o_ref[...] = _layer_norm(ALPHA * x_ref[...] + g2_ref[...] * acc_scr[...], lg_ref[...], lb_ref[...])


MOE_EPS = 1


def moe_out(x, sc, sh, g2, wr, w1, w3, w2, lg, lb, tm=512):
    m, d = x.shape
    tm = min(tm, m)
    per_row = sc.shape[0] != 1
    row = pl.BlockSpec((tm, d), lambda i, e: (i, 0))
    mspec = row if per_row else pl.BlockSpec((1, d), lambda i, e: (0, 0))
    vspec = pl.BlockSpec((1, d), lambda i, e: (0, 0))
    return pl.pallas_call(
        _moe_kernel,
        grid=(m // tm, N_EXPERTS // MOE_EPS),
        in_specs=[row, mspec, mspec, mspec, pl.BlockSpec((d, LANE), lambda i, e: (0, 0)),
                  pl.BlockSpec((MOE_EPS, d, EXPERT_HIDDEN), lambda i, e: (e, 0, 0)),
                  pl.BlockSpec((MOE_EPS, d, EXPERT_HIDDEN), lambda i, e: (e, 0, 0)),
                  pl.BlockSpec((MOE_EPS, EXPERT_HIDDEN, d), lambda i, e: (e, 0, 0)), vspec, vspec],
        out_specs=row,
        out_shape=jax.ShapeDtypeStruct((m, d), F32),
        scratch_shapes=[pltpu.VMEM((tm, d), BF16), pltpu.VMEM((tm, LANE), F32), pltpu.VMEM((tm, d), F32)],
        compiler_params=_cparams(("arbitrary", "arbitrary")),
        name="moe_out",
    )(x, sc, sh, g2, wr, w1, w3, w2, lg, lb)


def _pad_rows(a, n):
    return a if a.shape[1] == n else jnp.pad(a, ((0, 0), (0, n - a.shape[1]), (0, 0)))


def _layer_weights(l, p):
    w_in = p['w_in'][l]
    o1, o2, o3 = SSD_COLS, SSD_COLS + RWKV_COLS, SSD_COLS + RWKV_COLS + NSA_COLS
    padc = lambda w, n: jnp.pad(w, ((0, 0), (0, n - w.shape[1])))
    w_ssd = padc(w_in[:, :o1], SSD_W)
    w_nsa = padc(w_in[:, o2:o3], NSA_W)
    z64 = jnp.zeros((64, RWKV_DIM), F32)
    seg = _head_onehot(RWKV_HEADS, RWKV_HD, LANE).T
    rwkv_prm = (p['rwkv_mu'][l][None], p['rwkv_w0'][l][None], p['rwkv_a0'][l][None], p['rwkv_k_k'][l][None],
                p['rwkv_k_a'][l][None], p['rwkv_r_k'][l].reshape(1, RWKV_DIM),
                jnp.concatenate([p['rwkv_w2'][l], z64], 0), jnp.concatenate([z64, p['rwkv_a2'][l]], 0),
                p['rwkv_g2'][l], seg, seg.T)
    wr = jnp.pad(jnp.concatenate([p['w_group'][l], p['w_router'][l]], axis=1),
                 ((0, 0), (0, LANE - N_GROUPS - N_EXPERTS)))
    row = lambda v: v[None]
    return dict(
        w_ssd=w_ssd.astype(BF16), w_rwkv=w_in[:, o1:o2].astype(BF16), w_nsa=w_nsa.astype(BF16),
        w_gate=w_in[:, o3:].astype(BF16),
        ssd=(jnp.pad(p['ssd_conv_w'][l], ((0, 4), (0, 0))), row(p['ssd_conv_b'][l]),
             row(jnp.pad(p['ssd_dt_bias'][l], (0, LANE - SSD_HEADS))), row(jnp.pad(p['ssd_a_log'][l], (0, LANE - SSD_HEADS))),
             row(jnp.repeat(p['ssd_d'][l], SSD_HD)), row(p['ssd_norm_w'][l])),
        rwkv=rwkv_prm, lnx=(row(p['rwkv_lnx_w'][l]), row(p['rwkv_lnx_b'][l])),
        cmp=_cmp_weights(p['cmp_pe'][l], p['cmp_w1'][l], p['cmp_w2'][l]),
        wo=tuple(p[k][l].astype(BF16) for k in ('w_o_ssd', 'w_o_rwkv', 'w_o_nsa', 'w_out')),
        ln1=(row(p['ln1_g'][l]), row(p['ln1_b'][l])), ln2=(row(p['ln2_g'][l]), row(p['ln2_b'][l])),
        wr=wr, w1=p['moe_w1'][l].astype(BF16), w3=p['moe_w3'][l].astype(BF16), w2=p['moe_w2'][l].astype(BF16))


def _trunk_layer(x, mod, l, w, pos0, conv_st, ssm_st, shift_st, rwkv_st, cache_win, cache_cmp, cache_sel,
                 page_table):
    b, L, d = x.shape
    m = b * L
    x2 = x.reshape(m, d)
    sh1, sc1, g1, sh2, sc2, g2 = [mod[:, k * d:(k + 1) * d] for k in range(6)]
    if b > 1:
        sh1, sc1, g1, sh2, sc2, g2 = [jnp.repeat(t, L, axis=0) for t in (sh1, sc1, g1, sh2, sc2, g2)]
    c_ssd, c_rwkv, c_nsa, c_gate = (
        c.reshape(b, L, -1) for c in mod_proj(x2, sc1, sh1, [w['w_ssd'], w['w_rwkv'], w['w_nsa'], w['w_gate']]))

    lp = -(-L // SSD_Q) * SSD_Q
    cst8 = jnp.pad(conv_st, ((0, 0), (8 - (SSD_CONV - 1), 0), (0, 0)))
    h0t = jnp.transpose(ssm_st.reshape(b, SSD_INNER, SSD_STATE), (0, 2, 1))
    y_a, ht = ssd_mixer(_pad_rows(c_ssd, lp), cst8, h0t, L, *w['ssd'])
    y_a = y_a[:, :L]
    ssm_new = jnp.transpose(ht, (0, 2, 1)).reshape(b, SSD_HEADS, SSD_HD, SSD_STATE)
    conv_new = c_ssd[:, L - (SSD_CONV - 1):, SSD_INNER:SSD_INNER + SSD_CONV_DIM]

    lp = -(-L // RWKV_C) * RWKV_C
    sh8 = jnp.pad(shift_st[:, None, :], ((0, 0), (7, 0), (0, 0)))
    pre = rwkv_prep(_pad_rows(c_rwkv, lp), sh8, L, w['rwkv'])
    y_b, sp = rwkv_scan(*pre, _pair_blockdiag(rwkv_st), *w['lnx'])
    y_b = y_b[:, :L]
    rwkv_new = _pair_unblock(sp)
    shift_new = c_rwkv[:, -1]

    pek, pev, wk, wv, w2k, w2v = w['cmp']
    kvrow = (2, NSA_KVH, NSA_HD)
    rows = lambda tile: c_nsa[:, :, tile * LANE:(tile + 2) * LANE]
    if cache_cmp is None:
        hp = cmp_h_rows(c_nsa, wk, wv, NSA_CMP_TILE)
        hp = _pad_rows(hp, NCP)
        hn = jnp.zeros((b, 8, CMP_HW), F32)
        ks, vst = kvprep_rows(c_nsa, SEL_TK, NSA_SEL_TILE)
        kwin, vwt = kvprep_rows(c_nsa, WIN_TK, NSA_WIN_TILE)
        cq, n_buf, n_win = c_nsa, 0, L
        win_new = rows(NSA_WIN_TILE)[:, max(L - WINDOW, 0):]
    else:
        cq = _pad_rows(c_nsa, TQ)
        hp = _pad_rows(cmp_h_pages(cache_cmp, page_table, l, wk, wv), NCP)
        hn = cmp_h_rows(cq, wk, wv, NSA_CMP_TILE)
        ks, vst = kvprep_pages(cache_sel, page_table, l, cq, NSA_SEL_TILE)
        n_buf = cache_win.shape[1]
        win_all = jnp.concatenate([cache_win.reshape(b, n_buf, 2 * LANE), rows(NSA_WIN_TILE)], axis=1)
        n_win = n_buf + L
        kwin, vwt = kvprep_rows(_pad_rows(win_all, -(-(n_buf + TQ) // WIN_TK) * WIN_TK), WIN_TK)
        win_new = win_all[:, n_win - min(WINDOW, n_win):]
    hn_row = NCP - 1 if cache_cmp is None else pos0 // CMP_STRIDE - 1
    ckh, ckl, cvt = cmp_finish(hp, hn, hn_row, pek, pev, wk, wv, w2k, w2v)
    if L == TQS:
        y_c = nsa_attention_small(c_nsa, ckh, ckl, cvt, ks, vst, kwin, vwt, pos0, n_buf, n_win)
    else:
        y_c = nsa_attention(cq, ckh, ckl, cvt, ks, vst, kwin, vwt, pos0, L, n_buf, n_win)[:, :L]
    cmp_rows = rows(NSA_CMP_TILE).reshape((b, L) + kvrow)
    sel_rows = rows(NSA_SEL_TILE).reshape((b, L) + kvrow)
    win_new = win_new.reshape(win_new.shape[:2] + kvrow)

    flat = lambda t: t.reshape(m, -1)
    x1 = merge_out(x2, flat(y_a), flat(y_b), flat(y_c), flat(c_gate), g1, *w['wo'], *w['ln1'])
    x_out = moe_out(x1, sc2, sh2, g2, w['wr'], w['w1'], w['w3'], w['w2'], *w['ln2'])
    return x_out.reshape(b, L, d), (cmp_rows, sel_rows, win_new, ssm_new, conv_new, rwkv_new, shift_new)


def kernel(x_prompt, x_sample, c_prompt, c_sample, cache_cmp, cache_sel, cache_win, state_ssm, state_ssm_conv,
           state_rwkv, state_rwkv_shift, page_table, w_ada, b_ada, w_in, ssd_conv_w, ssd_conv_b, ssd_dt_bias,
           ssd_a_log, ssd_d, ssd_norm_w, rwkv_mu, rwkv_w0, rwkv_w2, rwkv_a0, rwkv_a2, rwkv_g2, rwkv_k_k, rwkv_k_a,
           rwkv_r_k, rwkv_lnx_w, rwkv_lnx_b, cmp_pe, cmp_w1, cmp_w2, w_o_ssd, w_o_rwkv, w_o_nsa, w_out, ln1_g,
           ln1_b, ln2_g, ln2_b, w_group, w_router, moe_w1, moe_w3, moe_w2):
    p = dict(w_in=w_in, ssd_conv_w=ssd_conv_w, ssd_conv_b=ssd_conv_b, ssd_dt_bias=ssd_dt_bias, ssd_a_log=ssd_a_log,
             ssd_d=ssd_d, ssd_norm_w=ssd_norm_w, rwkv_mu=rwkv_mu, rwkv_w0=rwkv_w0, rwkv_w2=rwkv_w2, rwkv_a0=rwkv_a0,
             rwkv_a2=rwkv_a2, rwkv_g2=rwkv_g2, rwkv_k_k=rwkv_k_k, rwkv_k_a=rwkv_k_a, rwkv_r_k=rwkv_r_k,
             rwkv_lnx_w=rwkv_lnx_w, rwkv_lnx_b=rwkv_lnx_b, cmp_pe=cmp_pe, cmp_w1=cmp_w1, cmp_w2=cmp_w2,
             w_o_ssd=w_o_ssd, w_o_rwkv=w_o_rwkv, w_o_nsa=w_o_nsa, w_out=w_out, ln1_g=ln1_g, ln1_b=ln1_b,
             ln2_g=ln2_g, ln2_b=ln2_b, w_group=w_group, w_router=w_router, moe_w1=moe_w1, moe_w3=moe_w3,
             moe_w2=moe_w2)
    bp, bs = x_prompt.shape[0], x_sample.shape[0]
    past_len = page_table.shape[1] * PAGE
    nb = -(-(bp + bs) // SUBLANE) * SUBLANE
    c_all = jnp.pad(jnp.concatenate([c_prompt, c_sample], axis=0), ((0, nb - bp - bs), (0, 0)))
    mod = ada_mod(c_all, w_ada, b_ada)
    n_phys = cache_cmp.shape[1]
    cmp_pages = cache_cmp.reshape(DEPTH, n_phys, PAGE, 2 * LANE)
    sel_pages = cache_sel.reshape(DEPTH, n_phys, PAGE, 2 * LANE)
    zeros = lambda *s: jnp.zeros(s, F32)
    xp, xs = x_prompt, x_sample
    st_p, st_s = [], []
    for l in range(DEPTH):
        w = _layer_weights(l, p)
        xp, sp_l = _trunk_layer(xp, mod[l, :bp], l, w, 0, zeros(bp, SSD_CONV - 1, SSD_CONV_DIM),
                                zeros(bp, SSD_HEADS, SSD_HD, SSD_STATE), zeros(bp, RWKV_COLS),
                                zeros(bp, RWKV_HEADS, RWKV_HD, RWKV_HD), None, None, None, None)
        xs, ss_l = _trunk_layer(xs, mod[l, bp:bp + bs], l, w, past_len, state_ssm_conv[l], state_ssm[l],
                                state_rwkv_shift[l], state_rwkv[l], cache_win[l], cmp_pages, sel_pages, page_table)
        st_p.append(sp_l)
        st_s.append(ss_l)
    sp = [jnp.stack(z) for z in zip(*st_p)]
    ss = [jnp.stack(z) for z in zip(*st_s)]
    return (xp, xs, sp[0], sp[1], sp[2], sp[3], sp[4], sp[5], sp[6], ss[0], ss[1], ss[2], ss[3], ss[4], ss[5], ss[6])
```
